```python
import math
import jax, jax.numpy as jnp
from jax import lax
import numpy as np

D_MODEL = 2048
BATCH = 32
SEQ = 256
DEPTH = 2
DEC_BATCH = 4
DEC_SEQ = 1024
PAST_LEN = 512

GRID_W = 64
N_EVEN = (DEPTH + 1) // 2
N_ODD = DEPTH // 2
QBLK = 128
ROPE_BASE = 10000.0
EPS = 1e-6
NEG_INF = -1e30

DA_HEADS = 8
DA_QK_DIM = 64
DA_V_DIM = 2 * DA_QK_DIM
MLA_HEADS = 8
MLA_Q_RANK = 512
MLA_KV_RANK = 512
MLA_NOPE = 128
MLA_ROPE = 64
MLA_V = 128
MLA_QK = MLA_NOPE + MLA_ROPE
GQ_HEADS = 16
GQ_KV_HEADS = 4
GQ_GROUP = GQ_HEADS // GQ_KV_HEADS
GQ_DIM = 128
WINDOW = 128
D_FF = 4 * D_MODEL

DA_QW = DA_HEADS * 2 * DA_QK_DIM
DA_VW = DA_HEADS * DA_V_DIM
AB_SPLITS = [DA_QW, 2 * DA_QW, 2 * DA_QW + DA_VW,
             2 * DA_QW + DA_VW + MLA_Q_RANK,
             2 * DA_QW + DA_VW + MLA_Q_RANK + MLA_KV_RANK]
AB_IN = 2 * DA_QW + DA_VW + MLA_Q_RANK + MLA_KV_RANK + MLA_ROPE
AB_OUT = DA_VW + MLA_HEADS * MLA_V
C_SPLITS = [GQ_HEADS * GQ_DIM, GQ_HEADS * GQ_DIM + GQ_KV_HEADS * GQ_DIM]
C_IN = GQ_HEADS * GQ_DIM + 2 * GQ_KV_HEADS * GQ_DIM
C_OUT = GQ_HEADS * GQ_DIM

kernel_name = "hybrid_diffusion_prefix_ctx_step"


def rmsnorm(x, g):
    xf = x.astype(jnp.float32)
    y = xf * lax.rsqrt(jnp.mean(xf * xf, axis=-1, keepdims=True) + EPS)
    return (y * g.astype(jnp.float32)).astype(x.dtype)


def rope_1d(x, pos):
    half = x.shape[-1] // 2
    inv = ROPE_BASE ** (-jnp.arange(half, dtype=jnp.float32) / half)
    ang = pos.astype(jnp.float32)[:, None] * inv
    cos = jnp.cos(ang)[:, None, :].astype(x.dtype)
    sin = jnp.sin(ang)[:, None, :].astype(x.dtype)
    x1, x2 = x[..., :half], x[..., half:]
    return jnp.concatenate([x1 * cos - x2 * sin, x2 * cos + x1 * sin], axis=-1)


def axial_rope(x, pos):
    a = x.shape[-1] // 2
    return jnp.concatenate([rope_1d(x[..., :a], pos[0]), rope_1d(x[..., a:], pos[1])], axis=-1)


def to_blocks(x):
    b, s = x.shape[:2]
    return jnp.swapaxes(x.reshape(b, s // QBLK, QBLK, *x.shape[2:]), 0, 1)


def from_blocks(y):
    nb, b, q = y.shape[:3]
    return jnp.swapaxes(y, 0, 1).reshape(b, nb * q, *y.shape[3:])


def diff_attention(q1, q2, k1, k2, v, lam):
    scale = DA_QK_DIM ** -0.5

    def block(qs):
        b1, b2 = qs
        p1 = jax.nn.softmax(jnp.einsum('bqhd,bkhd->bhqk', b1, k1).astype(jnp.float32) * scale, axis=-1)
        p2 = jax.nn.softmax(jnp.einsum('bqhd,bkhd->bhqk', b2, k2).astype(jnp.float32) * scale, axis=-1)
        a = (p1 - lam * p2).astype(v.dtype)
        return jnp.einsum('bhqk,bkhd->bqhd', a, v)

    return from_blocks(lax.map(block, (to_blocks(q1), to_blocks(q2))))


def softmax_attention(q, k, v):
    scale = q.shape[-1] ** -0.5

    def block(qb):
        p = jax.nn.softmax(jnp.einsum('bqhd,bkhd->bhqk', qb, k).astype(jnp.float32) * scale, axis=-1)
        return jnp.einsum('bhqk,bkhd->bqhd', p.astype(v.dtype), v)

    return from_blocks(lax.map(block, to_blocks(q)))


def sink_window_attention(q, k_ctx, v_ctx, sink, k_lat=None, v_lat=None):
    b, s = q.shape[:2]
    lc = k_ctx.shape[1]
    scale = GQ_DIM ** -0.5
    band = QBLK + 2 * WINDOW
    qg = q.reshape(b, s, GQ_KV_HEADS, GQ_GROUP, GQ_DIM)
    sink_col = sink.astype(jnp.float32).reshape(1, GQ_KV_HEADS, GQ_GROUP, 1, 1)
    if k_lat is not None:
        pad = ((0, 0), (WINDOW, WINDOW), (0, 0), (0, 0))
        kp = jnp.pad(k_lat, pad)
        vp = jnp.pad(v_lat, pad)

    def block(args):
        qb, bi = args
        s_ctx = jnp.einsum('bqhgd,bkhd->bhgqk', qb, k_ctx).astype(jnp.float32) * scale
        sinks = jnp.broadcast_to(sink_col, s_ctx.shape[:-1] + (1,))
        if k_lat is None:
            p = jax.nn.softmax(jnp.concatenate([s_ctx, sinks], axis=-1), axis=-1)
            return jnp.einsum('bhgqk,bkhd->bqhgd', p[..., :lc].astype(v_ctx.dtype), v_ctx)
        kb = lax.dynamic_slice_in_dim(kp, bi * QBLK, band, axis=1)
        vb = lax.dynamic_slice_in_dim(vp, bi * QBLK, band, axis=1)
        s_loc = jnp.einsum('bqhgd,bkhd->bhgqk', qb, kb).astype(jnp.float32) * scale
        qi = bi * QBLK + jnp.arange(QBLK)
        kj = bi * QBLK - WINDOW + jnp.arange(band)
        valid = (jnp.abs(qi[:, None] - kj[None, :]) <= WINDOW) & (kj >= 0)[None, :] & (kj < s)[None, :]
        s_loc = jnp.where(valid, s_loc, NEG_INF)
        p = jax.nn.softmax(jnp.concatenate([s_ctx, s_loc, sinks], axis=-1), axis=-1)
        o = jnp.einsum('bhgqk,bkhd->bqhgd', p[..., :lc].astype(v_ctx.dtype), v_ctx)
        return o + jnp.einsum('bhgqk,bkhd->bqhgd', p[..., lc:lc + band].astype(vb.dtype), vb)

    out = lax.map(block, (to_blocks(qg), jnp.arange(s // QBLK)))
    return from_blocks(out).reshape(b, s, GQ_HEADS * GQ_DIM)


def mla_keys_values(ckv, kr, w_kv_up, k_norm):
    b, l = ckv.shape[:2]
    kv = (ckv @ w_kv_up).reshape(b, l, MLA_HEADS, MLA_NOPE + MLA_V)
    k_nope, v = kv[..., :MLA_NOPE], kv[..., MLA_NOPE:]
    k_rope = jnp.broadcast_to(kr[:, :, None, :], (b, l, MLA_HEADS, MLA_ROPE))
    k = rmsnorm(jnp.concatenate([k_nope, k_rope], axis=-1), k_norm)
    return k, v


def rope_tail(t, pos):
    return jnp.concatenate([t[..., :MLA_NOPE], axial_rope(t[..., MLA_NOPE:], pos)], axis=-1)


def mixer_ab(h, pos, ctx, w_in, w_out, lq1, lk1, lq2, lk2, da_qn, da_kn, da_subln, lam_init,
             mq_norm, w_q_up, mkv_norm, w_kv_up, mla_qn, mla_kn):
    b, s, _ = h.shape
    dq, dk, dv, mq, mkv, mkr = jnp.split(h @ w_in, AB_SPLITS, axis=-1)
    dq = rmsnorm(dq.reshape(b, s, DA_HEADS, 2, DA_QK_DIM), da_qn)
    dk = rmsnorm(dk.reshape(b, s, DA_HEADS, 2, DA_QK_DIM), da_kn)
    dv = dv.reshape(b, s, DA_HEADS, DA_V_DIM)
    q1, q2, k1, k2 = dq[:, :, :, 0], dq[:, :, :, 1], dk[:, :, :, 0], dk[:, :, :, 1]
    q = rmsnorm((rmsnorm(mq, mq_norm) @ w_q_up).reshape(b, s, MLA_HEADS, MLA_QK), mla_qn)
    ckv = rmsnorm(mkv, mkv_norm)
    k, v = mla_keys_values(ckv, mkr, w_kv_up, mla_kn)
    if ctx is None:
        new = (jnp.concatenate([k1, k2], axis=-1), dv, ckv, mkr)
    else:
        c_dak, c_dav, c_ckv, c_kr = ctx
        q1, q2, k1, k2 = [axial_rope(t, pos) for t in (q1, q2, k1, k2)]
        k1 = jnp.concatenate([c_dak[..., :DA_QK_DIM], k1], axis=1)
        k2 = jnp.concatenate([c_dak[..., DA_QK_DIM:], k2], axis=1)
        dv = jnp.concatenate([c_dav, dv], axis=1)
        q = rope_tail(q, pos)
        ck, cv = mla_keys_values(c_ckv, c_kr, w_kv_up, mla_kn)
        k = jnp.concatenate([ck, rope_tail(k, pos)], axis=1)
        v = jnp.concatenate([cv, v], axis=1)
        new = None
    f32 = jnp.float32
    lam = (jnp.exp(jnp.sum(lq1.astype(f32) * lk1.astype(f32)))
           - jnp.exp(jnp.sum(lq2.astype(f32) * lk2.astype(f32))) + lam_init)
    o_da = rmsnorm(diff_attention(q1, q2, k1, k2, dv, lam), da_subln) * (1.0 - lam_init)
    o_mla = softmax_attention(q, k, v)
    o = jnp.concatenate([o_da.reshape(b, s, DA_VW), o_mla.reshape(b, s, MLA_HEADS * MLA_V)], axis=-1)
    return o @ w_out, new


def mixer_c(h, pos, ctx, w_in, w_out, qn, kn, sink):
    b, s, _ = h.shape
    q, k, v = jnp.split(h @ w_in, C_SPLITS, axis=-1)
    q = rmsnorm(q.reshape(b, s, GQ_HEADS, GQ_DIM), qn)
    k = rmsnorm(k.reshape(b, s, GQ_KV_HEADS, GQ_DIM), kn)
    v = v.reshape(b, s, GQ_KV_HEADS, GQ_DIM)
    if ctx is None:
        o = sink_window_attention(q, k, v, sink)
        new = (k, v)
    else:
        o = sink_window_attention(axial_rope(q, pos), ctx[0], ctx[1], sink,
                                  axial_rope(k, pos), v)
        new = None
    return o @ w_out, new


def sqrelu_mlp(h, w1, w2):
    return jnp.square(jax.nn.relu(h @ w1)) @ w2


def run_trunk(x, cond, pos, cache, P):
    new = {"da_k": [], "da_v": [], "mla_ckv": [], "mla_krope": [], "gq_k": [], "gq_v": []}
    for l in range(DEPTH):
        m = (jax.nn.silu(cond) @ P["ada_w"][l] + P["ada_b"][l]).reshape(-1, 1, 6 * D_MODEL)
        sh1, sc1, g1, sh2, sc2, g2 = jnp.split(m, 6, axis=-1)
        h = rmsnorm(x, P["norm1_g"][l]) * (1 + sc1) + sh1
        i = l // 2
        if l % 2 == 0:
            ctx = None if cache is None else (cache["da_k"][:, i], cache["da_v"][:, i],
                                              cache["mla_ckv"][:, i], cache["mla_krope"][:, i])
            y, nc = mixer_ab(h, pos, ctx, P["ab_w_in"][i], P["ab_w_out"][i],
                             P["da_lambda_q1"][i], P["da_lambda_k1"][i],
                             P["da_lambda_q2"][i], P["da_lambda_k2"][i],
                             P["da_q_norm"][i], P["da_k_norm"][i], P["da_subln"][i],
                             0.8 - 0.6 * math.exp(-0.3 * l),
                             P["mla_q_a_norm"][i], P["mla_w_q_up"][i],
                             P["mla_kv_a_norm"][i], P["mla_w_kv_up"][i],
                             P["mla_q_norm"][i], P["mla_k_norm"][i])
            if nc is not None:
                for name, t in zip(("da_k", "da_v", "mla_ckv", "mla_krope"), nc):
                    new[name].append(t)
        else:
            ctx = None if cache is None else (cache["gq_k"][:, i], cache["gq_v"][:, i])
            y, nc = mixer_c(h, pos, ctx, P["c_w_in"][i], P["c_w_out"][i],
                            P["gq_q_norm"][i], P["gq_k_norm"][i], P["gq_sink"][i])
            if nc is not None:
                new["gq_k"].append(nc[0])
                new["gq_v"].append(nc[1])
        x = x + g1 * y
        h = rmsnorm(x, P["norm2_g"][l]) * (1 + sc2) + sh2
        x = x + g2 * sqrelu_mlp(h, P["ff1_w"][l], P["ff2_w"][l])
    return x, new


def setup_inputs(seed: int = 0) -> dict:
    key = jax.random.key(seed)
    ks = iter(jax.random.split(key, 40))
    f32 = jnp.float32

    def nrm(shape, scale=1.0):
        return jax.random.normal(next(ks), shape, f32) * scale

    def gain(shape):
        return 1.0 + 0.02 * jax.random.normal(next(ks), shape, f32)

    D = D_MODEL
    return {
        "x_prompt": nrm((BATCH, SEQ, D)),
        "x_sample": nrm((DEC_BATCH, DEC_SEQ, D)),
        "cache_da_k": nrm((DEC_BATCH, N_EVEN, PAST_LEN, DA_HEADS, 2 * DA_QK_DIM)),
        "cache_da_v": nrm((DEC_BATCH, N_EVEN, PAST_LEN, DA_HEADS, DA_V_DIM)),
        "cache_mla_ckv": nrm((DEC_BATCH, N_EVEN, PAST_LEN, MLA_KV_RANK)),
        "cache_mla_krope": nrm((DEC_BATCH, N_EVEN, PAST_LEN, MLA_ROPE)),
        "cache_gq_k": nrm((DEC_BATCH, N_ODD, PAST_LEN, GQ_KV_HEADS, GQ_DIM)),
        "cache_gq_v": nrm((DEC_BATCH, N_ODD, PAST_LEN, GQ_KV_HEADS, GQ_DIM)),
        "c": nrm((DEC_BATCH, D)),
        "c_ctx": nrm((D,)),
        "norm1_g": gain((DEPTH, D)),
        "norm2_g": gain((DEPTH, D)),
        "ada_w": nrm((DEPTH, D, 6 * D), 0.5 * D ** -0.5),
        "ada_b": nrm((DEPTH, 6 * D), 0.02),
        "ff1_w": nrm((DEPTH, D, D_FF), D ** -0.5),
        "ff2_w": nrm((DEPTH, D_FF, D), D_FF ** -0.5),
        "ab_w_in": nrm((N_EVEN, D, AB_IN), D ** -0.5),
        "ab_w_out": nrm((N_EVEN, AB_OUT, D), AB_OUT ** -0.5),
        "da_lambda_q1": nrm((N_EVEN, DA_QK_DIM), 0.1),
        "da_lambda_k1": nrm((N_EVEN, DA_QK_DIM), 0.1),
        "da_lambda_q2": nrm((N_EVEN, DA_QK_DIM), 0.1),
        "da_lambda_k2": nrm((N_EVEN, DA_QK_DIM), 0.1),
        "da_q_norm": gain((N_EVEN, DA_QK_DIM)),
        "da_k_norm": gain((N_EVEN, DA_QK_DIM)),
        "da_subln": gain((N_EVEN, DA_V_DIM)),
        "mla_q_a_norm": gain((N_EVEN, MLA_Q_RANK)),
        "mla_w_q_up": nrm((N_EVEN, MLA_Q_RANK, MLA_HEADS * MLA_QK), MLA_Q_RANK ** -0.5),
        "mla_kv_a_norm": gain((N_EVEN, MLA_KV_RANK)),
        "mla_w_kv_up": nrm((N_EVEN, MLA_KV_RANK, MLA_HEADS * (MLA_NOPE + MLA_V)), MLA_KV_RANK ** -0.5),
        "mla_q_norm": gain((N_EVEN, MLA_QK)),
        "mla_k_norm": gain((N_EVEN, MLA_QK)),
        "c_w_in": nrm((N_ODD, D, C_IN), D ** -0.5),
        "c_w_out": nrm((N_ODD, C_OUT, D), C_OUT ** -0.5),
        "gq_q_norm": gain((N_ODD, GQ_DIM)),
        "gq_k_norm": gain((N_ODD, GQ_DIM)),
        "gq_sink": nrm((N_ODD, GQ_HEADS), 0.5),
    }


def reference(x_prompt, x_sample, cache_da_k, cache_da_v, cache_mla_ckv, cache_mla_krope,
              cache_gq_k, cache_gq_v, c, c_ctx, norm1_g, norm2_g, ada_w, ada_b, ff1_w, ff2_w,
              ab_w_in, ab_w_out, da_lambda_q1, da_lambda_k1, da_lambda_q2, da_lambda_k2,
              da_q_norm, da_k_norm, da_subln, mla_q_a_norm, mla_w_q_up, mla_kv_a_norm,
              mla_w_kv_up, mla_q_norm, mla_k_norm, c_w_in, c_w_out, gq_q_norm, gq_k_norm,
              gq_sink):
    P = {"norm1_g": norm1_g, "norm2_g": norm2_g, "ada_w": ada_w, "ada_b": ada_b,
         "ff1_w": ff1_w, "ff2_w": ff2_w, "ab_w_in": ab_w_in, "ab_w_out": ab_w_out,
         "da_lambda_q1": da_lambda_q1, "da_lambda_k1": da_lambda_k1,
         "da_lambda_q2": da_lambda_q2, "da_lambda_k2": da_lambda_k2,
         "da_q_norm": da_q_norm, "da_k_norm": da_k_norm, "da_subln": da_subln,
         "mla_q_a_norm": mla_q_a_norm, "mla_w_q_up": mla_w_q_up,
         "mla_kv_a_norm": mla_kv_a_norm, "mla_w_kv_up": mla_w_kv_up,
         "mla_q_norm": mla_q_norm, "mla_k_norm": mla_k_norm,
         "c_w_in": c_w_in, "c_w_out": c_w_out, "gq_q_norm": gq_q_norm,
         "gq_k_norm": gq_k_norm, "gq_sink": gq_sink}
    y_prompt, new = run_trunk(x_prompt, c_ctx, None, None, P)
    s = x_sample.shape[1]
    rows = s // GRID_W
    pos = (jnp.repeat(jnp.arange(rows), GRID_W), jnp.tile(jnp.arange(GRID_W), rows))
    cache = {"da_k": cache_da_k, "da_v": cache_da_v, "mla_ckv": cache_mla_ckv,
             "mla_krope": cache_mla_krope, "gq_k": cache_gq_k, "gq_v": cache_gq_v}
    y_sample, _ = run_trunk(x_sample, c, pos, cache, P)
    new_da_k = jnp.stack(new["da_k"], axis=1)
    new_da_v = jnp.stack(new["da_v"], axis=1)
    new_mla_ckv = jnp.stack(new["mla_ckv"], axis=1)
    new_mla_krope = jnp.stack(new["mla_krope"], axis=1)
    new_gq_k = jnp.stack(new["gq_k"], axis=1)
    new_gq_v = jnp.stack(new["gq_v"], axis=1)
    return (y_prompt, y_sample, new_da_k, new_da_v, new_mla_ckv, new_mla_krope, new_gq_k, new_gq_v)
```

```python
import functools
import math

import jax
import jax.numpy as jnp
from jax import lax
from jax.experimental import pallas as pl
from jax.experimental.pallas import tpu as pltpu

F32 = jnp.float32
BF16 = jnp.bfloat16

D_MODEL = 2048
D_FF = 4 * D_MODEL
GRID_W = 64
ROPE_BASE = 10000.0
EPS = 1e-6
NEG_INF = -1e30
DA_HEADS = 8
DA_QK_DIM = 64
MLA_HEADS = 8
MLA_RANK = 512
MLA_NOPE = 128
MLA_ROPE = 64
MLA_V = 128
MLA_QK = MLA_NOPE + MLA_ROPE
MLA_SLAB = 256
GQ_HEADS = 16
GQ_KV_HEADS = 4
GQ_GROUP = GQ_HEADS // GQ_KV_HEADS
GQ_DIM = 128
WINDOW = 128
HEAD_W = 128
AB_IN = 4160
AB_IN_PAD = 4224

LANES = 128
VMEM_CAP_BYTES = 56 * 1024 * 1024


def _vmem_limit(block_bytes, scratch_bytes=0, temp_bytes=0):
    est = 2 * block_bytes + scratch_bytes + temp_bytes + (4 << 20)
    return int(min(max(est, 16 << 20), VMEM_CAP_BYTES))


def _nbytes(shape, dtype):
    return math.prod(shape) * jnp.dtype(dtype).itemsize


def _params(sem, vmem):
    return pltpu.CompilerParams(dimension_semantics=sem, vmem_limit_bytes=vmem)


def _ada_kernel(c_ref, w_ref, b_ref, o_ref):
    c = c_ref[...]
    s = (c / (1.0 + jnp.exp(-c))).astype(BF16)
    o_ref[0] = jnp.dot(s, w_ref[0].astype(BF16), preferred_element_type=F32) + b_ref[0]


def _ada_mod(cond8, ada_w, ada_b):
    depth, d, n = ada_w.shape
    tn = 1024
    blocks = _nbytes((d, tn), F32) + _nbytes((8, d), F32) + _nbytes((8, tn), F32)
    return pl.pallas_call(
        _ada_kernel,
        grid=(depth, n // tn),
        in_specs=[pl.BlockSpec((8, d), lambda l, j: (0, 0)),
                  pl.BlockSpec((1, d, tn), lambda l, j: (l, 0, j)),
                  pl.BlockSpec((1, 1, tn), lambda l, j: (l, 0, j))],
        out_specs=pl.BlockSpec((1, 8, tn), lambda l, j: (l, 0, j)),
        out_shape=jax.ShapeDtypeStruct((depth, 8, n), F32),
        compiler_params=_params(("arbitrary", "arbitrary"),
                                _vmem_limit(blocks, temp_bytes=_nbytes((d, tn), BF16))),
        name="ada_mod",
    )(cond8, ada_w, ada_b.reshape(depth, 1, n))


def _mod_spec(layer, chunk, tm, n_prompt, group_rows):
    npt = n_prompt // tm
    per = group_rows // tm

    def idx(i, *_):
        row = jnp.where(i < npt, 0, 1 + (i - npt) // per)
        return (layer, row, 0, chunk)

    return pl.BlockSpec((1, 1, 1, D_MODEL), idx)


def _normmod(x, g, sc, sh):
    ms = jnp.mean(x * x, axis=-1, keepdims=True)
    y = x * lax.rsqrt(ms + EPS) * g
    return y * (1.0 + sc) + sh


def _proj_kernel(x_ref, g_ref, sh_ref, sc_ref, w_ref, o_ref, h_ref):
    @pl.when(pl.program_id(1) == 0)
    def _():
        h_ref[...] = _normmod(x_ref[...], g_ref[...], sc_ref[0, 0], sh_ref[0, 0]).astype(BF16)

    o_ref[...] = jnp.dot(h_ref[...], w_ref[...], preferred_element_type=F32)


def _proj(x, norm_g, mod, layer, w, tn, n_prompt, group_rows):
    t, d = x.shape
    n = w.shape[1]
    tm = 512
    blocks = (_nbytes((tm, d), F32) + _nbytes((d, tn), BF16) + _nbytes((tm, tn), F32)
              + 3 * _nbytes((1, d), F32))
    return pl.pallas_call(
        _proj_kernel,
        grid=(t // tm, n // tn),
        in_specs=[pl.BlockSpec((tm, d), lambda i, j: (i, 0)),
                  pl.BlockSpec((1, d), lambda i, j: (0, 0)),
                  _mod_spec(layer, 0, tm, n_prompt, group_rows),
                  _mod_spec(layer, 1, tm, n_prompt, group_rows),
                  pl.BlockSpec((d, tn), lambda i, j: (0, j))],
        out_specs=pl.BlockSpec((tm, tn), lambda i, j: (i, j)),
        out_shape=jax.ShapeDtypeStruct((t, n), F32),
        scratch_shapes=[pltpu.VMEM((tm, d), BF16)],
        compiler_params=_params(("arbitrary", "arbitrary"),
                                _vmem_limit(blocks, _nbytes((tm, d), BF16),
                                            2 * _nbytes((tm, d), F32))),
        name=f"proj_l{layer}",
    )(x, norm_g.reshape(1, d), mod, mod, w)


def _outproj_kernel(n_in, x_ref, g_ref, *refs):
    o_ref = refs[-1]
    acc = None
    for k in range(n_in):
        part = jnp.dot(refs[2 * k][...], refs[2 * k + 1][...], preferred_element_type=F32)
        acc = part if acc is None else acc + part
    o_ref[...] = x_ref[...] + g_ref[0, 0] * acc


def _outproj(x, mod, layer, pairs, n_prompt, group_rows):
    t, d = x.shape
    tm = 512
    in_specs = [pl.BlockSpec((tm, d), lambda i: (i, 0)),
                _mod_spec(layer, 2, tm, n_prompt, group_rows)]
    args = [x, mod]
    blocks = 2 * _nbytes((tm, d), F32)
    for o, w in pairs:
        k = o.shape[1]
        in_specs += [pl.BlockSpec((tm, k), lambda i: (i, 0)),
                     pl.BlockSpec((k, d), lambda i: (0, 0))]
        args += [o, w]
        blocks += _nbytes((tm, k), BF16) + _nbytes((k, d), BF16)
    return pl.pallas_call(
        functools.partial(_outproj_kernel, len(pairs)),
        grid=(t // tm,),
        in_specs=in_specs,
        out_specs=pl.BlockSpec((tm, d), lambda i: (i, 0)),
        out_shape=jax.ShapeDtypeStruct((t, d), F32),
        compiler_params=_params(("arbitrary",), _vmem_limit(blocks, 0, 2 * _nbytes((tm, d), F32))),
        name=f"outproj_l{layer}",
    )(*args)


def _ffn_kernel(x_ref, g_ref, sh_ref, sc_ref, gate_ref, w1_ref, w2_ref, o_ref, h_ref, acc_ref):
    f = pl.program_id(1)

    @pl.when(f == 0)
    def _():
        h_ref[...] = _normmod(x_ref[...], g_ref[...], sc_ref[0, 0], sh_ref[0, 0]).astype(BF16)
        acc_ref[...] = jnp.zeros_like(acc_ref)

    a = jnp.dot(h_ref[...], w1_ref[...], preferred_element_type=F32)
    a = jnp.square(jnp.maximum(a, 0.0)).astype(BF16)
    acc_ref[...] += jnp.dot(a, w2_ref[...], preferred_element_type=F32)

    @pl.when(f == pl.num_programs(1) - 1)
    def _():
        o_ref[...] = x_ref[...] + gate_ref[0, 0] * acc_ref[...]


def _ffn(x, norm_g, mod, layer, w1, w2, n_prompt, group_rows):
    t, d = x.shape
    ff = w1.shape[1]
    tm, tf = 512, 512
    blocks = (2 * _nbytes((tm, d), F32) + _nbytes((d, tf), BF16) + _nbytes((tf, d), BF16)
              + 4 * _nbytes((1, d), F32))
    scratch = _nbytes((tm, d), BF16) + _nbytes((tm, d), F32)
    return pl.pallas_call(
        _ffn_kernel,
        grid=(t // tm, ff // tf),
        in_specs=[pl.BlockSpec((tm, d), lambda i, f: (i, 0)),
                  pl.BlockSpec((1, d), lambda i, f: (0, 0)),
                  _mod_spec(layer, 3, tm, n_prompt, group_rows),
                  _mod_spec(layer, 4, tm, n_prompt, group_rows),
                  _mod_spec(layer, 5, tm, n_prompt, group_rows),
                  pl.BlockSpec((d, tf), lambda i, f: (0, f)),
                  pl.BlockSpec((tf, d), lambda i, f: (f, 0))],
        out_specs=pl.BlockSpec((tm, d), lambda i, f: (i, 0)),
        out_shape=jax.ShapeDtypeStruct((t, d), F32),
        scratch_shapes=[pltpu.VMEM((tm, d), BF16), pltpu.VMEM((tm, d), F32)],
        compiler_params=_params(("arbitrary", "arbitrary"),
                                _vmem_limit(blocks, scratch,
                                            _nbytes((tm, tf), F32) * 2 + _nbytes((tm, d), F32))),
        name=f"ffn_l{layer}",
    )(x, norm_g.reshape(1, d), mod, mod, mod, w1, w2)


def _rope_tables(seq, pattern):
    pos_row = (jnp.arange(seq) // GRID_W).astype(F32)
    pos_col = (jnp.arange(seq) % GRID_W).astype(F32)
    cos_cols, sa_cols, sb_cols = [], [], []
    for width, kind in pattern:
        if kind == "none":
            cos_cols.append(jnp.ones((seq, width), F32))
            sa_cols.append(jnp.zeros((seq, width), F32))
            sb_cols.append(jnp.zeros((seq, width), F32))
            continue
        half = width // 2
        inv = ROPE_BASE ** (-jnp.arange(half, dtype=F32) / half)
        pos = pos_row if kind == "row" else pos_col
        ang = pos[:, None] * inv
        cos, sin = jnp.cos(ang), jnp.sin(ang)
        zero = jnp.zeros_like(sin)
        cos_cols += [cos, cos]
        sa_cols += [-sin, zero]
        sb_cols += [zero, sin]
    tabs = [jnp.concatenate(c, axis=1) for c in (cos_cols, sa_cols, sb_cols)]
    assert tabs[0].shape == (seq, LANES)
    return tabs


def _rope(x, cos, sin_a, sin_b, half):
    return (x * cos + pltpu.roll(x, LANES - half, 1) * sin_a + pltpu.roll(x, half, 1) * sin_b)


def _post_ab_kernel(rope, p_ref, qn_ref, kn_ref, mqn_ref, wq_ref, mkvn_ref, wkv_ref,
                    gq_ref, gkn_ref, gkr_ref, *refs):
    if rope:
        (dc_ref, dsa_ref, dsb_ref, mc_ref, msa_ref, msb_ref,
         qda_ref, kda_ref, vda_ref, qm_ref, km_ref, vm_ref) = refs
    else:
        (qda_ref, kda_ref, vda_ref, qm_ref, km_ref, vm_ref,
         nk_ref, nv_ref, nckv_ref, nkr_ref) = refs
    lane = lax.broadcasted_iota(jnp.int32, (1, HEAD_W), 1)
    lo = lane < DA_QK_DIM

    def da_norm(x, g):
        sq = x * x
        s_lo = jnp.sum(jnp.where(lo, sq, 0.0), axis=-1, keepdims=True)
        s_hi = jnp.sum(jnp.where(lo, 0.0, sq), axis=-1, keepdims=True)
        r = jnp.where(lo, lax.rsqrt(s_lo / DA_QK_DIM + EPS), lax.rsqrt(s_hi / DA_QK_DIM + EPS))
        return x * r * g

    half_da = DA_QK_DIM // 4
    for h in range(DA_HEADS):
        sl = slice(h * HEAD_W, (h + 1) * HEAD_W)
        q = da_norm(p_ref[:, sl], qn_ref[...])
        k = da_norm(p_ref[:, 1024 + h * HEAD_W:1024 + (h + 1) * HEAD_W], kn_ref[...])
        if rope:
            q = _rope(q, dc_ref[...], dsa_ref[...], dsb_ref[...], half_da)
            k = _rope(k, dc_ref[...], dsa_ref[...], dsb_ref[...], half_da)
        else:
            nk_ref[:, sl] = k
        qda_ref[:, sl] = q.astype(BF16)
        kda_ref[:, sl] = k.astype(BF16)
    dv = p_ref[:, 2048:3072]
    vda_ref[...] = dv.astype(BF16)
    if not rope:
        nv_ref[...] = dv

    def rms(x, g):
        return x * lax.rsqrt(jnp.mean(x * x, axis=-1, keepdims=True) + EPS) * g

    mq = rms(p_ref[:, 3072:3584], mqn_ref[...]).astype(BF16)
    qf = jnp.dot(mq, wq_ref[...], preferred_element_type=F32)
    for h in range(MLA_HEADS):
        a = qf[:, h * MLA_SLAB:h * MLA_SLAB + LANES]
        b = qf[:, h * MLA_SLAB + LANES:(h + 1) * MLA_SLAB]
        ss = (jnp.sum(a * a, axis=-1, keepdims=True) + jnp.sum(b * b, axis=-1, keepdims=True))
        r = lax.rsqrt(ss / MLA_QK + EPS)
        a = a * r * gq_ref[:, :LANES]
        b = b * r * gq_ref[:, LANES:]
        if rope:
            b = _rope(b, mc_ref[...], msa_ref[...], msb_ref[...], MLA_ROPE // 4)
        qm_ref[:, h * MLA_SLAB:h * MLA_SLAB + LANES] = a.astype(BF16)
        qm_ref[:, h * MLA_SLAB + LANES:(h + 1) * MLA_SLAB] = b.astype(BF16)

    ckv = rms(p_ref[:, 3584:4096], mkvn_ref[...])
    kr = p_ref[:, 4096:AB_IN_PAD]
    if not rope:
        nckv_ref[...] = ckv
        nkr_ref[...] = kr[:, :MLA_ROPE]
    kv = jnp.dot(ckv.astype(BF16), wkv_ref[...], preferred_element_type=F32)
    ss_kr = jnp.sum(kr * kr, axis=-1, keepdims=True)
    krg = kr * gkr_ref[...]
    if rope:
        krg = _rope(krg, mc_ref[...], msa_ref[...], msb_ref[...], MLA_ROPE // 4)
    for h in range(MLA_HEADS):
        kn = kv[:, h * LANES:(h + 1) * LANES]
        r = lax.rsqrt((jnp.sum(kn * kn, axis=-1, keepdims=True) + ss_kr) / MLA_QK + EPS)
        km_ref[:, h * MLA_SLAB:h * MLA_SLAB + LANES] = (kn * r * gkn_ref[...]).astype(BF16)
        km_ref[:, h * MLA_SLAB + LANES:(h + 1) * MLA_SLAB] = (krg * r).astype(BF16)
    vm_ref[...] = kv[:, MLA_HEADS * LANES:].astype(BF16)


def _post_ab(p, row0, rows, seq, rope, w, tabs):
    tm = 256
    off = row0 // tm
    full = lambda shape: pl.BlockSpec(shape, lambda i: (0,) * len(shape))
    in_specs = [pl.BlockSpec((tm, AB_IN_PAD), lambda i: (i + off, 0)),
                full((1, HEAD_W)), full((1, HEAD_W)), full((1, MLA_RANK)),
                full((MLA_RANK, MLA_HEADS * MLA_SLAB)), full((1, MLA_RANK)),
                full((MLA_RANK, MLA_HEADS * (MLA_NOPE + MLA_V))),
                full((1, MLA_SLAB)), full((1, LANES)), full((1, LANES))]
    args = [p, w["da_qn"], w["da_kn"], w["mq_norm"], w["wq"], w["mkv_norm"], w["wkv"],
            w["gq"], w["gkn"], w["gkr"]]
    if rope:
        per = seq // tm
        in_specs += [pl.BlockSpec((tm, LANES), lambda i: (i % per, 0))] * 6
        args += list(tabs)
    row = lambda n: pl.BlockSpec((tm, n), lambda i: (i, 0))
    out_specs = [row(1024), row(1024), row(1024), row(2048), row(2048), row(1024)]
    out_shape = [jax.ShapeDtypeStruct((rows, n), BF16) for n in (1024, 1024, 1024, 2048, 2048, 1024)]
    if not rope:
        out_specs += [row(1024), row(1024), row(MLA_RANK), row(MLA_ROPE)]
        out_shape += [jax.ShapeDtypeStruct((rows, n), F32) for n in (1024, 1024, MLA_RANK, MLA_ROPE)]
    blocks = (_nbytes((tm, AB_IN_PAD), F32) + 2 * _nbytes((MLA_RANK, 2048), BF16)
              + _nbytes((tm, 8192), BF16) + _nbytes((tm, 2048 + 576), F32) + 6 * _nbytes((tm, LANES), F32))
    return pl.pallas_call(
        functools.partial(_post_ab_kernel, rope),
        grid=(rows // tm,),
        in_specs=in_specs,
        out_specs=out_specs,
        out_shape=out_shape,
        compiler_params=_params(("arbitrary",), _vmem_limit(blocks, 0, 4 * _nbytes((tm, 2048), F32))),
        name="post_ab_rope" if rope else "post_ab",
    )(*args)


def _ctx_mla_kernel(ckv_ref, kr_ref, wkv_ref, gkn_ref, gkr_ref, km_ref, vm_ref):
    kv = jnp.dot(ckv_ref[...].astype(BF16), wkv_ref[...], preferred_element_type=F32)
    kr = kr_ref[...]
    ss_kr = jnp.sum(kr * kr, axis=-1, keepdims=True)
    krg = kr * gkr_ref[...]
    for h in range(MLA_HEADS):
        kn = kv[:, h * LANES:(h + 1) * LANES]
        r = lax.rsqrt((jnp.sum(kn * kn, axis=-1, keepdims=True) + ss_kr) / MLA_QK + EPS)
        km_ref[:, h * MLA_SLAB:h * MLA_SLAB + LANES] = (kn * r * gkn_ref[...]).astype(BF16)
        km_ref[:, h * MLA_SLAB + LANES:(h + 1) * MLA_SLAB] = (krg * r).astype(BF16)
    vm_ref[...] = kv[:, MLA_HEADS * LANES:].astype(BF16)


def _ctx_mla(ckv, kr128, w):
    rows = ckv.shape[0]
    tm = 256
    full = lambda shape: pl.BlockSpec(shape, lambda i: (0,) * len(shape))
    blocks = (_nbytes((tm, MLA_RANK + LANES), F32) + _nbytes((MLA_RANK, 2048), BF16)
              + _nbytes((tm, 3072), BF16))
    return pl.pallas_call(
        _ctx_mla_kernel,
        grid=(rows // tm,),
        in_specs=[pl.BlockSpec((tm, MLA_RANK), lambda i: (i, 0)),
                  pl.BlockSpec((tm, LANES), lambda i: (i, 0)),
                  full((MLA_RANK, 2048)), full((1, LANES)), full((1, LANES))],
        out_specs=[pl.BlockSpec((tm, 2048), lambda i: (i, 0)),
                   pl.BlockSpec((tm, 1024), lambda i: (i, 0))],
        out_shape=[jax.ShapeDtypeStruct((rows, 2048), BF16),
                   jax.ShapeDtypeStruct((rows, 1024), BF16)],
        compiler_params=_params(("arbitrary",), _vmem_limit(blocks, 0, 2 * _nbytes((tm, 2048), F32))),
        name="ctx_mla",
    )(ckv, kr128, w["wkv"], w["gkn"], w["gkr"])


def _dot_nt(a, b):
    return lax.dot_general(a, b, (((1,), (1,)), ((), ())), preferred_element_type=F32)


def _softmax_parts(scores, extra=None):
    m = functools.reduce(jnp.maximum, [jnp.max(s, axis=-1, keepdims=True) for s in scores])
    if extra is not None:
        m = jnp.maximum(m, extra)
    es = [jnp.exp(s - m) for s in scores]
    den = functools.reduce(jnp.add, [jnp.sum(e, axis=-1, keepdims=True) for e in es])
    if extra is not None:
        den = den + jnp.exp(extra - m)
    return es, 1.0 / den


def _da_attn_kernel(n_parts, lam_init, lam_ref, gsub_ref, q_ref, *refs):
    o_ref = refs[-1]
    lv = lam_ref[...]
    lam = (jnp.exp(jnp.sum(lv[0:1] * lv[1:2], axis=-1, keepdims=True))
           - jnp.exp(jnp.sum(lv[2:3] * lv[3:4], axis=-1, keepdims=True)) + lam_init)
    q = q_ref[...]
    lane = lax.broadcasted_iota(jnp.int32, (1, HEAD_W), 1)
    zero = jnp.zeros_like(q)
    q1 = jnp.where(lane < DA_QK_DIM, q, zero)
    q2 = jnp.where(lane < DA_QK_DIM, zero, q)
    scale = DA_QK_DIM ** -0.5
    ks = [refs[2 * p][...].astype(BF16) for p in range(n_parts)]
    vs = [refs[2 * p + 1][...].astype(BF16) for p in range(n_parts)]
    e1, r1 = _softmax_parts([_dot_nt(q1, k) * scale for k in ks])
    e2, r2 = _softmax_parts([_dot_nt(q2, k) * scale for k in ks])
    o = None
    for p in range(n_parts):
        a = (e1[p] * r1 - lam * (e2[p] * r2)).astype(BF16)
        part = jnp.dot(a, vs[p], preferred_element_type=F32)
        o = part if o is None else o + part
    y = o * lax.rsqrt(jnp.mean(o * o, axis=-1, keepdims=True) + EPS) * gsub_ref[...]
    o_ref[...] = (y * (1.0 - lam_init)).astype(BF16)


def _mla_attn_kernel(n_parts, q_ref, *refs):
    o_ref = refs[-1]
    q = q_ref[...]
    scale = MLA_QK ** -0.5
    ks = [refs[2 * p][...] for p in range(n_parts)]
    vs = [refs[2 * p + 1][...] for p in range(n_parts)]
    es, r = _softmax_parts([_dot_nt(q, k) * scale for k in ks])
    o = None
    for p in range(n_parts):
        part = jnp.dot((es[p] * r).astype(BF16), vs[p], preferred_element_type=F32)
        o = part if o is None else o + part
    o_ref[...] = o.astype(BF16)


def _head_attn(kernel, name, q, kv_parts, batch, seq, heads, qw, vw, tq, extra_in=()):
    nq = seq // tq
    in_specs = [pl.BlockSpec(a.shape, lambda b, h, i, nd=a.ndim: (0,) * nd) for a in extra_in]
    args = list(extra_in)
    in_specs.append(pl.BlockSpec((tq, qw), lambda b, h, i: (b * nq + i, h)))
    args.append(q)
    blocks = _nbytes((tq, qw), BF16) + _nbytes((tq, vw), BF16)
    total_l = 0
    for k, v, l in kv_parts:
        in_specs += [pl.BlockSpec((l, qw), lambda b, h, i: (b, h)),
                     pl.BlockSpec((l, vw), lambda b, h, i: (b, h))]
        args += [k, v]
        blocks += _nbytes((l, qw), k.dtype) + _nbytes((l, vw), v.dtype)
        total_l += l
    return pl.pallas_call(
        kernel,
        grid=(batch, heads, nq),
        in_specs=in_specs,
        out_specs=pl.BlockSpec((tq, vw), lambda b, h, i: (b * nq + i, h)),
        out_shape=jax.ShapeDtypeStruct((batch * seq, heads * vw), BF16),
        compiler_params=_params(("arbitrary",) * 3,
                                _vmem_limit(blocks, 0, 8 * _nbytes((tq, total_l), F32))),
        name=name,
    )(*args)


def _gq_attn_kernel(n_parts, tq, seq, kw, q_ref, sink_ref, *refs):
    o_ref = refs[-1]
    scale = GQ_DIM ** -0.5
    qi = pl.program_id(1)
    ks, vs, mask = [], [], None
    if n_parts == 2:
        ks.append(refs[0][...].astype(BF16))
        vs.append(refs[1][...].astype(BF16))
        start = pl.multiple_of(jnp.clip(qi * tq - WINDOW, 0, seq - kw), WINDOW)
        ks.append(refs[2][pl.ds(start, kw), :])
        vs.append(refs[3][pl.ds(start, kw), :])
        rows = qi * tq + lax.broadcasted_iota(jnp.int32, (tq, kw), 0)
        cols = start + lax.broadcasted_iota(jnp.int32, (tq, kw), 1)
        mask = jnp.abs(rows - cols) <= WINDOW
    else:
        ks.append(refs[0][...])
        vs.append(refs[1][...])
    for j in range(GQ_GROUP):
        q = q_ref[:, j * GQ_DIM:(j + 1) * GQ_DIM]
        scores = [_dot_nt(q, k) * scale for k in ks]
        if mask is not None:
            scores[-1] = jnp.where(mask, scores[-1], NEG_INF)
        sink = sink_ref[0, j:j + 1, 0:1]
        es, r = _softmax_parts(scores, extra=sink)
        o = None
        for p in range(n_parts):
            part = jnp.dot((es[p] * r).astype(BF16), vs[p], preferred_element_type=F32)
            o = part if o is None else o + part
        o_ref[:, j * GQ_DIM:(j + 1) * GQ_DIM] = o.astype(BF16)


def _gq_attn(name, q, sink, kv_parts, batch, seq, tq):
    nq = seq // tq
    kw = min(seq, tq + 2 * WINDOW)
    gw = GQ_GROUP * GQ_DIM
    in_specs = [pl.BlockSpec((tq, gw), lambda b, i, g: (b * nq + i, g)),
                pl.BlockSpec((1, GQ_GROUP, LANES), lambda b, i, g: (g, 0, 0))]
    args = [q, sink]
    blocks = 2 * _nbytes((tq, gw), BF16)
    total_l = 0
    for k, v, l in kv_parts:
        in_specs += [pl.BlockSpec((l, GQ_DIM), lambda b, i, g: (b, g)),
                     pl.BlockSpec((l, GQ_DIM), lambda b, i, g: (b, g))]
        args += [k, v]
        blocks += 2 * _nbytes((l, GQ_DIM), k.dtype)
        total_l += min(l, kw) if len(kv_parts) == 2 and l == seq else l
    return pl.pallas_call(
        functools.partial(_gq_attn_kernel, len(kv_parts), tq, seq, kw),
        grid=(batch, nq, GQ_KV_HEADS),
        in_specs=in_specs,
        out_specs=pl.BlockSpec((tq, gw), lambda b, i, g: (b * nq + i, g)),
        out_shape=jax.ShapeDtypeStruct((batch * seq, GQ_HEADS * GQ_DIM), BF16),
        compiler_params=_params(("arbitrary",) * 3,
                                _vmem_limit(blocks, 0, 8 * _nbytes((tq, total_l), F32))),
        name=name,
    )(*args)


def _post_c_kernel(rope, p_ref, qn_ref, kn_ref, *refs):
    if rope:
        c_ref, sa_ref, sb_ref, q_ref, k_ref, v_ref = refs
    else:
        q_ref, k_ref, v_ref, nk_ref, nv_ref = refs

    def rms(x, g):
        return x * lax.rsqrt(jnp.mean(x * x, axis=-1, keepdims=True) + EPS) * g

    nq = GQ_HEADS * GQ_DIM
    nk = GQ_KV_HEADS * GQ_DIM
    for h in range(GQ_HEADS):
        sl = slice(h * GQ_DIM, (h + 1) * GQ_DIM)
        q = rms(p_ref[:, sl], qn_ref[...])
        if rope:
            q = _rope(q, c_ref[...], sa_ref[...], sb_ref[...], GQ_DIM // 4)
        q_ref[:, sl] = q.astype(BF16)
    for h in range(GQ_KV_HEADS):
        sl = slice(h * GQ_DIM, (h + 1) * GQ_DIM)
        k = rms(p_ref[:, nq + h * GQ_DIM:nq + (h + 1) * GQ_DIM], kn_ref[...])
        if rope:
            k = _rope(k, c_ref[...], sa_ref[...], sb_ref[...], GQ_DIM // 4)
        else:
            nk_ref[:, sl] = k
        k_ref[:, sl] = k.astype(BF16)
    v = p_ref[:, nq + nk:]
    v_ref[...] = v.astype(BF16)
    if not rope:
        nv_ref[...] = v


def _post_c(p, row0, rows, seq, rope, qn, kn, tabs):
    tm = 256
    off = row0 // tm
    n = p.shape[1]
    nq = GQ_HEADS * GQ_DIM
    nk = GQ_KV_HEADS * GQ_DIM
    in_specs = [pl.BlockSpec((tm, n), lambda i: (i + off, 0)),
                pl.BlockSpec((1, GQ_DIM), lambda i: (0, 0)),
                pl.BlockSpec((1, GQ_DIM), lambda i: (0, 0))]
    args = [p, qn, kn]
    if rope:
        per = seq // tm
        in_specs += [pl.BlockSpec((tm, LANES), lambda i: (i % per, 0))] * 3
        args += list(tabs)
    row = lambda w: pl.BlockSpec((tm, w), lambda i: (i, 0))
    out_specs = [row(nq), row(nk), row(nk)]
    out_shape = [jax.ShapeDtypeStruct((rows, w), BF16) for w in (nq, nk, nk)]
    if not rope:
        out_specs += [row(nk), row(nk)]
        out_shape += [jax.ShapeDtypeStruct((rows, nk), F32)] * 2
    blocks = _nbytes((tm, n), F32) + _nbytes((tm, n), BF16) + 2 * _nbytes((tm, nk), F32) + 3 * _nbytes((tm, LANES), F32)
    return pl.pallas_call(
        functools.partial(_post_c_kernel, rope),
        grid=(rows // tm,),
        in_specs=in_specs,
        out_specs=out_specs,
        out_shape=out_shape,
        compiler_params=_params(("arbitrary",), _vmem_limit(blocks, 0, 2 * _nbytes((tm, n), F32))),
        name="post_c_rope" if rope else "post_c",
    )(*args)


def kernel(x_prompt, x_sample, cache_da_k, cache_da_v, cache_mla_ckv, cache_mla_krope, cache_gq_k, cache_gq_v, c, c_ctx, norm1_g, norm2_g, ada_w, ada_b, ff1_w, ff2_w, ab_w_in, ab_w_out, da_lambda_q1, da_lambda_k1, da_lambda_q2, da_lambda_k2, da_q_norm, da_k_norm, da_subln, mla_q_a_norm, mla_w_q_up, mla_kv_a_norm, mla_w_kv_up, mla_q_norm, mla_k_norm, c_w_in, c_w_out, gq_q_norm, gq_k_norm, gq_sink):
    pb, ps, d = x_prompt.shape
    sb, ss, _ = x_sample.shape
    past = cache_da_k.shape[2]
    n_p, n_s = pb * ps, sb * ss
    assert sb + 1 <= 8 and d == D_MODEL

    x = jnp.concatenate([x_prompt.reshape(n_p, d), x_sample.reshape(n_s, d)], axis=0)
    cond8 = jnp.concatenate([c_ctx[None], c, jnp.zeros((8 - 1 - sb, d), F32)], axis=0)
    w_in0 = jnp.pad(ab_w_in[0], ((0, 0), (0, AB_IN_PAD - AB_IN))).astype(BF16)
    w_out0 = ab_w_out[0].astype(BF16)
    wq = jnp.pad(mla_w_q_up[0].reshape(MLA_RANK, MLA_HEADS, MLA_QK),
                 ((0, 0), (0, 0), (0, MLA_SLAB - MLA_QK))).reshape(MLA_RANK, MLA_HEADS * MLA_SLAB)
    wkv3 = mla_w_kv_up[0].reshape(MLA_RANK, MLA_HEADS, MLA_NOPE + MLA_V)
    wkv = jnp.concatenate([wkv3[..., :MLA_NOPE].reshape(MLA_RANK, -1),
                           wkv3[..., MLA_NOPE:].reshape(MLA_RANK, -1)], axis=1)
    wts = {
        "da_qn": jnp.tile(da_q_norm[0], 2).reshape(1, HEAD_W),
        "da_kn": jnp.tile(da_k_norm[0], 2).reshape(1, HEAD_W),
        "mq_norm": mla_q_a_norm[0].reshape(1, MLA_RANK),
        "mkv_norm": mla_kv_a_norm[0].reshape(1, MLA_RANK),
        "wq": wq.astype(BF16),
        "wkv": wkv.astype(BF16),
        "gq": jnp.pad(mla_q_norm[0], (0, MLA_SLAB - MLA_QK)).reshape(1, MLA_SLAB),
        "gkn": mla_k_norm[0, :MLA_NOPE].reshape(1, LANES),
        "gkr": jnp.pad(mla_k_norm[0, MLA_NOPE:], (0, LANES - MLA_ROPE)).reshape(1, LANES),
    }
    lam4 = jnp.stack([da_lambda_q1[0], da_lambda_k1[0], da_lambda_q2[0], da_lambda_k2[0]])
    gsub = da_subln[0].reshape(1, HEAD_W)
    lam_init = 0.8 - 0.6 * math.exp(-0.3 * 0)
    tabs_da = _rope_tables(ss, [(32, "row"), (32, "col"), (32, "row"), (32, "col")])
    tabs_mla = _rope_tables(ss, [(32, "row"), (32, "col"), (32, "none"), (32, "none")])
    tabs_gq = _rope_tables(ss, [(64, "row"), (64, "col")])

    mod = _ada_mod(cond8, ada_w, ada_b).reshape(2, 8, 1, 6 * d)
    grp = dict(n_prompt=n_p, group_rows=ss)

    p0 = _proj(x, norm1_g[0], mod, 0, w_in0, 1408, **grp)
    (qda_p, kda_p, vda_p, qm_p, km_p, vm_p,
     new_da_k, new_da_v, new_ckv, new_kr) = _post_ab(p0, 0, n_p, ps, False, wts, None)
    qda_s, kda_s, vda_s, qm_s, km_s, vm_s = _post_ab(p0, n_p, n_s, ss, True, wts, tabs_da + tabs_mla)
    kr_ctx = jnp.pad(cache_mla_krope[:, 0].reshape(sb * past, MLA_ROPE), ((0, 0), (0, LANES - MLA_ROPE)))
    km_c, vm_c = _ctx_mla(cache_mla_ckv[:, 0].reshape(sb * past, MLA_RANK), kr_ctx, wts)

    o_da_p = _head_attn(functools.partial(_da_attn_kernel, 1, lam_init), "da_attn_prompt", qda_p,
                        [(kda_p, vda_p, ps)],
                        pb, ps, DA_HEADS, HEAD_W, HEAD_W, ps, extra_in=(lam4, gsub))
    o_da_s = _head_attn(functools.partial(_da_attn_kernel, 2, lam_init), "da_attn_sample", qda_s,
                        [(cache_da_k[:, 0].reshape(sb * past, -1), cache_da_v[:, 0].reshape(sb * past, -1), past),
                         (kda_s, vda_s, ss)],
                        sb, ss, DA_HEADS, HEAD_W, HEAD_W, 256, extra_in=(lam4, gsub))
    o_m_p = _head_attn(functools.partial(_mla_attn_kernel, 1), "mla_attn_prompt", qm_p,
                       [(km_p, vm_p, ps)], pb, ps, MLA_HEADS, MLA_SLAB, MLA_V, ps)
    o_m_s = _head_attn(functools.partial(_mla_attn_kernel, 2), "mla_attn_sample", qm_s,
                       [(km_c, vm_c, past), (km_s, vm_s, ss)], sb, ss, MLA_HEADS, MLA_SLAB, MLA_V, 256)
    o_da = jnp.concatenate([o_da_p, o_da_s], axis=0)
    o_m = jnp.concatenate([o_m_p, o_m_s], axis=0)
    x = _outproj(x, mod, 0, [(o_da, w_out0[:DA_HEADS * HEAD_W]), (o_m, w_out0[DA_HEADS * HEAD_W:])], **grp)
    x = _ffn(x, norm2_g[0], mod, 0, ff1_w[0].astype(BF16), ff2_w[0].astype(BF16), **grp)

    p1 = _proj(x, norm1_g[1], mod, 1, c_w_in[0].astype(BF16), 1536, **grp)
    qn = gq_q_norm[0].reshape(1, GQ_DIM)
    kn = gq_k_norm[0].reshape(1, GQ_DIM)
    qc_p, kc_p, vc_p, new_gq_k, new_gq_v = _post_c(p1, 0, n_p, ps, False, qn, kn, None)
    qc_s, kc_s, vc_s = _post_c(p1, n_p, n_s, ss, True, qn, kn, tabs_gq)
    sink = jnp.broadcast_to(gq_sink[0].reshape(GQ_KV_HEADS, GQ_GROUP, 1), (GQ_KV_HEADS, GQ_GROUP, LANES))
    o_c_p = _gq_attn("gq_attn_prompt", qc_p, sink, [(kc_p, vc_p, ps)], pb, ps, ps)
    o_c_s = _gq_attn("gq_attn_sample", qc_s, sink,
                     [(cache_gq_k[:, 0].reshape(sb * past, -1), cache_gq_v[:, 0].reshape(sb * past, -1), past),
                      (kc_s, vc_s, ss)], sb, ss, 256)
    o_c = jnp.concatenate([o_c_p, o_c_s], axis=0)
    x = _outproj(x, mod, 1, [(o_c, c_w_out[0].astype(BF16))], **grp)
    x = _ffn(x, norm2_g[1], mod, 1, ff1_w[1].astype(BF16), ff2_w[1].astype(BF16), **grp)

    y_prompt = x[:n_p].reshape(pb, ps, d)
    y_sample = x[n_p:].reshape(sb, ss, d)
    return (y_prompt, y_sample,
            new_da_k.reshape(pb, 1, ps, DA_HEADS, HEAD_W), new_da_v.reshape(pb, 1, ps, DA_HEADS, HEAD_W),
            new_ckv.reshape(pb, 1, ps, MLA_RANK), new_kr.reshape(pb, 1, ps, MLA_ROPE),
            new_gq_k.reshape(pb, 1, ps, GQ_KV_HEADS, GQ_DIM), new_gq_v.reshape(pb, 1, ps, GQ_KV_HEADS, GQ_DIM))
```

```python
import functools
import math
from typing import NamedTuple

import jax
import jax.numpy as jnp
from jax import lax
from jax.experimental import pallas as pl
from jax.experimental.pallas import tpu as pltpu

F32 = jnp.float32
BF16 = jnp.bfloat16

D_MODEL = 2048
GRID_W = 64
ROPE_BASE = 10000.0
EPS = 1e-6
NEG_INF = -1e30
LOG2E = math.log2(math.e)
DA_HEADS = 8
DA_QK_DIM = 64
MLA_HEADS = 8
MLA_RANK = 512
MLA_NOPE = 128
MLA_ROPE = 64
MLA_V = 128
MLA_QK = MLA_NOPE + MLA_ROPE
MLA_SLAB = 256
GQ_HEADS = 16
GQ_KV_HEADS = 4
GQ_GROUP = GQ_HEADS // GQ_KV_HEADS
GQ_DIM = 128
WINDOW = 128
HEAD_W = 128
AB_IN = 4160
AB_IN_PAD = 4224

LANES = 128
VMEM_CAP_BYTES = 56 * 1024 * 1024


class Trunk(NamedTuple):
    groups: int
    seq: int
    batch: int
    mod_row0: int
    rope: bool

    @property
    def rows(self):
        return self.batch * self.seq


def _vmem_limit(block_bytes, scratch_bytes=0, temp_bytes=0):
    est = 2 * block_bytes + scratch_bytes + temp_bytes + (4 << 20)
    return int(min(max(est, 16 << 20), VMEM_CAP_BYTES))


def _nbytes(shape, dtype):
    return math.prod(shape) * jnp.dtype(dtype).itemsize


def _params(sem, vmem):
    return pltpu.CompilerParams(dimension_semantics=sem, vmem_limit_bytes=vmem)


def _ada_kernel(c_ref, w_ref, b_ref, o_ref):
    c = c_ref[...]
    s = (c / (1.0 + jnp.exp(-c))).astype(BF16)
    o_ref[0] = jnp.dot(s, w_ref[0].astype(BF16), preferred_element_type=F32) + b_ref[0]


def _ada_mod(cond8, ada_w, ada_b):
    depth, d, n = ada_w.shape
    tn = 1024
    blocks = _nbytes((d, tn), F32) + _nbytes((8, d), F32) + _nbytes((8, tn), F32)
    return pl.pallas_call(
        _ada_kernel,
        grid=(depth, n // tn),
        in_specs=[pl.BlockSpec((8, d), lambda l, j: (0, 0)),
                  pl.BlockSpec((1, d, tn), lambda l, j: (l, 0, j)),
                  pl.BlockSpec((1, 1, tn), lambda l, j: (l, 0, j))],
        out_specs=pl.BlockSpec((1, 8, tn), lambda l, j: (l, 0, j)),
        out_shape=jax.ShapeDtypeStruct((depth, 8, n), F32),
        compiler_params=_params(("arbitrary", "arbitrary"),
                                _vmem_limit(blocks, temp_bytes=_nbytes((d, tn), BF16))),
        name="ada_mod",
    )(cond8, ada_w, ada_b.reshape(depth, 1, n))


def _mod_spec(layer, chunk, tm, trunk):
    per = trunk.rows // trunk.groups // tm

    def idx(i, *_):
        return (layer, trunk.mod_row0 + i // per, 0, chunk)

    return pl.BlockSpec((1, 1, 1, D_MODEL), idx)


def _normmod(x, g, sc, sh):
    ms = jnp.mean(x * x, axis=-1, keepdims=True)
    y = x * lax.rsqrt(ms + EPS) * g
    return y * (1.0 + sc) + sh


def _proj_kernel(x_ref, g_ref, sh_ref, sc_ref, w_ref, o_ref, h_ref):
    @pl.when(pl.program_id(1) == 0)
    def _():
        h_ref[...] = _normmod(x_ref[...], g_ref[...], sc_ref[0, 0], sh_ref[0, 0]).astype(BF16)

    o_ref[...] = jnp.dot(h_ref[...], w_ref[...], preferred_element_type=F32)


def _proj(x, norm_g, mod, layer, w, tn, trunk, tag):
    t, d = x.shape
    n = w.shape[1]
    tm = 512
    blocks = (_nbytes((tm, d), F32) + _nbytes((d, tn), BF16) + _nbytes((tm, tn), F32)
              + 3 * _nbytes((1, d), F32))
    return pl.pallas_call(
        _proj_kernel,
        grid=(t // tm, n // tn),
        in_specs=[pl.BlockSpec((tm, d), lambda i, j: (i, 0)),
                  pl.BlockSpec((1, d), lambda i, j: (0, 0)),
                  _mod_spec(layer, 0, tm, trunk),
                  _mod_spec(layer, 1, tm, trunk),
                  pl.BlockSpec((d, tn), lambda i, j: (0, j))],
        out_specs=pl.BlockSpec((tm, tn), lambda i, j: (i, j)),
        out_shape=jax.ShapeDtypeStruct((t, n), F32),
        scratch_shapes=[pltpu.VMEM((tm, d), BF16)],
        compiler_params=_params(("arbitrary", "arbitrary"),
                                _vmem_limit(blocks, _nbytes((tm, d), BF16),
                                            2 * _nbytes((tm, d), F32))),
        name=f"proj_l{layer}_{tag}",
    )(x, norm_g.reshape(1, d), mod, mod, w)


def _outproj_kernel(n_in, x_ref, g_ref, *refs):
    o_ref = refs[-1]
    acc = None
    for k in range(n_in):
        part = jnp.dot(refs[2 * k][...], refs[2 * k + 1][...], preferred_element_type=F32)
        acc = part if acc is None else acc + part
    o_ref[...] = x_ref[...] + g_ref[0, 0] * acc


def _outproj(x, mod, layer, pairs, trunk, tag):
    t, d = x.shape
    tm = 512
    in_specs = [pl.BlockSpec((tm, d), lambda i: (i, 0)),
                _mod_spec(layer, 2, tm, trunk)]
    args = [x, mod]
    blocks = 2 * _nbytes((tm, d), F32)
    for o, w in pairs:
        k = o.shape[1]
        in_specs += [pl.BlockSpec((tm, k), lambda i: (i, 0)),
                     pl.BlockSpec((k, d), lambda i: (0, 0))]
        args += [o, w]
        blocks += _nbytes((tm, k), BF16) + _nbytes((k, d), BF16)
    return pl.pallas_call(
        functools.partial(_outproj_kernel, len(pairs)),
        grid=(t // tm,),
        in_specs=in_specs,
        out_specs=pl.BlockSpec((tm, d), lambda i: (i, 0)),
        out_shape=jax.ShapeDtypeStruct((t, d), F32),
        compiler_params=_params(("arbitrary",), _vmem_limit(blocks, 0, 2 * _nbytes((tm, d), F32))),
        name=f"outproj_l{layer}_{tag}",
    )(*args)


def _ffn_kernel(x_ref, g_ref, sh_ref, sc_ref, gate_ref, w1_ref, w2_ref, o_ref, h_ref, acc_ref):
    f = pl.program_id(1)

    @pl.when(f == 0)
    def _():
        h_ref[...] = _normmod(x_ref[...], g_ref[...], sc_ref[0, 0], sh_ref[0, 0]).astype(BF16)
        acc_ref[...] = jnp.zeros_like(acc_ref)

    a = jnp.dot(h_ref[...], w1_ref[...], preferred_element_type=F32)
    a = jnp.square(jnp.maximum(a, 0.0)).astype(BF16)
    acc_ref[...] += jnp.dot(a, w2_ref[...], preferred_element_type=F32)

    @pl.when(f == pl.num_programs(1) - 1)
    def _():
        o_ref[...] = x_ref[...] + gate_ref[0, 0] * acc_ref[...]


def _ffn(x, norm_g, mod, layer, w1, w2, trunk, tag):
    t, d = x.shape
    ff = w1.shape[1]
    tm, tf = 512, 1024
    blocks = (2 * _nbytes((tm, d), F32) + _nbytes((d, tf), BF16) + _nbytes((tf, d), BF16)
              + 4 * _nbytes((1, d), F32))
    scratch = _nbytes((tm, d), BF16) + _nbytes((tm, d), F32)
    return pl.pallas_call(
        _ffn_kernel,
        grid=(t // tm, ff // tf),
        in_specs=[pl.BlockSpec((tm, d), lambda i, f: (i, 0)),
                  pl.BlockSpec((1, d), lambda i, f: (0, 0)),
                  _mod_spec(layer, 3, tm, trunk),
                  _mod_spec(layer, 4, tm, trunk),
                  _mod_spec(layer, 5, tm, trunk),
                  pl.BlockSpec((d, tf), lambda i, f: (0, f)),
                  pl.BlockSpec((tf, d), lambda i, f: (f, 0))],
        out_specs=pl.BlockSpec((tm, d), lambda i, f: (i, 0)),
        out_shape=jax.ShapeDtypeStruct((t, d), F32),
        scratch_shapes=[pltpu.VMEM((tm, d), BF16), pltpu.VMEM((tm, d), F32)],
        compiler_params=_params(("arbitrary", "arbitrary"),
                                _vmem_limit(blocks, scratch,
                                            _nbytes((tm, tf), F32) * 2 + _nbytes((tm, d), F32))),
        name=f"ffn_l{layer}_{tag}",
    )(x, norm_g.reshape(1, d), mod, mod, mod, w1, w2)


def _rope_tables(seq, pattern):
    pos_row = (jnp.arange(seq) // GRID_W).astype(F32)
    pos_col = (jnp.arange(seq) % GRID_W).astype(F32)
    cos_cols, sa_cols, sb_cols = [], [], []
    for width, kind in pattern:
        if kind == "none":
            cos_cols.append(jnp.ones((seq, width), F32))
            sa_cols.append(jnp.zeros((seq, width), F32))
            sb_cols.append(jnp.zeros((seq, width), F32))
            continue
        half = width // 2
        inv = ROPE_BASE ** (-jnp.arange(half, dtype=F32) / half)
        pos = pos_row if kind == "row" else pos_col
        ang = pos[:, None] * inv
        cos, sin = jnp.cos(ang), jnp.sin(ang)
        zero = jnp.zeros_like(sin)
        cos_cols += [cos, cos]
        sa_cols += [-sin, zero]
        sb_cols += [zero, sin]
    tabs = [jnp.concatenate(c, axis=1) for c in (cos_cols, sa_cols, sb_cols)]
    assert tabs[0].shape == (seq, LANES)
    return tabs


def _rope(x, cos, sin_a, sin_b, half):
    return (x * cos + pltpu.roll(x, LANES - half, 1) * sin_a + pltpu.roll(x, half, 1) * sin_b)


def _mla_keys(kv, kr, krg, gkn_ref, km_ref, vm_ref):
    ss_kr = jnp.sum(kr * kr, axis=-1, keepdims=True)
    for h in range(MLA_HEADS):
        kn = kv[:, h * LANES:(h + 1) * LANES]
        r = lax.rsqrt((jnp.sum(kn * kn, axis=-1, keepdims=True) + ss_kr) / MLA_QK + EPS)
        km_ref[:, h * MLA_SLAB:h * MLA_SLAB + LANES] = (kn * r * gkn_ref[...]).astype(BF16)
        km_ref[:, h * MLA_SLAB + LANES:(h + 1) * MLA_SLAB] = (krg * r).astype(BF16)
    vm_ref[...] = kv[:, MLA_HEADS * LANES:].astype(BF16)


def _post_ab_kernel(rope, p_ref, qn_ref, kn_ref, mqn_ref, wq_ref, mkvn_ref, wkv_ref,
                    gq_ref, gkn_ref, gkr_ref, *refs):
    if rope:
        (dc_ref, dsa_ref, dsb_ref, mc_ref, msa_ref, msb_ref,
         qda_ref, kda_ref, vda_ref, qm_ref, km_ref, vm_ref) = refs
    else:
        (qda_ref, kda_ref, vda_ref, qm_ref, km_ref, vm_ref,
         nk_ref, nv_ref, nckv_ref, nkr_ref) = refs
    lane = lax.broadcasted_iota(jnp.int32, (1, HEAD_W), 1)
    lo = lane < DA_QK_DIM

    def da_norm(x, g):
        sq = x * x
        s_lo = jnp.sum(jnp.where(lo, sq, 0.0), axis=-1, keepdims=True)
        s_hi = jnp.sum(jnp.where(lo, 0.0, sq), axis=-1, keepdims=True)
        r = jnp.where(lo, lax.rsqrt(s_lo / DA_QK_DIM + EPS), lax.rsqrt(s_hi / DA_QK_DIM + EPS))
        return x * r * g

    half_da = DA_QK_DIM // 4
    for h in range(DA_HEADS):
        sl = slice(h * HEAD_W, (h + 1) * HEAD_W)
        q = da_norm(p_ref[:, sl], qn_ref[...])
        k = da_norm(p_ref[:, 1024 + h * HEAD_W:1024 + (h + 1) * HEAD_W], kn_ref[...])
        if rope:
            q = _rope(q, dc_ref[...], dsa_ref[...], dsb_ref[...], half_da)
            k = _rope(k, dc_ref[...], dsa_ref[...], dsb_ref[...], half_da)
        else:
            nk_ref[:, sl] = k
        qda_ref[:, sl] = q.astype(BF16)
        kda_ref[:, sl] = k.astype(BF16)
    dv = p_ref[:, 2048:3072]
    vda_ref[...] = dv.astype(BF16)
    if not rope:
        nv_ref[...] = dv

    def rms(x, g):
        return x * lax.rsqrt(jnp.mean(x * x, axis=-1, keepdims=True) + EPS) * g

    mq = rms(p_ref[:, 3072:3584], mqn_ref[...]).astype(BF16)
    qf = jnp.dot(mq, wq_ref[...], preferred_element_type=F32)
    for h in range(MLA_HEADS):
        a = qf[:, h * MLA_SLAB:h * MLA_SLAB + LANES]
        b = qf[:, h * MLA_SLAB + LANES:(h + 1) * MLA_SLAB]
        ss = (jnp.sum(a * a, axis=-1, keepdims=True) + jnp.sum(b * b, axis=-1, keepdims=True))
        r = lax.rsqrt(ss / MLA_QK + EPS)
        a = a * r * gq_ref[:, :LANES]
        b = b * r * gq_ref[:, LANES:]
        if rope:
            b = _rope(b, mc_ref[...], msa_ref[...], msb_ref[...], MLA_ROPE // 4)
        qm_ref[:, h * MLA_SLAB:h * MLA_SLAB + LANES] = a.astype(BF16)
        qm_ref[:, h * MLA_SLAB + LANES:(h + 1) * MLA_SLAB] = b.astype(BF16)

    ckv = rms(p_ref[:, 3584:4096], mkvn_ref[...])
    kr = p_ref[:, 4096:AB_IN_PAD]
    if not rope:
        nckv_ref[...] = ckv
        nkr_ref[...] = kr[:, :MLA_ROPE]
    kv = jnp.dot(ckv.astype(BF16), wkv_ref[...], preferred_element_type=F32)
    krg = kr * gkr_ref[...]
    if rope:
        krg = _rope(krg, mc_ref[...], msa_ref[...], msb_ref[...], MLA_ROPE // 4)
    _mla_keys(kv, kr, krg, gkn_ref, km_ref, vm_ref)


def _post_ab(p, trunk, w, tabs):
    tm = 256
    rows, rope = trunk.rows, trunk.rope
    full = lambda shape: pl.BlockSpec(shape, lambda i: (0,) * len(shape))
    in_specs = [pl.BlockSpec((tm, AB_IN_PAD), lambda i: (i, 0)),
                full((1, HEAD_W)), full((1, HEAD_W)), full((1, MLA_RANK)),
                full((MLA_RANK, MLA_HEADS * MLA_SLAB)), full((1, MLA_RANK)),
                full((MLA_RANK, MLA_HEADS * (MLA_NOPE + MLA_V))),
                full((1, MLA_SLAB)), full((1, LANES)), full((1, LANES))]
    args = [p, w["da_qn"], w["da_kn"], w["mq_norm"], w["wq"], w["mkv_norm"], w["wkv"],
            w["gq"], w["gkn"], w["gkr"]]
    if rope:
        per = trunk.seq // tm
        in_specs += [pl.BlockSpec((tm, LANES), lambda i: (i % per, 0))] * 6
        args += list(tabs)
    row = lambda n: pl.BlockSpec((tm, n), lambda i: (i, 0))
    out_specs = [row(1024), row(1024), row(1024), row(2048), row(2048), row(1024)]
    out_shape = [jax.ShapeDtypeStruct((rows, n), BF16) for n in (1024, 1024, 1024, 2048, 2048, 1024)]
    if not rope:
        out_specs += [row(1024), row(1024), row(MLA_RANK), row(MLA_ROPE)]
        out_shape += [jax.ShapeDtypeStruct((rows, n), F32) for n in (1024, 1024, MLA_RANK, MLA_ROPE)]
    blocks = (_nbytes((tm, AB_IN_PAD), F32) + 2 * _nbytes((MLA_RANK, 2048), BF16)
              + _nbytes((tm, 8192), BF16) + _nbytes((tm, 2048 + 576), F32) + 6 * _nbytes((tm, LANES), F32))
    return pl.pallas_call(
        functools.partial(_post_ab_kernel, rope),
        grid=(rows // tm,),
        in_specs=in_specs,
        out_specs=out_specs,
        out_shape=out_shape,
        compiler_params=_params(("arbitrary",), _vmem_limit(blocks, 0, 4 * _nbytes((tm, 2048), F32))),
        name="post_ab_rope" if rope else "post_ab",
    )(*args)


def _ctx_mla_kernel(ckv_ref, kr_ref, wkv_ref, gkn_ref, gkr_ref, km_ref, vm_ref):
    kv = jnp.dot(ckv_ref[...].astype(BF16), wkv_ref[...], preferred_element_type=F32)
    kr = kr_ref[...]
    _mla_keys(kv, kr, kr * gkr_ref[...], gkn_ref, km_ref, vm_ref)


def _ctx_mla(ckv, kr128, w):
    rows = ckv.shape[0]
    tm = 256
    full = lambda shape: pl.BlockSpec(shape, lambda i: (0,) * len(shape))
    blocks = (_nbytes((tm, MLA_RANK + LANES), F32) + _nbytes((MLA_RANK, 2048), BF16)
              + _nbytes((tm, 3072), BF16))
    return pl.pallas_call(
        _ctx_mla_kernel,
        grid=(rows // tm,),
        in_specs=[pl.BlockSpec((tm, MLA_RANK), lambda i: (i, 0)),
                  pl.BlockSpec((tm, LANES), lambda i: (i, 0)),
                  full((MLA_RANK, 2048)), full((1, LANES)), full((1, LANES))],
        out_specs=[pl.BlockSpec((tm, 2048), lambda i: (i, 0)),
                   pl.BlockSpec((tm, 1024), lambda i: (i, 0))],
        out_shape=[jax.ShapeDtypeStruct((rows, 2048), BF16),
                   jax.ShapeDtypeStruct((rows, 1024), BF16)],
        compiler_params=_params(("arbitrary",), _vmem_limit(blocks, 0, 2 * _nbytes((tm, 2048), F32))),
        name="ctx_mla",
    )(ckv, kr128, w["wkv"], w["gkn"], w["gkr"])


def _dot_nt(a, b):
    return lax.dot_general(a, b, (((1,), (1,)), ((), ())), preferred_element_type=F32)


def _attn_rows(q, parts, scale, mask=None, sink=None):
    c = scale * LOG2E
    scores = [_dot_nt(q, k) * c for k, _ in parts]
    if mask is not None:
        scores[-1] = jnp.where(mask, scores[-1], NEG_INF)
    m = functools.reduce(jnp.maximum, [jnp.max(s, axis=-1, keepdims=True) for s in scores])
    if sink is not None:
        sink2 = sink * LOG2E
        m = jnp.maximum(m, sink2)
    es = [jnp.exp2(s - m) for s in scores]
    den = functools.reduce(jnp.add, [jnp.sum(e, axis=-1, keepdims=True) for e in es])
    if sink is not None:
        den = den + jnp.exp2(sink2 - m)
    o = None
    for e, (_, v) in zip(es, parts):
        part = jnp.dot(e.astype(BF16), v, preferred_element_type=F32)
        o = part if o is None else o + part
    return o, 1.0 / den


def _kv_parts(refs, n_parts, sl_k, sl_v):
    return [(refs[2 * p][:, sl_k].astype(BF16), refs[2 * p + 1][:, sl_v].astype(BF16))
            for p in range(n_parts)]


def _da_attn_kernel(n_parts, lam_init, lam_ref, gsub_ref, q_ref, *refs):
    o_ref = refs[-1]
    tq = q_ref.shape[0]
    lv = lam_ref[...]
    lam = (jnp.exp(jnp.sum(lv[0:1] * lv[1:2], axis=-1, keepdims=True))
           - jnp.exp(jnp.sum(lv[2:3] * lv[3:4], axis=-1, keepdims=True)) + lam_init)
    lo = lax.broadcasted_iota(jnp.int32, (1, HEAD_W), 1) < DA_QK_DIM
    for h in range(DA_HEADS):
        sl = slice(h * HEAD_W, (h + 1) * HEAD_W)
        q = q_ref[:, sl]
        zero = jnp.zeros_like(q)
        q12 = jnp.concatenate([jnp.where(lo, q, zero), jnp.where(lo, zero, q)], axis=0)
        o12, r = _attn_rows(q12, _kv_parts(refs, n_parts, sl, sl), DA_QK_DIM ** -0.5)
        o = o12[:tq] * r[:tq] - lam * (o12[tq:] * r[tq:])
        y = o * lax.rsqrt(jnp.mean(o * o, axis=-1, keepdims=True) + EPS) * gsub_ref[...]
        o_ref[:, sl] = (y * (1.0 - lam_init)).astype(BF16)


def _mla_attn_kernel(n_parts, q_ref, *refs):
    o_ref = refs[-1]
    for h in range(MLA_HEADS):
        sl_k = slice(h * MLA_SLAB, (h + 1) * MLA_SLAB)
        sl_v = slice(h * MLA_V, (h + 1) * MLA_V)
        o, r = _attn_rows(q_ref[:, sl_k], _kv_parts(refs, n_parts, sl_k, sl_v), MLA_QK ** -0.5)
        o_ref[:, sl_v] = (o * r).astype(BF16)


def _gq_attn_kernel(n_parts, seq, kw, sink_ref, q_ref, *refs):
    o_ref = refs[-1]
    tq = q_ref.shape[0]
    qi = pl.program_id(1)
    mask = None
    if n_parts == 2:
        start = pl.multiple_of(jnp.clip(qi * tq - WINDOW, 0, seq - kw), WINDOW)
        rows = qi * tq + (lax.broadcasted_iota(jnp.int32, (GQ_GROUP * tq, kw), 0) & (tq - 1))
        cols = start + lax.broadcasted_iota(jnp.int32, (GQ_GROUP * tq, kw), 1)
        mask = jnp.abs(rows - cols) <= WINDOW
    for g in range(GQ_KV_HEADS):
        sl = slice(g * GQ_DIM, (g + 1) * GQ_DIM)
        heads = range(g * GQ_GROUP, (g + 1) * GQ_GROUP)
        q4 = jnp.concatenate([q_ref[:, j * GQ_DIM:(j + 1) * GQ_DIM] for j in heads], axis=0)
        sink = jnp.concatenate([jnp.broadcast_to(sink_ref[j:j + 1, 0:1], (tq, 1)) for j in heads], axis=0)
        if n_parts == 2:
            parts = [(refs[0][:, sl].astype(BF16), refs[1][:, sl].astype(BF16)),
                     (refs[2][pl.ds(start, kw), sl], refs[3][pl.ds(start, kw), sl])]
        else:
            parts = [(refs[0][:, sl], refs[1][:, sl])]
        o, r = _attn_rows(q4, parts, GQ_DIM ** -0.5, mask=mask, sink=sink)
        o = o * r
        for n, j in enumerate(heads):
            o_ref[:, j * GQ_DIM:(j + 1) * GQ_DIM] = o[n * tq:(n + 1) * tq].astype(BF16)


def _attention(kernel, name, q, kv_parts, trunk, tq, out_w, stacked_rows, extra_in=()):
    nq = trunk.seq // tq
    in_specs = [pl.BlockSpec(a.shape, lambda b, i, nd=a.ndim: (0,) * nd) for a in extra_in]
    args = list(extra_in)
    qw = q.shape[1]
    in_specs.append(pl.BlockSpec((tq, qw), lambda b, i: (b * nq + i, 0)))
    args.append(q)
    blocks = _nbytes((tq, qw), BF16) + _nbytes((tq, out_w), BF16)
    total_l = 0
    for k, v, l in kv_parts:
        in_specs += [pl.BlockSpec((l, k.shape[1]), lambda b, i: (b, 0)),
                     pl.BlockSpec((l, v.shape[1]), lambda b, i: (b, 0))]
        args += [k, v]
        blocks += _nbytes((l, k.shape[1]), k.dtype) + _nbytes((l, v.shape[1]), v.dtype)
        total_l += l
    return pl.pallas_call(
        kernel,
        grid=(trunk.batch, nq),
        in_specs=in_specs,
        out_specs=pl.BlockSpec((tq, out_w), lambda b, i: (b * nq + i, 0)),
        out_shape=jax.ShapeDtypeStruct((trunk.rows, out_w), BF16),
        compiler_params=_params(("arbitrary",) * 2,
                                _vmem_limit(blocks, 0, 6 * _nbytes((stacked_rows, total_l), F32))),
        name=name,
    )(*args)


def _post_c_kernel(rope, p_ref, qn_ref, kn_ref, *refs):
    if rope:
        c_ref, sa_ref, sb_ref, q_ref, k_ref, v_ref = refs
    else:
        q_ref, k_ref, v_ref, nk_ref, nv_ref = refs

    def rms(x, g):
        return x * lax.rsqrt(jnp.mean(x * x, axis=-1, keepdims=True) + EPS) * g

    nq = GQ_HEADS * GQ_DIM
    nk = GQ_KV_HEADS * GQ_DIM
    for h in range(GQ_HEADS):
        sl = slice(h * GQ_DIM, (h + 1) * GQ_DIM)
        q = rms(p_ref[:, sl], qn_ref[...])
        if rope:
            q = _rope(q, c_ref[...], sa_ref[...], sb_ref[...], GQ_DIM // 4)
        q_ref[:, sl] = q.astype(BF16)
    for h in range(GQ_KV_HEADS):
        sl = slice(h * GQ_DIM, (h + 1) * GQ_DIM)
        k = rms(p_ref[:, nq + h * GQ_DIM:nq + (h + 1) * GQ_DIM], kn_ref[...])
        if rope:
            k = _rope(k, c_ref[...], sa_ref[...], sb_ref[...], GQ_DIM // 4)
        else:
            nk_ref[:, sl] = k
        k_ref[:, sl] = k.astype(BF16)
    v = p_ref[:, nq + nk:]
    v_ref[...] = v.astype(BF16)
    if not rope:
        nv_ref[...] = v


def _post_c(p, trunk, qn, kn, tabs):
    tm = 256
    rows, rope = trunk.rows, trunk.rope
    n = p.shape[1]
    nq = GQ_HEADS * GQ_DIM
    nk = GQ_KV_HEADS * GQ_DIM
    in_specs = [pl.BlockSpec((tm, n), lambda i: (i, 0)),
                pl.BlockSpec((1, GQ_DIM), lambda i: (0, 0)),
                pl.BlockSpec((1, GQ_DIM), lambda i: (0, 0))]
    args = [p, qn, kn]
    if rope:
        per = trunk.seq // tm
        in_specs += [pl.BlockSpec((tm, LANES), lambda i: (i % per, 0))] * 3
        args += list(tabs)
    row = lambda w: pl.BlockSpec((tm, w), lambda i: (i, 0))
    out_specs = [row(nq), row(nk), row(nk)]
    out_shape = [jax.ShapeDtypeStruct((rows, w), BF16) for w in (nq, nk, nk)]
    if not rope:
        out_specs += [row(nk), row(nk)]
        out_shape += [jax.ShapeDtypeStruct((rows, nk), F32)] * 2
    blocks = (_nbytes((tm, n), F32) + _nbytes((tm, n), BF16) + 2 * _nbytes((tm, nk), F32)
              + 3 * _nbytes((tm, LANES), F32))
    return pl.pallas_call(
        functools.partial(_post_c_kernel, rope),
        grid=(rows // tm,),
        in_specs=in_specs,
        out_specs=out_specs,
        out_shape=out_shape,
        compiler_params=_params(("arbitrary",), _vmem_limit(blocks, 0, 2 * _nbytes((tm, n), F32))),
        name="post_c_rope" if rope else "post_c",
    )(*args)


def _run_trunk(x, trunk, tag, mod, P, ctx):
    tq = min(trunk.seq, 256)
    p0 = _proj(x, P["norm1_g"][0], mod, 0, P["w_in0"], 1408, trunk, tag)
    post = _post_ab(p0, trunk, P, P["tabs_ab"] if trunk.rope else None)
    qda, kda, vda, qm, km, vm = post[:6]
    da_parts, mla_parts = [(kda, vda, trunk.seq)], [(km, vm, trunk.seq)]
    if ctx is not None:
        da_parts = [(ctx["da_k"], ctx["da_v"], ctx["past"])] + da_parts
        mla_parts = [(ctx["mla_k"], ctx["mla_v"], ctx["past"])] + mla_parts
    o_da = _attention(functools.partial(_da_attn_kernel, len(da_parts), P["lam_init"]),
                      f"da_attn_{tag}", qda, da_parts, trunk, tq, DA_HEADS * HEAD_W, 2 * tq,
                      extra_in=(P["lam4"], P["gsub"]))
    o_m = _attention(functools.partial(_mla_attn_kernel, len(mla_parts)),
                     f"mla_attn_{tag}", qm, mla_parts, trunk, tq, MLA_HEADS * MLA_V, tq)
    x = _outproj(x, mod, 0, [(o_da, P["w_out0_da"]), (o_m, P["w_out0_mla"])], trunk, tag)
    x = _ffn(x, P["norm2_g"][0], mod, 0, P["ff1"][0], P["ff2"][0], trunk, tag)

    p1 = _proj(x, P["norm1_g"][1], mod, 1, P["w_in1"], 1536, trunk, tag)
    post_c = _post_c(p1, trunk, P["gq_qn"], P["gq_kn"], P["tabs_gq"] if trunk.rope else None)
    qc, kc, vc = post_c[:3]
    gq_parts = [(kc, vc, trunk.seq)]
    if ctx is not None:
        gq_parts = [(ctx["gq_k"], ctx["gq_v"], ctx["past"])] + gq_parts
    kw = min(trunk.seq, tq + 2 * WINDOW)
    o_c = _attention(functools.partial(_gq_attn_kernel, len(gq_parts), trunk.seq, kw),
                     f"gq_attn_{tag}", qc, gq_parts, trunk, tq, GQ_HEADS * GQ_DIM, GQ_GROUP * tq,
                     extra_in=(P["sink"],))
    x = _outproj(x, mod, 1, [(o_c, P["w_out1"])], trunk, tag)
    x = _ffn(x, P["norm2_g"][1], mod, 1, P["ff1"][1], P["ff2"][1], trunk, tag)
    return x, post[6:], post_c[3:]


def kernel(x_prompt, x_sample, cache_da_k, cache_da_v, cache_mla_ckv, cache_mla_krope, cache_gq_k, cache_gq_v, c, c_ctx, norm1_g, norm2_g, ada_w, ada_b, ff1_w, ff2_w, ab_w_in, ab_w_out, da_lambda_q1, da_lambda_k1, da_lambda_q2, da_lambda_k2, da_q_norm, da_k_norm, da_subln, mla_q_a_norm, mla_w_q_up, mla_kv_a_norm, mla_w_kv_up, mla_q_norm, mla_k_norm, c_w_in, c_w_out, gq_q_norm, gq_k_norm, gq_sink):
    pb, ps, d = x_prompt.shape
    sb, ss, _ = x_sample.shape
    past = cache_da_k.shape[2]
    assert sb + 1 <= 8 and d == D_MODEL

    cond8 = jnp.concatenate([c_ctx[None], c, jnp.zeros((8 - 1 - sb, d), F32)], axis=0)
    wq = jnp.pad(mla_w_q_up[0].reshape(MLA_RANK, MLA_HEADS, MLA_QK),
                 ((0, 0), (0, 0), (0, MLA_SLAB - MLA_QK))).reshape(MLA_RANK, MLA_HEADS * MLA_SLAB)
    wkv3 = mla_w_kv_up[0].reshape(MLA_RANK, MLA_HEADS, MLA_NOPE + MLA_V)
    wkv = jnp.concatenate([wkv3[..., :MLA_NOPE].reshape(MLA_RANK, -1),
                           wkv3[..., MLA_NOPE:].reshape(MLA_RANK, -1)], axis=1)
    w_out0 = ab_w_out[0].astype(BF16)
    P = {
        "norm1_g": norm1_g, "norm2_g": norm2_g,
        "w_in0": jnp.pad(ab_w_in[0], ((0, 0), (0, AB_IN_PAD - AB_IN))).astype(BF16),
        "w_out0_da": w_out0[:DA_HEADS * HEAD_W], "w_out0_mla": w_out0[DA_HEADS * HEAD_W:],
        "w_in1": c_w_in[0].astype(BF16), "w_out1": c_w_out[0].astype(BF16),
        "ff1": ff1_w.astype(BF16), "ff2": ff2_w.astype(BF16),
        "da_qn": jnp.tile(da_q_norm[0], 2).reshape(1, HEAD_W),
        "da_kn": jnp.tile(da_k_norm[0], 2).reshape(1, HEAD_W),
        "mq_norm": mla_q_a_norm[0].reshape(1, MLA_RANK),
        "mkv_norm": mla_kv_a_norm[0].reshape(1, MLA_RANK),
        "wq": wq.astype(BF16), "wkv": wkv.astype(BF16),
        "gq": jnp.pad(mla_q_norm[0], (0, MLA_SLAB - MLA_QK)).reshape(1, MLA_SLAB),
        "gkn": mla_k_norm[0, :MLA_NOPE].reshape(1, LANES),
        "gkr": jnp.pad(mla_k_norm[0, MLA_NOPE:], (0, LANES - MLA_ROPE)).reshape(1, LANES),
        "lam4": jnp.stack([da_lambda_q1[0], da_lambda_k1[0], da_lambda_q2[0], da_lambda_k2[0]]),
        "gsub": da_subln[0].reshape(1, HEAD_W),
        "lam_init": 0.8 - 0.6 * math.exp(-0.3 * 0),
        "gq_qn": gq_q_norm[0].reshape(1, GQ_DIM), "gq_kn": gq_k_norm[0].reshape(1, GQ_DIM),
        "sink": jnp.broadcast_to(gq_sink[0].reshape(GQ_HEADS, 1), (GQ_HEADS, LANES)),
        "tabs_ab": (_rope_tables(ss, [(32, "row"), (32, "col"), (32, "row"), (32, "col")])
                    + _rope_tables(ss, [(32, "row"), (32, "col"), (32, "none"), (32, "none")])),
        "tabs_gq": _rope_tables(ss, [(64, "row"), (64, "col")]),
    }

    mod = _ada_mod(cond8, ada_w, ada_b).reshape(2, 8, 1, 6 * d)

    kr_ctx = jnp.pad(cache_mla_krope[:, 0].reshape(sb * past, MLA_ROPE), ((0, 0), (0, LANES - MLA_ROPE)))
    mla_k_ctx, mla_v_ctx = _ctx_mla(cache_mla_ckv[:, 0].reshape(sb * past, MLA_RANK), kr_ctx, P)
    ctx = {
        "past": past,
        "da_k": cache_da_k[:, 0].reshape(sb * past, -1), "da_v": cache_da_v[:, 0].reshape(sb * past, -1),
        "mla_k": mla_k_ctx, "mla_v": mla_v_ctx,
        "gq_k": cache_gq_k[:, 0].reshape(sb * past, -1), "gq_v": cache_gq_v[:, 0].reshape(sb * past, -1),
    }

    prompt = Trunk(groups=1, seq=ps, batch=pb, mod_row0=0, rope=False)
    sample = Trunk(groups=sb, seq=ss, batch=sb, mod_row0=1, rope=True)
    y_p, (new_da_k, new_da_v, new_ckv, new_kr), (new_gq_k, new_gq_v) = _run_trunk(
        x_prompt.reshape(pb * ps, d), prompt, "prompt", mod, P, None)
    y_s, _, _ = _run_trunk(x_sample.reshape(sb * ss, d), sample, "sample", mod, P, ctx)

    return (y_p.reshape(pb, ps, d), y_s.reshape(sb, ss, d),
            new_da_k.reshape(pb, 1, ps, DA_HEADS, HEAD_W), new_da_v.reshape(pb, 1, ps, DA_HEADS, HEAD_W),
            new_ckv.reshape(pb, 1, ps, MLA_RANK), new_kr.reshape(pb, 1, ps, MLA_ROPE),
            new_gq_k.reshape(pb, 1, ps, GQ_KV_HEADS, GQ_DIM), new_gq_v.reshape(pb, 1, ps, GQ_KV_HEADS, GQ_DIM))
```

```python
import functools
import math
from typing import NamedTuple

import jax
import jax.numpy as jnp
from jax import lax
from jax.experimental import pallas as pl
from jax.experimental.pallas import tpu as pltpu

F32 = jnp.float32
BF16 = jnp.bfloat16

D_MODEL = 2048
GRID_W = 64
ROPE_BASE = 10000.0
EPS = 1e-6
NEG_INF = -1e30
LOG2E = math.log2(math.e)
DA_HEADS = 8
DA_QK_DIM = 64
DA_W = DA_HEADS * 2 * DA_QK_DIM
MLA_HEADS = 8
MLA_RANK = 512
MLA_NOPE = 128
MLA_ROPE = 64
MLA_V = 128
MLA_QK = MLA_NOPE + MLA_ROPE
MLA_SLAB = 256
GQ_HEADS = 16
GQ_KV_HEADS = 4
GQ_GROUP = GQ_HEADS // GQ_KV_HEADS
GQ_DIM = 128
WINDOW = 128
HEAD_W = 128
AB_IN = 4160
AB_IN_PAD = 4224

LANES = 128
MXU_N = 256
VMEM_CAP_BYTES = 56 * 1024 * 1024


class Trunk(NamedTuple):
    groups: int
    seq: int
    batch: int
    mod_row0: int
    rope: bool

    @property
    def rows(self):
        return self.batch * self.seq


def _vmem_limit(block_bytes, scratch_bytes=0, temp_bytes=0):
    est = 2 * block_bytes + scratch_bytes + temp_bytes + (4 << 20)
    return int(min(max(est, 16 << 20), VMEM_CAP_BYTES))


def _nbytes(shape, dtype):
    return math.prod(shape) * jnp.dtype(dtype).itemsize


def _params(sem, vmem):
    return pltpu.CompilerParams(dimension_semantics=sem, vmem_limit_bytes=vmem)


def _resident(shape):
    return pl.BlockSpec(shape, lambda *_: (0,) * len(shape), pipeline_mode=pl.Buffered(1))


def _const(shape):
    return pl.BlockSpec(shape, lambda *_: (0,) * len(shape))


def _ada_kernel(c_ref, w_ref, b_ref, o_ref):
    c = c_ref[...]
    s = (c / (1.0 + jnp.exp(-c))).astype(BF16)
    o_ref[0] = jnp.dot(s, w_ref[0].astype(BF16), preferred_element_type=F32) + b_ref[0]


def _ada_mod(cond8, ada_w, ada_b):
    depth, d, n = ada_w.shape
    tn = 1024
    blocks = _nbytes((d, tn), F32) + _nbytes((8, d), F32) + _nbytes((8, tn), F32)
    return pl.pallas_call(
        _ada_kernel,
        grid=(depth, n // tn),
        in_specs=[pl.BlockSpec((8, d), lambda l, j: (0, 0)),
                  pl.BlockSpec((1, d, tn), lambda l, j: (l, 0, j)),
                  pl.BlockSpec((1, 1, tn), lambda l, j: (l, 0, j))],
        out_specs=pl.BlockSpec((1, 8, tn), lambda l, j: (l, 0, j)),
        out_shape=jax.ShapeDtypeStruct((depth, 8, n), F32),
        compiler_params=_params(("arbitrary", "arbitrary"),
                                _vmem_limit(blocks, temp_bytes=_nbytes((d, tn), BF16))),
        name="ada_mod",
    )(cond8, ada_w, ada_b.reshape(depth, 1, n))


def _mod_spec(layer, chunk, tm, trunk):
    per = trunk.rows // trunk.groups // tm

    def idx(i, *_):
        return (layer, trunk.mod_row0 + i // per, 0, chunk)

    return pl.BlockSpec((1, 1, 1, D_MODEL), idx)


def _normmod(x, g, sc, sh):
    ms = jnp.mean(x * x, axis=-1, keepdims=True)
    y = x * lax.rsqrt(ms + EPS) * g
    return y * (1.0 + sc) + sh


def _rms(x, g):
    return x * lax.rsqrt(jnp.mean(x * x, axis=-1, keepdims=True) + EPS) * g


def _outproj_kernel(n_in, x_ref, g_ref, *refs):
    o_ref = refs[-1]
    acc = None
    for k in range(n_in):
        part = jnp.dot(refs[2 * k][...], refs[2 * k + 1][...], preferred_element_type=F32)
        acc = part if acc is None else acc + part
    o_ref[...] = x_ref[...] + g_ref[0, 0] * acc


def _outproj(x, mod, layer, pairs, trunk, tag):
    t, d = x.shape
    tm = 512
    in_specs = [pl.BlockSpec((tm, d), lambda i: (i, 0)),
                _mod_spec(layer, 2, tm, trunk)]
    args = [x, mod]
    blocks = 2 * _nbytes((tm, d), F32)
    for o, w, blk in pairs:
        k = o.shape[1]
        in_specs += [pl.BlockSpec((tm, k), lambda i: (i, 0)),
                     pl.BlockSpec((k, d), lambda i, blk=blk: (blk, 0))]
        args += [o, w]
        blocks += _nbytes((tm, k), BF16) + _nbytes((k, d), BF16)
    return pl.pallas_call(
        functools.partial(_outproj_kernel, len(pairs)),
        grid=(t // tm,),
        in_specs=in_specs,
        out_specs=pl.BlockSpec((tm, d), lambda i: (i, 0)),
        out_shape=jax.ShapeDtypeStruct((t, d), F32),
        compiler_params=_params(("arbitrary",), _vmem_limit(blocks, 0, 2 * _nbytes((tm, d), F32))),
        name=f"outproj_l{layer}_{tag}",
    )(*args)


def _ffn_kernel(x_ref, g_ref, sh_ref, sc_ref, gate_ref, w1_ref, w2_ref, o_ref, h_ref, acc_ref):
    f = pl.program_id(1)

    @pl.when(f == 0)
    def _():
        h_ref[...] = _normmod(x_ref[...], g_ref[...], sc_ref[0, 0], sh_ref[0, 0]).astype(BF16)
        acc_ref[...] = jnp.zeros_like(acc_ref)

    a = jnp.dot(h_ref[...], w1_ref[...], preferred_element_type=F32)
    a = jnp.square(jnp.maximum(a, 0.0)).astype(BF16)
    acc_ref[...] += jnp.dot(a, w2_ref[...], preferred_element_type=F32)

    @pl.when(f == pl.num_programs(1) - 1)
    def _():
        o_ref[...] = x_ref[...] + gate_ref[0, 0] * acc_ref[...]


def _ffn(x, norm_g, mod, layer, w1, w2, trunk, tag):
    t, d = x.shape
    ff = w1.shape[2]
    tm, tf = 512, 1024
    blocks = (2 * _nbytes((tm, d), F32) + _nbytes((d, tf), BF16) + _nbytes((tf, d), BF16)
              + 4 * _nbytes((1, d), F32))
    scratch = _nbytes((tm, d), BF16) + _nbytes((tm, d), F32)
    return pl.pallas_call(
        _ffn_kernel,
        grid=(t // tm, ff // tf),
        in_specs=[pl.BlockSpec((tm, d), lambda i, f: (i, 0)),
                  pl.BlockSpec((1, d), lambda i, f: (0, 0)),
                  _mod_spec(layer, 3, tm, trunk),
                  _mod_spec(layer, 4, tm, trunk),
                  _mod_spec(layer, 5, tm, trunk),
                  pl.BlockSpec((None, d, tf), lambda i, f: (layer, 0, f)),
                  pl.BlockSpec((None, tf, d), lambda i, f: (layer, f, 0))],
        out_specs=pl.BlockSpec((tm, d), lambda i, f: (i, 0)),
        out_shape=jax.ShapeDtypeStruct((t, d), F32),
        scratch_shapes=[pltpu.VMEM((tm, d), BF16), pltpu.VMEM((tm, d), F32)],
        compiler_params=_params(("arbitrary", "arbitrary"),
                                _vmem_limit(blocks, scratch,
                                            _nbytes((tm, tf), F32) * 2 + _nbytes((tm, d), F32))),
        name=f"ffn_l{layer}_{tag}",
    )(x, norm_g.reshape(1, d), mod, mod, mod, w1, w2)


def _rope_tables(seq, pattern):
    pos_row = (jnp.arange(seq) // GRID_W).astype(F32)
    pos_col = (jnp.arange(seq) % GRID_W).astype(F32)
    cos_cols, sa_cols, sb_cols = [], [], []
    for width, kind in pattern:
        if kind == "none":
            cos_cols.append(jnp.ones((seq, width), F32))
            sa_cols.append(jnp.zeros((seq, width), F32))
            sb_cols.append(jnp.zeros((seq, width), F32))
            continue
        half = width // 2
        inv = ROPE_BASE ** (-jnp.arange(half, dtype=F32) / half)
        pos = pos_row if kind == "row" else pos_col
        ang = pos[:, None] * inv
        cos, sin = jnp.cos(ang), jnp.sin(ang)
        zero = jnp.zeros_like(sin)
        cos_cols += [cos, cos]
        sa_cols += [-sin, zero]
        sb_cols += [zero, sin]
    tabs = [jnp.concatenate(c, axis=1) for c in (cos_cols, sa_cols, sb_cols)]
    assert tabs[0].shape == (seq, LANES)
    return tabs


def _rope(x, tabs, half):
    cos, sin_a, sin_b = tabs
    return (x * cos + pltpu.roll(x, LANES - half, 1) * sin_a + pltpu.roll(x, half, 1) * sin_b)


def _mla_keys(kv_dot, kr, krg, gkn_ref, km_ref, vm_ref, rows):
    ss_kr = jnp.sum(kr * kr, axis=-1, keepdims=True)
    for pair in range(MLA_HEADS // 2):
        kn2 = kv_dot(pair * MXU_N)
        for s in range(2):
            h = 2 * pair + s
            kn = kn2[:, s * LANES:(s + 1) * LANES]
            r = lax.rsqrt((jnp.sum(kn * kn, axis=-1, keepdims=True) + ss_kr) / MLA_QK + EPS)
            km_ref[rows, h * MLA_SLAB:h * MLA_SLAB + LANES] = (kn * r * gkn_ref[...]).astype(BF16)
            km_ref[rows, h * MLA_SLAB + LANES:(h + 1) * MLA_SLAB] = (krg * r).astype(BF16)
    for c in range(MLA_HEADS * MLA_V // MXU_N):
        v = kv_dot(MLA_HEADS * MLA_NOPE + c * MXU_N)
        vm_ref[rows, c * MXU_N:(c + 1) * MXU_N] = v.astype(BF16)


def _front_ab_kernel(rope, n_chunks, x_ref, g_ref, sh_ref, sc_ref, w_ref, qn_ref, kn_ref, mqn_ref,
                     wq_ref, mkvn_ref, wkv_ref, gq_ref, gkn_ref, gkr_ref, *refs):
    if rope:
        tab_refs, outs, h_ref = refs[:6], refs[6:12], refs[12]
    else:
        tab_refs, outs, h_ref = (), refs[:10], refs[10]
    qda_ref, kda_ref, vda_ref, qm_ref, km_ref, vm_ref = outs[:6]
    rc = x_ref.shape[0] // n_chunks
    lo = lax.broadcasted_iota(jnp.int32, (1, HEAD_W), 1) < DA_QK_DIM

    def da_norm(x, g):
        sq = x * x
        s_lo = jnp.sum(jnp.where(lo, sq, 0.0), axis=-1, keepdims=True)
        s_hi = jnp.sum(jnp.where(lo, 0.0, sq), axis=-1, keepdims=True)
        r = jnp.where(lo, lax.rsqrt(s_lo / DA_QK_DIM + EPS), lax.rsqrt(s_hi / DA_QK_DIM + EPS))
        return x * r * g

    for c in range(n_chunks):
        rows = slice(c * rc, (c + 1) * rc)
        h_ref[rows] = _normmod(x_ref[rows], g_ref[...], sc_ref[0, 0], sh_ref[0, 0]).astype(BF16)

        def hdot(c0, width=MXU_N):
            return jnp.dot(h_ref[rows], w_ref[:, c0:c0 + width], preferred_element_type=F32)

        if rope:
            tabs_da = [t[rows] for t in tab_refs[:3]]
            tabs_mla = [t[rows] for t in tab_refs[3:]]

        for pair in range(DA_HEADS // 2):
            pq = hdot(pair * MXU_N)
            pk = hdot(DA_W + pair * MXU_N)
            for s in range(2):
                sl = slice((2 * pair + s) * HEAD_W, (2 * pair + s + 1) * HEAD_W)
                q = da_norm(pq[:, s * HEAD_W:(s + 1) * HEAD_W], qn_ref[...])
                k = da_norm(pk[:, s * HEAD_W:(s + 1) * HEAD_W], kn_ref[...])
                if rope:
                    q = _rope(q, tabs_da, DA_QK_DIM // 4)
                    k = _rope(k, tabs_da, DA_QK_DIM // 4)
                else:
                    outs[6][rows, sl] = k
                qda_ref[rows, sl] = q.astype(BF16)
                kda_ref[rows, sl] = k.astype(BF16)
        for cc in range(DA_W // MXU_N):
            dv = hdot(2 * DA_W + cc * MXU_N)
            vda_ref[rows, cc * MXU_N:(cc + 1) * MXU_N] = dv.astype(BF16)
            if not rope:
                outs[7][rows, cc * MXU_N:(cc + 1) * MXU_N] = dv

        mq = _rms(hdot(3 * DA_W, MLA_RANK), mqn_ref[...]).astype(BF16)
        for h in range(MLA_HEADS):
            qf = jnp.dot(mq, wq_ref[:, h * MLA_SLAB:(h + 1) * MLA_SLAB], preferred_element_type=F32)
            a, b = qf[:, :LANES], qf[:, LANES:]
            ss = jnp.sum(a * a, axis=-1, keepdims=True) + jnp.sum(b * b, axis=-1, keepdims=True)
            r = lax.rsqrt(ss / MLA_QK + EPS)
            a = a * r * gq_ref[:, :LANES]
            b = b * r * gq_ref[:, LANES:]
            if rope:
                b = _rope(b, tabs_mla, MLA_ROPE // 4)
            qm_ref[rows, h * MLA_SLAB:h * MLA_SLAB + LANES] = a.astype(BF16)
            qm_ref[rows, h * MLA_SLAB + LANES:(h + 1) * MLA_SLAB] = b.astype(BF16)

        ckv = _rms(hdot(3 * DA_W + MLA_RANK, MLA_RANK), mkvn_ref[...])
        kr = hdot(3 * DA_W + 2 * MLA_RANK, LANES)
        if not rope:
            outs[8][rows] = ckv
            outs[9][rows] = kr[:, :MLA_ROPE]
        ckv_b = ckv.astype(BF16)
        krg = kr * gkr_ref[...]
        if rope:
            krg = _rope(krg, tabs_mla, MLA_ROPE // 4)

        def kv_dot(c0):
            return jnp.dot(ckv_b, wkv_ref[:, c0:c0 + MXU_N], preferred_element_type=F32)

        _mla_keys(kv_dot, kr, krg, gkn_ref, km_ref, vm_ref, rows)


def _front_ab(x, norm_g, mod, trunk, w, tabs, tag):
    tm, n_chunks = 256, 2
    rows, rope, d = trunk.rows, trunk.rope, D_MODEL
    in_specs = [pl.BlockSpec((tm, d), lambda i: (i, 0)), _const((1, d)),
                _mod_spec(0, 0, tm, trunk), _mod_spec(0, 1, tm, trunk),
                _resident((d, AB_IN_PAD)),
                _const((1, HEAD_W)), _const((1, HEAD_W)), _const((1, MLA_RANK)),
                _resident((MLA_RANK, MLA_HEADS * MLA_SLAB)), _const((1, MLA_RANK)),
                _resident((MLA_RANK, MLA_HEADS * (MLA_NOPE + MLA_V))),
                _const((1, MLA_SLAB)), _const((1, LANES)), _const((1, LANES))]
    args = [x, norm_g.reshape(1, d), mod, mod, w["w_in0"], w["da_qn"], w["da_kn"], w["mq_norm"],
            w["wq"], w["mkv_norm"], w["wkv"], w["gq"], w["gkn"], w["gkr"]]
    if rope:
        per = trunk.seq // tm
        in_specs += [pl.BlockSpec((tm, LANES), lambda i: (i % per, 0))] * 6
        args += list(tabs)
    row = lambda n: pl.BlockSpec((tm, n), lambda i: (i, 0))
    widths = (DA_W, DA_W, DA_W, MLA_HEADS * MLA_SLAB, MLA_HEADS * MLA_SLAB, MLA_HEADS * MLA_V)
    out_specs = [row(n) for n in widths]
    out_shape = [jax.ShapeDtypeStruct((rows, n), BF16) for n in widths]
    out_bytes = sum(_nbytes((tm, n), BF16) for n in widths)
    if not rope:
        cache_w = (DA_W, DA_W, MLA_RANK, MLA_ROPE)
        out_specs += [row(n) for n in cache_w]
        out_shape += [jax.ShapeDtypeStruct((rows, n), F32) for n in cache_w]
        out_bytes += sum(_nbytes((tm, n), F32) for n in cache_w)
    resident = _nbytes((d, AB_IN_PAD), BF16) + 2 * _nbytes((MLA_RANK, 2048), BF16)
    blocks = _nbytes((tm, d), F32) + out_bytes + 6 * _nbytes((tm, LANES), F32)
    return pl.pallas_call(
        functools.partial(_front_ab_kernel, rope, n_chunks),
        grid=(rows // tm,),
        in_specs=in_specs,
        out_specs=out_specs,
        out_shape=out_shape,
        scratch_shapes=[pltpu.VMEM((tm, d), BF16)],
        compiler_params=_params(("arbitrary",),
                                _vmem_limit(blocks, resident + _nbytes((tm, d), BF16),
                                            4 * _nbytes((tm, d), F32))),
        name=f"front_ab_{tag}",
    )(*args)


def _ctx_mla_kernel(ckv_ref, kr_ref, wkv_ref, gkn_ref, gkr_ref, km_ref, vm_ref):
    ckv_b = ckv_ref[...].astype(BF16)
    kr = kr_ref[...]

    def kv_dot(c0):
        return jnp.dot(ckv_b, wkv_ref[:, c0:c0 + MXU_N], preferred_element_type=F32)

    _mla_keys(kv_dot, kr, kr * gkr_ref[...], gkn_ref, km_ref, vm_ref, slice(None))


def _ctx_mla(ckv, kr128, w):
    rows = ckv.shape[0]
    tm = 256
    blocks = (_nbytes((tm, MLA_RANK + LANES), F32) + _nbytes((MLA_RANK, 2048), BF16)
              + _nbytes((tm, 3072), BF16))
    return pl.pallas_call(
        _ctx_mla_kernel,
        grid=(rows // tm,),
        in_specs=[pl.BlockSpec((tm, MLA_RANK), lambda i: (i, 0)),
                  pl.BlockSpec((tm, LANES), lambda i: (i, 0)),
                  _const((MLA_RANK, 2048)), _const((1, LANES)), _const((1, LANES))],
        out_specs=[pl.BlockSpec((tm, 2048), lambda i: (i, 0)),
                   pl.BlockSpec((tm, 1024), lambda i: (i, 0))],
        out_shape=[jax.ShapeDtypeStruct((rows, 2048), BF16),
                   jax.ShapeDtypeStruct((rows, 1024), BF16)],
        compiler_params=_params(("arbitrary",), _vmem_limit(blocks, 0, 2 * _nbytes((tm, 2048), F32))),
        name="ctx_mla",
    )(ckv, kr128, w["wkv"], w["gkn"], w["gkr"])


def _front_c_kernel(rope, n_chunks, x_ref, g_ref, sh_ref, sc_ref, w_ref, qn_ref, kn_ref, *refs):
    if rope:
        tab_refs, outs, h_ref = refs[:3], refs[3:6], refs[6]
    else:
        tab_refs, outs, h_ref = (), refs[:5], refs[5]
    q_ref, k_ref, v_ref = outs[:3]
    rc = x_ref.shape[0] // n_chunks
    nq = GQ_HEADS * GQ_DIM
    nk = GQ_KV_HEADS * GQ_DIM
    for c in range(n_chunks):
        rows = slice(c * rc, (c + 1) * rc)
        h_ref[rows] = _normmod(x_ref[rows], g_ref[...], sc_ref[0, 0], sh_ref[0, 0]).astype(BF16)

        def hdot(c0):
            return jnp.dot(h_ref[rows], w_ref[:, c0:c0 + MXU_N], preferred_element_type=F32)

        if rope:
            tabs = [t[rows] for t in tab_refs]
        for pair in range(GQ_HEADS // 2):
            pq = hdot(pair * MXU_N)
            for s in range(2):
                sl = slice((2 * pair + s) * GQ_DIM, (2 * pair + s + 1) * GQ_DIM)
                q = _rms(pq[:, s * GQ_DIM:(s + 1) * GQ_DIM], qn_ref[...])
                if rope:
                    q = _rope(q, tabs, GQ_DIM // 4)
                q_ref[rows, sl] = q.astype(BF16)
        for pair in range(GQ_KV_HEADS // 2):
            pk = hdot(nq + pair * MXU_N)
            for s in range(2):
                sl = slice((2 * pair + s) * GQ_DIM, (2 * pair + s + 1) * GQ_DIM)
                k = _rms(pk[:, s * GQ_DIM:(s + 1) * GQ_DIM], kn_ref[...])
                if rope:
                    k = _rope(k, tabs, GQ_DIM // 4)
                else:
                    outs[3][rows, sl] = k
                k_ref[rows, sl] = k.astype(BF16)
        for cc in range(nk // MXU_N):
            v = hdot(nq + nk + cc * MXU_N)
            v_ref[rows, cc * MXU_N:(cc + 1) * MXU_N] = v.astype(BF16)
            if not rope:
                outs[4][rows, cc * MXU_N:(cc + 1) * MXU_N] = v


def _front_c(x, norm_g, mod, trunk, w, tabs, tag):
    tm, n_chunks = 256, 2
    rows, rope, d = trunk.rows, trunk.rope, D_MODEL
    nq = GQ_HEADS * GQ_DIM
    nk = GQ_KV_HEADS * GQ_DIM
    n = nq + 2 * nk
    in_specs = [pl.BlockSpec((tm, d), lambda i: (i, 0)), _const((1, d)),
                _mod_spec(1, 0, tm, trunk), _mod_spec(1, 1, tm, trunk),
                _resident((d, n)), _const((1, GQ_DIM)), _const((1, GQ_DIM))]
    args = [x, norm_g.reshape(1, d), mod, mod, w["w_in1"], w["gq_qn"], w["gq_kn"]]
    if rope:
        per = trunk.seq // tm
        in_specs += [pl.BlockSpec((tm, LANES), lambda i: (i % per, 0))] * 3
        args += list(tabs)
    row = lambda wd: pl.BlockSpec((tm, wd), lambda i: (i, 0))
    out_specs = [row(nq), row(nk), row(nk)]
    out_shape = [jax.ShapeDtypeStruct((rows, wd), BF16) for wd in (nq, nk, nk)]
    out_bytes = _nbytes((tm, n), BF16)
    if not rope:
        out_specs += [row(nk), row(nk)]
        out_shape += [jax.ShapeDtypeStruct((rows, nk), F32)] * 2
        out_bytes += 2 * _nbytes((tm, nk), F32)
    blocks = _nbytes((tm, d), F32) + out_bytes + 3 * _nbytes((tm, LANES), F32)
    return pl.pallas_call(
        functools.partial(_front_c_kernel, rope, n_chunks),
        grid=(rows // tm,),
        in_specs=in_specs,
        out_specs=out_specs,
        out_shape=out_shape,
        scratch_shapes=[pltpu.VMEM((tm, d), BF16)],
        compiler_params=_params(("arbitrary",),
                                _vmem_limit(blocks, _nbytes((d, n), BF16) + _nbytes((tm, d), BF16),
                                            4 * _nbytes((tm, d), F32))),
        name=f"front_c_{tag}",
    )(*args)


def _dot_nt(a, b):
    return lax.dot_general(a, b, (((1,), (1,)), ((), ())), preferred_element_type=F32)


def _attn_rows(q, parts, scale, mask=None, sink=None):
    c = scale * LOG2E
    scores = [_dot_nt(q, k) * c for k, _ in parts]
    if mask is not None:
        scores[-1] = jnp.where(mask, scores[-1], NEG_INF)
    m = functools.reduce(jnp.maximum, [jnp.max(s, axis=-1, keepdims=True) for s in scores])
    if sink is not None:
        sink2 = sink * LOG2E
        m = jnp.maximum(m, sink2)
    es = [jnp.exp2(s - m) for s in scores]
    den = functools.reduce(jnp.add, [jnp.sum(e, axis=-1, keepdims=True) for e in es])
    if sink is not None:
        den = den + jnp.exp2(sink2 - m)
    o = None
    for e, (_, v) in zip(es, parts):
        part = jnp.dot(e.astype(BF16), v, preferred_element_type=F32)
        o = part if o is None else o + part
    return o, 1.0 / den


def _kv_parts(refs, n_parts, sl_k, sl_v):
    return [(refs[2 * p][:, sl_k].astype(BF16), refs[2 * p + 1][:, sl_v].astype(BF16))
            for p in range(n_parts)]


def _da_attn_kernel(n_parts, lam_init, lam_ref, gsub_ref, q_ref, *refs):
    o_ref = refs[-1]
    tq = q_ref.shape[0]
    lv = lam_ref[...]
    lam = (jnp.exp(jnp.sum(lv[0:1] * lv[1:2], axis=-1, keepdims=True))
           - jnp.exp(jnp.sum(lv[2:3] * lv[3:4], axis=-1, keepdims=True)) + lam_init)
    lo = lax.broadcasted_iota(jnp.int32, (1, HEAD_W), 1) < DA_QK_DIM
    for h in range(DA_HEADS):
        sl = slice(h * HEAD_W, (h + 1) * HEAD_W)
        q = q_ref[:, sl]
        zero = jnp.zeros_like(q)
        q12 = jnp.concatenate([jnp.where(lo, q, zero), jnp.where(lo, zero, q)], axis=0)
        o12, r = _attn_rows(q12, _kv_parts(refs, n_parts, sl, sl), DA_QK_DIM ** -0.5)
        o = o12[:tq] * r[:tq] - lam * (o12[tq:] * r[tq:])
        y = o * lax.rsqrt(jnp.mean(o * o, axis=-1, keepdims=True) + EPS) * gsub_ref[...]
        o_ref[:, sl] = (y * (1.0 - lam_init)).astype(BF16)


def _mla_attn_kernel(n_parts, q_ref, *refs):
    o_ref = refs[-1]
    for h in range(MLA_HEADS):
        sl_k = slice(h * MLA_SLAB, (h + 1) * MLA_SLAB)
        sl_v = slice(h * MLA_V, (h + 1) * MLA_V)
        o, r = _attn_rows(q_ref[:, sl_k], _kv_parts(refs, n_parts, sl_k, sl_v), MLA_QK ** -0.5)
        o_ref[:, sl_v] = (o * r).astype(BF16)


def _gq_attn_kernel(n_parts, seq, kw, sink_ref, q_ref, *refs):
    o_ref = refs[-1]
    tq = q_ref.shape[0]
    qi = pl.program_id(1)
    mask = None
    if n_parts == 2:
        start = pl.multiple_of(jnp.clip(qi * tq - WINDOW, 0, seq - kw), WINDOW)
        rows = qi * tq + (lax.broadcasted_iota(jnp.int32, (GQ_GROUP * tq, kw), 0) & (tq - 1))
        cols = start + lax.broadcasted_iota(jnp.int32, (GQ_GROUP * tq, kw), 1)
        mask = jnp.abs(rows - cols) <= WINDOW
    for g in range(GQ_KV_HEADS):
        sl = slice(g * GQ_DIM, (g + 1) * GQ_DIM)
        heads = range(g * GQ_GROUP, (g + 1) * GQ_GROUP)
        q4 = jnp.concatenate([q_ref[:, j * GQ_DIM:(j + 1) * GQ_DIM] for j in heads], axis=0)
        sink = jnp.concatenate([jnp.broadcast_to(sink_ref[j:j + 1, 0:1], (tq, 1)) for j in heads], axis=0)
        if n_parts == 2:
            parts = [(refs[0][:, sl].astype(BF16), refs[1][:, sl].astype(BF16)),
                     (refs[2][pl.ds(start, kw), sl], refs[3][pl.ds(start, kw), sl])]
        else:
            parts = [(refs[0][:, sl], refs[1][:, sl])]
        o, r = _attn_rows(q4, parts, GQ_DIM ** -0.5, mask=mask, sink=sink)
        o = o * r
        for n, j in enumerate(heads):
            o_ref[:, j * GQ_DIM:(j + 1) * GQ_DIM] = o[n * tq:(n + 1) * tq].astype(BF16)


def _attention(kernel, name, q, kv_parts, trunk, tq, out_w, stacked_rows, extra_in=()):
    nq = trunk.seq // tq
    in_specs = [_const(a.shape) for a in extra_in]
    args = list(extra_in)
    qw = q.shape[1]
    in_specs.append(pl.BlockSpec((tq, qw), lambda b, i: (b * nq + i, 0)))
    args.append(q)
    blocks = _nbytes((tq, qw), BF16) + _nbytes((tq, out_w), BF16)
    total_l = 0
    for k, v, l in kv_parts:
        in_specs += [pl.BlockSpec((l, k.shape[1]), lambda b, i: (b, 0)),
                     pl.BlockSpec((l, v.shape[1]), lambda b, i: (b, 0))]
        args += [k, v]
        blocks += _nbytes((l, k.shape[1]), k.dtype) + _nbytes((l, v.shape[1]), v.dtype)
        total_l += l
    return pl.pallas_call(
        kernel,
        grid=(trunk.batch, nq),
        in_specs=in_specs,
        out_specs=pl.BlockSpec((tq, out_w), lambda b, i: (b * nq + i, 0)),
        out_shape=jax.ShapeDtypeStruct((trunk.rows, out_w), BF16),
        compiler_params=_params(("arbitrary",) * 2,
                                _vmem_limit(blocks, 0, 6 * _nbytes((stacked_rows, total_l), F32))),
        name=name,
    )(*args)


def _run_trunk(x, trunk, tag, mod, P, ctx):
    tq = min(trunk.seq, 256)
    front = _front_ab(x, P["norm1_g"][0], mod, trunk, P, P["tabs_ab"] if trunk.rope else None, tag)
    qda, kda, vda, qm, km, vm = front[:6]
    da_parts, mla_parts = [(kda, vda, trunk.seq)], [(km, vm, trunk.seq)]
    if ctx is not None:
        da_parts = [(ctx["da_k"], ctx["da_v"], ctx["past"])] + da_parts
        mla_parts = [(ctx["mla_k"], ctx["mla_v"], ctx["past"])] + mla_parts
    o_da = _attention(functools.partial(_da_attn_kernel, len(da_parts), P["lam_init"]),
                      f"da_attn_{tag}", qda, da_parts, trunk, tq, DA_HEADS * HEAD_W, 2 * tq,
                      extra_in=(P["lam4"], P["gsub"]))
    o_m = _attention(functools.partial(_mla_attn_kernel, len(mla_parts)),
                     f"mla_attn_{tag}", qm, mla_parts, trunk, tq, MLA_HEADS * MLA_V, tq)
    x = _outproj(x, mod, 0, [(o_da, P["w_out0"], 0), (o_m, P["w_out0"], 1)], trunk, tag)
    x = _ffn(x, P["norm2_g"][0], mod, 0, P["ff1"], P["ff2"], trunk, tag)

    front_c = _front_c(x, P["norm1_g"][1], mod, trunk, P, P["tabs_gq"] if trunk.rope else None, tag)
    qc, kc, vc = front_c[:3]
    gq_parts = [(kc, vc, trunk.seq)]
    if ctx is not None:
        gq_parts = [(ctx["gq_k"], ctx["gq_v"], ctx["past"])] + gq_parts
    kw = min(trunk.seq, tq + 2 * WINDOW)
    o_c = _attention(functools.partial(_gq_attn_kernel, len(gq_parts), trunk.seq, kw),
                     f"gq_attn_{tag}", qc, gq_parts, trunk, tq, GQ_HEADS * GQ_DIM, GQ_GROUP * tq,
                     extra_in=(P["sink"],))
    x = _outproj(x, mod, 1, [(o_c, P["w_out1"], 0)], trunk, tag)
    x = _ffn(x, P["norm2_g"][1], mod, 1, P["ff1"], P["ff2"], trunk, tag)
    return x, front[6:], front_c[3:]


def kernel(x_prompt, x_sample, cache_da_k, cache_da_v, cache_mla_ckv, cache_mla_krope, cache_gq_k, cache_gq_v, c, c_ctx, norm1_g, norm2_g, ada_w, ada_b, ff1_w, ff2_w, ab_w_in, ab_w_out, da_lambda_q1, da_lambda_k1, da_lambda_q2, da_lambda_k2, da_q_norm, da_k_norm, da_subln, mla_q_a_norm, mla_w_q_up, mla_kv_a_norm, mla_w_kv_up, mla_q_norm, mla_k_norm, c_w_in, c_w_out, gq_q_norm, gq_k_norm, gq_sink):
    pb, ps, d = x_prompt.shape
    sb, ss, _ = x_sample.shape
    past = cache_da_k.shape[2]
    assert sb + 1 <= 8 and d == D_MODEL

    cond8 = jnp.concatenate([c_ctx[None], c, jnp.zeros((8 - 1 - sb, d), F32)], axis=0)
    wq = jnp.pad(mla_w_q_up[0].reshape(MLA_RANK, MLA_HEADS, MLA_QK),
                 ((0, 0), (0, 0), (0, MLA_SLAB - MLA_QK))).reshape(MLA_RANK, MLA_HEADS * MLA_SLAB)
    wkv3 = mla_w_kv_up[0].reshape(MLA_RANK, MLA_HEADS, MLA_NOPE + MLA_V)
    wkv = jnp.concatenate([wkv3[..., :MLA_NOPE].reshape(MLA_RANK, -1),
                           wkv3[..., MLA_NOPE:].reshape(MLA_RANK, -1)], axis=1)
    P = {
        "norm1_g": norm1_g, "norm2_g": norm2_g,
        "w_in0": jnp.pad(ab_w_in[0], ((0, 0), (0, AB_IN_PAD - AB_IN))).astype(BF16),
        "w_out0": ab_w_out[0].astype(BF16),
        "w_in1": c_w_in[0].astype(BF16), "w_out1": c_w_out[0].astype(BF16),
        "ff1": ff1_w.astype(BF16), "ff2": ff2_w.astype(BF16),
        "da_qn": jnp.tile(da_q_norm[0], 2).reshape(1, HEAD_W),
        "da_kn": jnp.tile(da_k_norm[0], 2).reshape(1, HEAD_W),
        "mq_norm": mla_q_a_norm[0].reshape(1, MLA_RANK),
        "mkv_norm": mla_kv_a_norm[0].reshape(1, MLA_RANK),
        "wq": wq.astype(BF16), "wkv": wkv.astype(BF16),
        "gq": jnp.pad(mla_q_norm[0], (0, MLA_SLAB - MLA_QK)).reshape(1, MLA_SLAB),
        "gkn": mla_k_norm[0, :MLA_NOPE].reshape(1, LANES),
        "gkr": jnp.pad(mla_k_norm[0, MLA_NOPE:], (0, LANES - MLA_ROPE)).reshape(1, LANES),
        "lam4": jnp.stack([da_lambda_q1[0], da_lambda_k1[0], da_lambda_q2[0], da_lambda_k2[0]]),
        "gsub": da_subln[0].reshape(1, HEAD_W),
        "lam_init": 0.8 - 0.6 * math.exp(-0.3 * 0),
        "gq_qn": gq_q_norm[0].reshape(1, GQ_DIM), "gq_kn": gq_k_norm[0].reshape(1, GQ_DIM),
        "sink": jnp.broadcast_to(gq_sink[0].reshape(GQ_HEADS, 1), (GQ_HEADS, LANES)),
        "tabs_ab": (_rope_tables(ss, [(32, "row"), (32, "col"), (32, "row"), (32, "col")])
                    + _rope_tables(ss, [(32, "row"), (32, "col"), (32, "none"), (32, "none")])),
        "tabs_gq": _rope_tables(ss, [(64, "row"), (64, "col")]),
    }

    mod = _ada_mod(cond8, ada_w, ada_b).reshape(2, 8, 1, 6 * d)

    kr_ctx = jnp.pad(cache_mla_krope[:, 0].reshape(sb * past, MLA_ROPE), ((0, 0), (0, LANES - MLA_ROPE)))
    mla_k_ctx, mla_v_ctx = _ctx_mla(cache_mla_ckv[:, 0].reshape(sb * past, MLA_RANK), kr_ctx, P)
    ctx = {
        "past": past,
        "da_k": cache_da_k[:, 0].reshape(sb * past, -1), "da_v": cache_da_v[:, 0].reshape(sb * past, -1),
        "mla_k": mla_k_ctx, "mla_v": mla_v_ctx,
        "gq_k": cache_gq_k[:, 0].reshape(sb * past, -1), "gq_v": cache_gq_v[:, 0].reshape(sb * past, -1),
    }

    prompt = Trunk(groups=1, seq=ps, batch=pb, mod_row0=0, rope=False)
    sample = Trunk(groups=sb, seq=ss, batch=sb, mod_row0=1, rope=True)
    y_p, (new_da_k, new_da_v, new_ckv, new_kr), (new_gq_k, new_gq_v) = _run_trunk(
        x_prompt.reshape(pb * ps, d), prompt, "prompt", mod, P, None)
    y_s, _, _ = _run_trunk(x_sample.reshape(sb * ss, d), sample, "sample", mod, P, ctx)

    return (y_p.reshape(pb, ps, d), y_s.reshape(sb, ss, d),
            new_da_k.reshape(pb, 1, ps, DA_HEADS, HEAD_W), new_da_v.reshape(pb, 1, ps, DA_HEADS, HEAD_W),
            new_ckv.reshape(pb, 1, ps, MLA_RANK), new_kr.reshape(pb, 1, ps, MLA_ROPE),
            new_gq_k.reshape(pb, 1, ps, GQ_KV_HEADS, GQ_DIM), new_gq_v.reshape(pb, 1, ps, GQ_KV_HEADS, GQ_DIM))
```

```python
import functools
import math
from typing import NamedTuple

import jax
import jax.numpy as jnp
from jax import lax
from jax.experimental import pallas as pl
from jax.experimental.pallas import tpu as pltpu

F32 = jnp.float32
BF16 = jnp.bfloat16

D_MODEL = 2048
GRID_W = 64
ROPE_BASE = 10000.0
EPS = 1e-6
NEG_INF = -1e30
LOG2E = math.log2(math.e)
DA_HEADS = 8
DA_QK_DIM = 64
DA_W = DA_HEADS * 2 * DA_QK_DIM
MLA_HEADS = 8
MLA_RANK = 512
MLA_NOPE = 128
MLA_ROPE = 64
MLA_V = 128
MLA_QK = MLA_NOPE + MLA_ROPE
MLA_SLAB = 256
GQ_HEADS = 16
GQ_KV_HEADS = 4
GQ_GROUP = GQ_HEADS // GQ_KV_HEADS
GQ_DIM = 128
WINDOW = 128
HEAD_W = 128
AB_IN = 4160
AB_IN_PAD = 4224

LANES = 128
MXU_N = 256
VMEM_CAP_BYTES = 56 * 1024 * 1024


class Trunk(NamedTuple):
    groups: int
    seq: int
    batch: int
    mod_row0: int
    rope: bool

    @property
    def rows(self):
        return self.batch * self.seq


def _vmem_limit(block_bytes, scratch_bytes=0, temp_bytes=0):
    est = 2 * block_bytes + scratch_bytes + temp_bytes + (4 << 20)
    return int(min(max(est, 16 << 20), VMEM_CAP_BYTES))


def _nbytes(shape, dtype):
    return math.prod(shape) * jnp.dtype(dtype).itemsize


def _params(sem, vmem):
    return pltpu.CompilerParams(dimension_semantics=sem, vmem_limit_bytes=vmem)


def _resident(shape):
    return pl.BlockSpec(shape, lambda *_: (0,) * len(shape), pipeline_mode=pl.Buffered(1))


def _const(shape):
    return pl.BlockSpec(shape, lambda *_: (0,) * len(shape))


def _ada_kernel(c_ref, w_ref, b_ref, o_ref):
    c = c_ref[...]
    s = (c / (1.0 + jnp.exp(-c))).astype(BF16)
    o_ref[0] = jnp.dot(s, w_ref[0].astype(BF16), preferred_element_type=F32) + b_ref[0]


def _ada_mod(cond8, ada_w, ada_b):
    depth, d, n = ada_w.shape
    tn = 1024
    blocks = _nbytes((d, tn), F32) + _nbytes((8, d), F32) + _nbytes((8, tn), F32)
    return pl.pallas_call(
        _ada_kernel,
        grid=(depth, n // tn),
        in_specs=[pl.BlockSpec((8, d), lambda l, j: (0, 0)),
                  pl.BlockSpec((1, d, tn), lambda l, j: (l, 0, j)),
                  pl.BlockSpec((1, 1, tn), lambda l, j: (l, 0, j))],
        out_specs=pl.BlockSpec((1, 8, tn), lambda l, j: (l, 0, j)),
        out_shape=jax.ShapeDtypeStruct((depth, 8, n), F32),
        compiler_params=_params(("arbitrary", "arbitrary"),
                                _vmem_limit(blocks, temp_bytes=_nbytes((d, tn), BF16))),
        name="ada_mod",
    )(cond8, ada_w, ada_b.reshape(depth, 1, n))


def _mod_spec(layer, chunk, tm, trunk):
    per = trunk.rows // trunk.groups // tm

    def idx(i, *_):
        return (layer, trunk.mod_row0 + i // per, 0, chunk)

    return pl.BlockSpec((1, 1, 1, D_MODEL), idx)


def _normmod(x, g, sc, sh):
    ms = jnp.mean(x * x, axis=-1, keepdims=True)
    y = x * lax.rsqrt(ms + EPS) * g
    return y * (1.0 + sc) + sh


def _rms(x, g):
    return x * lax.rsqrt(jnp.mean(x * x, axis=-1, keepdims=True) + EPS) * g


def _outproj_kernel(n_in, x_ref, g_ref, *refs):
    o_ref = refs[-1]
    acc = None
    for k in range(n_in):
        part = jnp.dot(refs[2 * k][...], refs[2 * k + 1][...], preferred_element_type=F32)
        acc = part if acc is None else acc + part
    o_ref[...] = x_ref[...] + g_ref[0, 0] * acc


def _outproj(x, mod, layer, pairs, trunk, tag):
    t, d = x.shape
    tm = 512
    in_specs = [pl.BlockSpec((tm, d), lambda i: (i, 0)),
                _mod_spec(layer, 2, tm, trunk)]
    args = [x, mod]
    blocks = 2 * _nbytes((tm, d), F32)
    for o, w, blk in pairs:
        k = o.shape[1]
        in_specs += [pl.BlockSpec((tm, k), lambda i: (i, 0)),
                     pl.BlockSpec((k, d), lambda i, blk=blk: (blk, 0))]
        args += [o, w]
        blocks += _nbytes((tm, k), BF16) + _nbytes((k, d), BF16)
    return pl.pallas_call(
        functools.partial(_outproj_kernel, len(pairs)),
        grid=(t // tm,),
        in_specs=in_specs,
        out_specs=pl.BlockSpec((tm, d), lambda i: (i, 0)),
        out_shape=jax.ShapeDtypeStruct((t, d), F32),
        compiler_params=_params(("arbitrary",), _vmem_limit(blocks, 0, 2 * _nbytes((tm, d), F32))),
        name=f"outproj_l{layer}_{tag}",
    )(*args)


def _ffn_kernel(x_ref, g_ref, sh_ref, sc_ref, gate_ref, w1_ref, w2_ref, o_ref, h_ref, acc_ref):
    f = pl.program_id(1)

    @pl.when(f == 0)
    def _():
        h_ref[...] = _normmod(x_ref[...], g_ref[...], sc_ref[0, 0], sh_ref[0, 0]).astype(BF16)
        acc_ref[...] = jnp.zeros_like(acc_ref)

    a = jnp.dot(h_ref[...], w1_ref[...], preferred_element_type=F32)
    a = jnp.square(jnp.maximum(a, 0.0)).astype(BF16)
    acc_ref[...] += jnp.dot(a, w2_ref[...], preferred_element_type=F32)

    @pl.when(f == pl.num_programs(1) - 1)
    def _():
        o_ref[...] = x_ref[...] + gate_ref[0, 0] * acc_ref[...]


def _ffn(x, norm_g, mod, layer, w1, w2, trunk, tag):
    t, d = x.shape
    ff = w1.shape[2]
    tm, tf = 512, 1024
    blocks = (2 * _nbytes((tm, d), F32) + _nbytes((d, tf), BF16) + _nbytes((tf, d), BF16)
              + 4 * _nbytes((1, d), F32))
    scratch = _nbytes((tm, d), BF16) + _nbytes((tm, d), F32)
    return pl.pallas_call(
        _ffn_kernel,
        grid=(t // tm, ff // tf),
        in_specs=[pl.BlockSpec((tm, d), lambda i, f: (i, 0)),
                  pl.BlockSpec((1, d), lambda i, f: (0, 0)),
                  _mod_spec(layer, 3, tm, trunk),
                  _mod_spec(layer, 4, tm, trunk),
                  _mod_spec(layer, 5, tm, trunk),
                  pl.BlockSpec((None, d, tf), lambda i, f: (layer, 0, f)),
                  pl.BlockSpec((None, tf, d), lambda i, f: (layer, f, 0))],
        out_specs=pl.BlockSpec((tm, d), lambda i, f: (i, 0)),
        out_shape=jax.ShapeDtypeStruct((t, d), F32),
        scratch_shapes=[pltpu.VMEM((tm, d), BF16), pltpu.VMEM((tm, d), F32)],
        compiler_params=_params(("arbitrary", "arbitrary"),
                                _vmem_limit(blocks, scratch,
                                            _nbytes((tm, tf), F32) * 2 + _nbytes((tm, d), F32))),
        name=f"ffn_l{layer}_{tag}",
    )(x, norm_g.reshape(1, d), mod, mod, mod, w1, w2)


def _rope_tables(seq, pattern):
    pos_row = (jnp.arange(seq) // GRID_W).astype(F32)
    pos_col = (jnp.arange(seq) % GRID_W).astype(F32)
    cos_cols, sa_cols, sb_cols = [], [], []
    for width, kind in pattern:
        if kind == "none":
            cos_cols.append(jnp.ones((seq, width), F32))
            sa_cols.append(jnp.zeros((seq, width), F32))
            sb_cols.append(jnp.zeros((seq, width), F32))
            continue
        half = width // 2
        inv = ROPE_BASE ** (-jnp.arange(half, dtype=F32) / half)
        pos = pos_row if kind == "row" else pos_col
        ang = pos[:, None] * inv
        cos, sin = jnp.cos(ang), jnp.sin(ang)
        zero = jnp.zeros_like(sin)
        cos_cols += [cos, cos]
        sa_cols += [-sin, zero]
        sb_cols += [zero, sin]
    tabs = [jnp.concatenate(c, axis=1) for c in (cos_cols, sa_cols, sb_cols)]
    assert tabs[0].shape == (seq, LANES)
    return tabs


def _rope(x, tabs, half):
    cos, sin_a, sin_b = tabs
    return (x * cos + pltpu.roll(x, LANES - half, 1) * sin_a + pltpu.roll(x, half, 1) * sin_b)


def _mla_keys(kv_dot, kr, krg, gkn_ref, km_ref, vm_ref, rows):
    ss_kr = jnp.sum(kr * kr, axis=-1, keepdims=True)
    for pair in range(MLA_HEADS // 2):
        kn2 = kv_dot(pair * MXU_N)
        for s in range(2):
            h = 2 * pair + s
            kn = kn2[:, s * LANES:(s + 1) * LANES]
            r = lax.rsqrt((jnp.sum(kn * kn, axis=-1, keepdims=True) + ss_kr) / MLA_QK + EPS)
            km_ref[rows, h * MLA_SLAB:h * MLA_SLAB + LANES] = (kn * r * gkn_ref[...]).astype(BF16)
            km_ref[rows, h * MLA_SLAB + LANES:(h + 1) * MLA_SLAB] = (krg * r).astype(BF16)
    for c in range(MLA_HEADS * MLA_V // MXU_N):
        v = kv_dot(MLA_HEADS * MLA_NOPE + c * MXU_N)
        vm_ref[rows, c * MXU_N:(c + 1) * MXU_N] = v.astype(BF16)


def _front_ab_kernel(rope, n_chunks, x_ref, g_ref, sh_ref, sc_ref, w_ref, qn_ref, kn_ref, mqn_ref,
                     wq_ref, mkvn_ref, wkv_ref, gq_ref, gkn_ref, gkr_ref, *refs):
    if rope:
        tab_refs, outs, h_ref = refs[:6], refs[6:12], refs[12]
    else:
        tab_refs, outs, h_ref = (), refs[:10], refs[10]
    qda_ref, kda_ref, vda_ref, qm_ref, km_ref, vm_ref = outs[:6]
    rc = x_ref.shape[0] // n_chunks
    lo = lax.broadcasted_iota(jnp.int32, (1, HEAD_W), 1) < DA_QK_DIM
    qn_c = qn_ref[...] * (DA_QK_DIM ** -0.5 * LOG2E)
    gq_c = gq_ref[...] * (MLA_QK ** -0.5 * LOG2E)

    def da_norm(x, g):
        sq = x * x
        s_lo = jnp.sum(jnp.where(lo, sq, 0.0), axis=-1, keepdims=True)
        s_hi = jnp.sum(jnp.where(lo, 0.0, sq), axis=-1, keepdims=True)
        r = jnp.where(lo, lax.rsqrt(s_lo / DA_QK_DIM + EPS), lax.rsqrt(s_hi / DA_QK_DIM + EPS))
        return x * r * g

    for c in range(n_chunks):
        rows = slice(c * rc, (c + 1) * rc)
        h_ref[rows] = _normmod(x_ref[rows], g_ref[...], sc_ref[0, 0], sh_ref[0, 0]).astype(BF16)

        def hdot(c0, width=MXU_N):
            return jnp.dot(h_ref[rows], w_ref[:, c0:c0 + width], preferred_element_type=F32)

        if rope:
            tabs_da = [t[rows] for t in tab_refs[:3]]
            tabs_mla = [t[rows] for t in tab_refs[3:]]

        for pair in range(DA_HEADS // 2):
            pq = hdot(pair * MXU_N)
            pk = hdot(DA_W + pair * MXU_N)
            for s in range(2):
                sl = slice((2 * pair + s) * HEAD_W, (2 * pair + s + 1) * HEAD_W)
                q = da_norm(pq[:, s * HEAD_W:(s + 1) * HEAD_W], qn_c)
                k = da_norm(pk[:, s * HEAD_W:(s + 1) * HEAD_W], kn_ref[...])
                if rope:
                    q = _rope(q, tabs_da, DA_QK_DIM // 4)
                    k = _rope(k, tabs_da, DA_QK_DIM // 4)
                else:
                    outs[6][rows, sl] = k
                qda_ref[rows, sl] = q.astype(BF16)
                kda_ref[rows, sl] = k.astype(BF16)
        for cc in range(DA_W // MXU_N):
            dv = hdot(2 * DA_W + cc * MXU_N)
            vda_ref[rows, cc * MXU_N:(cc + 1) * MXU_N] = dv.astype(BF16)
            if not rope:
                outs[7][rows, cc * MXU_N:(cc + 1) * MXU_N] = dv

        mq = _rms(hdot(3 * DA_W, MLA_RANK), mqn_ref[...]).astype(BF16)
        for h in range(MLA_HEADS):
            qf = jnp.dot(mq, wq_ref[:, h * MLA_SLAB:(h + 1) * MLA_SLAB], preferred_element_type=F32)
            a, b = qf[:, :LANES], qf[:, LANES:]
            ss = jnp.sum(a * a, axis=-1, keepdims=True) + jnp.sum(b * b, axis=-1, keepdims=True)
            r = lax.rsqrt(ss / MLA_QK + EPS)
            a = a * r * gq_c[:, :LANES]
            b = b * r * gq_c[:, LANES:]
            if rope:
                b = _rope(b, tabs_mla, MLA_ROPE // 4)
            qm_ref[rows, h * MLA_SLAB:h * MLA_SLAB + LANES] = a.astype(BF16)
            qm_ref[rows, h * MLA_SLAB + LANES:(h + 1) * MLA_SLAB] = b.astype(BF16)

        ckv = _rms(hdot(3 * DA_W + MLA_RANK, MLA_RANK), mkvn_ref[...])
        kr = hdot(3 * DA_W + 2 * MLA_RANK, LANES)
        if not rope:
            outs[8][rows] = ckv
            outs[9][rows] = kr[:, :MLA_ROPE]
        ckv_b = ckv.astype(BF16)
        krg = kr * gkr_ref[...]
        if rope:
            krg = _rope(krg, tabs_mla, MLA_ROPE // 4)

        def kv_dot(c0):
            return jnp.dot(ckv_b, wkv_ref[:, c0:c0 + MXU_N], preferred_element_type=F32)

        _mla_keys(kv_dot, kr, krg, gkn_ref, km_ref, vm_ref, rows)


def _front_ab(x, norm_g, mod, trunk, w, tabs, tag):
    tm, n_chunks = 256, 2
    rows, rope, d = trunk.rows, trunk.rope, D_MODEL
    in_specs = [pl.BlockSpec((tm, d), lambda i: (i, 0)), _const((1, d)),
                _mod_spec(0, 0, tm, trunk), _mod_spec(0, 1, tm, trunk),
                _resident((d, AB_IN_PAD)),
                _const((1, HEAD_W)), _const((1, HEAD_W)), _const((1, MLA_RANK)),
                _resident((MLA_RANK, MLA_HEADS * MLA_SLAB)), _const((1, MLA_RANK)),
                _resident((MLA_RANK, MLA_HEADS * (MLA_NOPE + MLA_V))),
                _const((1, MLA_SLAB)), _const((1, LANES)), _const((1, LANES))]
    args = [x, norm_g.reshape(1, d), mod, mod, w["w_in0"], w["da_qn"], w["da_kn"], w["mq_norm"],
            w["wq"], w["mkv_norm"], w["wkv"], w["gq"], w["gkn"], w["gkr"]]
    if rope:
        per = trunk.seq // tm
        in_specs += [pl.BlockSpec((tm, LANES), lambda i: (i % per, 0))] * 6
        args += list(tabs)
    row = lambda n: pl.BlockSpec((tm, n), lambda i: (i, 0))
    widths = (DA_W, DA_W, DA_W, MLA_HEADS * MLA_SLAB, MLA_HEADS * MLA_SLAB, MLA_HEADS * MLA_V)
    out_specs = [row(n) for n in widths]
    out_shape = [jax.ShapeDtypeStruct((rows, n), BF16) for n in widths]
    out_bytes = sum(_nbytes((tm, n), BF16) for n in widths)
    if not rope:
        cache_w = (DA_W, DA_W, MLA_RANK, MLA_ROPE)
        out_specs += [row(n) for n in cache_w]
        out_shape += [jax.ShapeDtypeStruct((rows, n), F32) for n in cache_w]
        out_bytes += sum(_nbytes((tm, n), F32) for n in cache_w)
    resident = _nbytes((d, AB_IN_PAD), BF16) + 2 * _nbytes((MLA_RANK, 2048), BF16)
    blocks = _nbytes((tm, d), F32) + out_bytes + 6 * _nbytes((tm, LANES), F32)
    return pl.pallas_call(
        functools.partial(_front_ab_kernel, rope, n_chunks),
        grid=(rows // tm,),
        in_specs=in_specs,
        out_specs=out_specs,
        out_shape=out_shape,
        scratch_shapes=[pltpu.VMEM((tm, d), BF16)],
        compiler_params=_params(("arbitrary",),
                                _vmem_limit(blocks, resident + _nbytes((tm, d), BF16),
                                            4 * _nbytes((tm, d), F32))),
        name=f"front_ab_{tag}",
    )(*args)


def _ctx_mla_kernel(ckv_ref, kr_ref, wkv_ref, gkn_ref, gkr_ref, km_ref, vm_ref):
    ckv_b = ckv_ref[...].astype(BF16)
    kr = kr_ref[...]

    def kv_dot(c0):
        return jnp.dot(ckv_b, wkv_ref[:, c0:c0 + MXU_N], preferred_element_type=F32)

    _mla_keys(kv_dot, kr, kr * gkr_ref[...], gkn_ref, km_ref, vm_ref, slice(None))


def _ctx_mla(ckv, kr128, w):
    rows = ckv.shape[0]
    tm = 256
    blocks = (_nbytes((tm, MLA_RANK + LANES), F32) + _nbytes((MLA_RANK, 2048), BF16)
              + _nbytes((tm, 3072), BF16))
    return pl.pallas_call(
        _ctx_mla_kernel,
        grid=(rows // tm,),
        in_specs=[pl.BlockSpec((tm, MLA_RANK), lambda i: (i, 0)),
                  pl.BlockSpec((tm, LANES), lambda i: (i, 0)),
                  _const((MLA_RANK, 2048)), _const((1, LANES)), _const((1, LANES))],
        out_specs=[pl.BlockSpec((tm, 2048), lambda i: (i, 0)),
                   pl.BlockSpec((tm, 1024), lambda i: (i, 0))],
        out_shape=[jax.ShapeDtypeStruct((rows, 2048), BF16),
                   jax.ShapeDtypeStruct((rows, 1024), BF16)],
        compiler_params=_params(("arbitrary",), _vmem_limit(blocks, 0, 2 * _nbytes((tm, 2048), F32))),
        name="ctx_mla",
    )(ckv, kr128, w["wkv"], w["gkn"], w["gkr"])


def _front_c_kernel(rope, n_chunks, x_ref, g_ref, sh_ref, sc_ref, w_ref, qn_ref, kn_ref, *refs):
    if rope:
        tab_refs, outs, h_ref = refs[:3], refs[3:6], refs[6]
    else:
        tab_refs, outs, h_ref = (), refs[:5], refs[5]
    q_ref, k_ref, v_ref = outs[:3]
    rc = x_ref.shape[0] // n_chunks
    nq = GQ_HEADS * GQ_DIM
    nk = GQ_KV_HEADS * GQ_DIM
    qn_c = qn_ref[...] * (GQ_DIM ** -0.5 * LOG2E)
    for c in range(n_chunks):
        rows = slice(c * rc, (c + 1) * rc)
        h_ref[rows] = _normmod(x_ref[rows], g_ref[...], sc_ref[0, 0], sh_ref[0, 0]).astype(BF16)

        def hdot(c0):
            return jnp.dot(h_ref[rows], w_ref[:, c0:c0 + MXU_N], preferred_element_type=F32)

        if rope:
            tabs = [t[rows] for t in tab_refs]
        for pair in range(GQ_HEADS // 2):
            pq = hdot(pair * MXU_N)
            for s in range(2):
                sl = slice((2 * pair + s) * GQ_DIM, (2 * pair + s + 1) * GQ_DIM)
                q = _rms(pq[:, s * GQ_DIM:(s + 1) * GQ_DIM], qn_c)
                if rope:
                    q = _rope(q, tabs, GQ_DIM // 4)
                q_ref[rows, sl] = q.astype(BF16)
        for pair in range(GQ_KV_HEADS // 2):
            pk = hdot(nq + pair * MXU_N)
            for s in range(2):
                sl = slice((2 * pair + s) * GQ_DIM, (2 * pair + s + 1) * GQ_DIM)
                k = _rms(pk[:, s * GQ_DIM:(s + 1) * GQ_DIM], kn_ref[...])
                if rope:
                    k = _rope(k, tabs, GQ_DIM // 4)
                else:
                    outs[3][rows, sl] = k
                k_ref[rows, sl] = k.astype(BF16)
        for cc in range(nk // MXU_N):
            v = hdot(nq + nk + cc * MXU_N)
            v_ref[rows, cc * MXU_N:(cc + 1) * MXU_N] = v.astype(BF16)
            if not rope:
                outs[4][rows, cc * MXU_N:(cc + 1) * MXU_N] = v


def _front_c(x, norm_g, mod, trunk, w, tabs, tag):
    tm, n_chunks = 256, 2
    rows, rope, d = trunk.rows, trunk.rope, D_MODEL
    nq = GQ_HEADS * GQ_DIM
    nk = GQ_KV_HEADS * GQ_DIM
    n = nq + 2 * nk
    in_specs = [pl.BlockSpec((tm, d), lambda i: (i, 0)), _const((1, d)),
                _mod_spec(1, 0, tm, trunk), _mod_spec(1, 1, tm, trunk),
                _resident((d, n)), _const((1, GQ_DIM)), _const((1, GQ_DIM))]
    args = [x, norm_g.reshape(1, d), mod, mod, w["w_in1"], w["gq_qn"], w["gq_kn"]]
    if rope:
        per = trunk.seq // tm
        in_specs += [pl.BlockSpec((tm, LANES), lambda i: (i % per, 0))] * 3
        args += list(tabs)
    row = lambda wd: pl.BlockSpec((tm, wd), lambda i: (i, 0))
    out_specs = [row(nq), row(nk), row(nk)]
    out_shape = [jax.ShapeDtypeStruct((rows, wd), BF16) for wd in (nq, nk, nk)]
    out_bytes = _nbytes((tm, n), BF16)
    if not rope:
        out_specs += [row(nk), row(nk)]
        out_shape += [jax.ShapeDtypeStruct((rows, nk), F32)] * 2
        out_bytes += 2 * _nbytes((tm, nk), F32)
    blocks = _nbytes((tm, d), F32) + out_bytes + 3 * _nbytes((tm, LANES), F32)
    return pl.pallas_call(
        functools.partial(_front_c_kernel, rope, n_chunks),
        grid=(rows // tm,),
        in_specs=in_specs,
        out_specs=out_specs,
        out_shape=out_shape,
        scratch_shapes=[pltpu.VMEM((tm, d), BF16)],
        compiler_params=_params(("arbitrary",),
                                _vmem_limit(blocks, _nbytes((d, n), BF16) + _nbytes((tm, d), BF16),
                                            4 * _nbytes((tm, d), F32))),
        name=f"front_c_{tag}",
    )(*args)


def _dot_nt(a, b):
    return lax.dot_general(a, b, (((1,), (1,)), ((), ())), preferred_element_type=F32)


def _attn_rows(q, parts):
    scores = [_dot_nt(q, k) for k, _ in parts]
    m = functools.reduce(jnp.maximum, [jnp.max(s, axis=-1, keepdims=True) for s in scores])
    o1 = None
    for s, (_, v) in zip(scores, parts):
        e = jnp.exp2(s - m).astype(BF16)
        v1 = jnp.concatenate([v, jnp.ones((v.shape[0], LANES), BF16)], axis=1)
        part = jnp.dot(e, v1, preferred_element_type=F32)
        o1 = part if o1 is None else o1 + part
    dv = parts[0][1].shape[1]
    return o1[:, :dv], o1[:, dv:]


ONES_ROWS = 16


def _attn_cols(q, parts, mask=None, sink=None):
    scores = [_dot_nt(k, q) for k, _ in parts]
    if mask is not None:
        scores[-1] = jnp.where(mask, scores[-1], NEG_INF)
    m = functools.reduce(jnp.maximum, [jnp.max(s, axis=0, keepdims=True) for s in scores])
    if sink is not None:
        sink2 = sink * LOG2E
        m = jnp.maximum(m, sink2)
    o1 = None
    for s, (_, v) in zip(scores, parts):
        e = jnp.exp2(s - m).astype(BF16)
        v_t1 = jnp.concatenate([v.astype(F32).T.astype(BF16),
                                jnp.ones((ONES_ROWS, v.shape[0]), BF16)], axis=0)
        part = jnp.dot(v_t1, e, preferred_element_type=F32)
        o1 = part if o1 is None else o1 + part
    dv = parts[0][1].shape[1]
    den = o1[dv:dv + 1]
    if sink is not None:
        den = den + jnp.exp2(sink2 - m)
    return o1[:dv], den


def _kv_parts(refs, n_parts, sl_k, sl_v):
    return [(refs[2 * p][:, sl_k].astype(BF16), refs[2 * p + 1][:, sl_v].astype(BF16))
            for p in range(n_parts)]


def _da_attn_kernel(n_parts, lam_init, lam_ref, gsub_ref, q_ref, *refs):
    o_ref = refs[-1]
    tq = q_ref.shape[0]
    lv = lam_ref[...]
    lam = (jnp.exp(jnp.sum(lv[0:1] * lv[1:2], axis=-1, keepdims=True))
           - jnp.exp(jnp.sum(lv[2:3] * lv[3:4], axis=-1, keepdims=True)) + lam_init)
    lo = lax.broadcasted_iota(jnp.int32, (1, HEAD_W), 1) < DA_QK_DIM
    for h in range(DA_HEADS):
        sl = slice(h * HEAD_W, (h + 1) * HEAD_W)
        q = q_ref[:, sl]
        zero = jnp.zeros_like(q)
        q12 = jnp.concatenate([jnp.where(lo, q, zero), jnp.where(lo, zero, q)], axis=0)
        o12, d12 = _attn_rows(q12, _kv_parts(refs, n_parts, sl, sl))
        o12 = o12 * (1.0 / d12)
        o = o12[:tq] - lam * o12[tq:]
        y = o * lax.rsqrt(jnp.mean(o * o, axis=-1, keepdims=True) + EPS) * gsub_ref[...]
        o_ref[:, sl] = (y * (1.0 - lam_init)).astype(BF16)


def _mla_attn_kernel(n_parts, q_ref, *refs):
    o_ref = refs[-1]
    for h in range(MLA_HEADS):
        sl_k = slice(h * MLA_SLAB, (h + 1) * MLA_SLAB)
        sl_v = slice(h * MLA_V, (h + 1) * MLA_V)
        o, den = _attn_rows(q_ref[:, sl_k], _kv_parts(refs, n_parts, sl_k, sl_v))
        o_ref[:, sl_v] = (o * (1.0 / den)).astype(BF16)


def _gq_attn_kernel(n_parts, seq, kw, sink_ref, q_ref, *refs):
    o_ref = refs[-1]
    tq = q_ref.shape[0]
    qi = pl.program_id(1)
    mask = None
    if n_parts == 2:
        start = pl.multiple_of(jnp.clip(qi * tq - WINDOW, 0, seq - kw), WINDOW)
        keys = start + lax.broadcasted_iota(jnp.int32, (kw, GQ_GROUP * tq), 0)
        qrows = qi * tq + (lax.broadcasted_iota(jnp.int32, (kw, GQ_GROUP * tq), 1) & (tq - 1))
        mask = jnp.abs(qrows - keys) <= WINDOW
    for g in range(GQ_KV_HEADS):
        sl = slice(g * GQ_DIM, (g + 1) * GQ_DIM)
        heads = range(g * GQ_GROUP, (g + 1) * GQ_GROUP)
        q4 = jnp.concatenate([q_ref[:, j * GQ_DIM:(j + 1) * GQ_DIM] for j in heads], axis=0)
        sink = jnp.concatenate([jnp.broadcast_to(sink_ref[j:j + 1, 0:1], (1, tq)) for j in heads], axis=1)
        if n_parts == 2:
            parts = [(refs[0][:, sl].astype(BF16), refs[1][:, sl]),
                     (refs[2][pl.ds(start, kw), sl], refs[3][pl.ds(start, kw), sl])]
        else:
            parts = [(refs[0][:, sl], refs[1][:, sl])]
        o, den = _attn_cols(q4, parts, mask=mask, sink=sink)
        o = o * (1.0 / den)
        for n, j in enumerate(heads):
            o_ref[:, j * GQ_DIM:(j + 1) * GQ_DIM] = o[:, n * tq:(n + 1) * tq].T.astype(BF16)


def _attention(kernel, name, q, kv_parts, trunk, tq, out_w, stacked_rows, extra_in=()):
    nq = trunk.seq // tq
    in_specs = [_const(a.shape) for a in extra_in]
    args = list(extra_in)
    qw = q.shape[1]
    in_specs.append(pl.BlockSpec((tq, qw), lambda b, i: (b * nq + i, 0)))
    args.append(q)
    blocks = _nbytes((tq, qw), BF16) + _nbytes((tq, out_w), BF16)
    total_l = 0
    for k, v, l in kv_parts:
        in_specs += [pl.BlockSpec((l, k.shape[1]), lambda b, i: (b, 0)),
                     pl.BlockSpec((l, v.shape[1]), lambda b, i: (b, 0))]
        args += [k, v]
        blocks += _nbytes((l, k.shape[1]), k.dtype) + _nbytes((l, v.shape[1]), v.dtype)
        total_l += l
    return pl.pallas_call(
        kernel,
        grid=(trunk.batch, nq),
        in_specs=in_specs,
        out_specs=pl.BlockSpec((tq, out_w), lambda b, i: (b * nq + i, 0)),
        out_shape=jax.ShapeDtypeStruct((trunk.rows, out_w), BF16),
        compiler_params=_params(("arbitrary",) * 2,
                                _vmem_limit(blocks, 0, 6 * _nbytes((stacked_rows, total_l), F32))),
        name=name,
    )(*args)


def _run_trunk(x, trunk, tag, mod, P, ctx):
    tq = min(trunk.seq, 256)
    front = _front_ab(x, P["norm1_g"][0], mod, trunk, P, P["tabs_ab"] if trunk.rope else None, tag)
    qda, kda, vda, qm, km, vm = front[:6]
    da_parts, mla_parts = [(kda, vda, trunk.seq)], [(km, vm, trunk.seq)]
    if ctx is not None:
        da_parts = [(ctx["da_k"], ctx["da_v"], ctx["past"])] + da_parts
        mla_parts = [(ctx["mla_k"], ctx["mla_v"], ctx["past"])] + mla_parts
    o_da = _attention(functools.partial(_da_attn_kernel, len(da_parts), P["lam_init"]),
                      f"da_attn_{tag}", qda, da_parts, trunk, tq, DA_HEADS * HEAD_W, 2 * tq,
                      extra_in=(P["lam4"], P["gsub"]))
    o_m = _attention(functools.partial(_mla_attn_kernel, len(mla_parts)),
                     f"mla_attn_{tag}", qm, mla_parts, trunk, tq, MLA_HEADS * MLA_V, tq)
    x = _outproj(x, mod, 0, [(o_da, P["w_out0"], 0), (o_m, P["w_out0"], 1)], trunk, tag)
    x = _ffn(x, P["norm2_g"][0], mod, 0, P["ff1"], P["ff2"], trunk, tag)

    front_c = _front_c(x, P["norm1_g"][1], mod, trunk, P, P["tabs_gq"] if trunk.rope else None, tag)
    qc, kc, vc = front_c[:3]
    gq_parts = [(kc, vc, trunk.seq)]
    if ctx is not None:
        gq_parts = [(ctx["gq_k"], ctx["gq_v"], ctx["past"])] + gq_parts
    kw = min(trunk.seq, tq + 2 * WINDOW)
    o_c = _attention(functools.partial(_gq_attn_kernel, len(gq_parts), trunk.seq, kw),
                     f"gq_attn_{tag}", qc, gq_parts, trunk, tq, GQ_HEADS * GQ_DIM, GQ_GROUP * tq,
                     extra_in=(P["sink"],))
    x = _outproj(x, mod, 1, [(o_c, P["w_out1"], 0)], trunk, tag)
    x = _ffn(x, P["norm2_g"][1], mod, 1, P["ff1"], P["ff2"], trunk, tag)
    return x, front[6:], front_c[3:]


def kernel(x_prompt, x_sample, cache_da_k, cache_da_v, cache_mla_ckv, cache_mla_krope, cache_gq_k, cache_gq_v, c, c_ctx, norm1_g, norm2_g, ada_w, ada_b, ff1_w, ff2_w, ab_w_in, ab_w_out, da_lambda_q1, da_lambda_k1, da_lambda_q2, da_lambda_k2, da_q_norm, da_k_norm, da_subln, mla_q_a_norm, mla_w_q_up, mla_kv_a_norm, mla_w_kv_up, mla_q_norm, mla_k_norm, c_w_in, c_w_out, gq_q_norm, gq_k_norm, gq_sink):
    pb, ps, d = x_prompt.shape
    sb, ss, _ = x_sample.shape
    past = cache_da_k.shape[2]
    assert sb + 1 <= 8 and d == D_MODEL

    cond8 = jnp.concatenate([c_ctx[None], c, jnp.zeros((8 - 1 - sb, d), F32)], axis=0)
    wq = jnp.pad(mla_w_q_up[0].reshape(MLA_RANK, MLA_HEADS, MLA_QK),
                 ((0, 0), (0, 0), (0, MLA_SLAB - MLA_QK))).reshape(MLA_RANK, MLA_HEADS * MLA_SLAB)
    wkv3 = mla_w_kv_up[0].reshape(MLA_RANK, MLA_HEADS, MLA_NOPE + MLA_V)
    wkv = jnp.concatenate([wkv3[..., :MLA_NOPE].reshape(MLA_RANK, -1),
                           wkv3[..., MLA_NOPE:].reshape(MLA_RANK, -1)], axis=1)
    P = {
        "norm1_g": norm1_g, "norm2_g": norm2_g,
        "w_in0": jnp.pad(ab_w_in[0], ((0, 0), (0, AB_IN_PAD - AB_IN))).astype(BF16),
        "w_out0": ab_w_out[0].astype(BF16),
        "w_in1": c_w_in[0].astype(BF16), "w_out1": c_w_out[0].astype(BF16),
        "ff1": ff1_w.astype(BF16), "ff2": ff2_w.astype(BF16),
        "da_qn": jnp.tile(da_q_norm[0], 2).reshape(1, HEAD_W),
        "da_kn": jnp.tile(da_k_norm[0], 2).reshape(1, HEAD_W),
        "mq_norm": mla_q_a_norm[0].reshape(1, MLA_RANK),
        "mkv_norm": mla_kv_a_norm[0].reshape(1, MLA_RANK),
        "wq": wq.astype(BF16), "wkv": wkv.astype(BF16),
        "gq": jnp.pad(mla_q_norm[0], (0, MLA_SLAB - MLA_QK)).reshape(1, MLA_SLAB),
        "gkn": mla_k_norm[0, :MLA_NOPE].reshape(1, LANES),
        "gkr": jnp.pad(mla_k_norm[0, MLA_NOPE:], (0, LANES - MLA_ROPE)).reshape(1, LANES),
        "lam4": jnp.stack([da_lambda_q1[0], da_lambda_k1[0], da_lambda_q2[0], da_lambda_k2[0]]),
        "gsub": da_subln[0].reshape(1, HEAD_W),
        "lam_init": 0.8 - 0.6 * math.exp(-0.3 * 0),
        "gq_qn": gq_q_norm[0].reshape(1, GQ_DIM), "gq_kn": gq_k_norm[0].reshape(1, GQ_DIM),
        "sink": jnp.broadcast_to(gq_sink[0].reshape(GQ_HEADS, 1), (GQ_HEADS, LANES)),
        "tabs_ab": (_rope_tables(ss, [(32, "row"), (32, "col"), (32, "row"), (32, "col")])
                    + _rope_tables(ss, [(32, "row"), (32, "col"), (32, "none"), (32, "none")])),
        "tabs_gq": _rope_tables(ss, [(64, "row"), (64, "col")]),
    }

    mod = _ada_mod(cond8, ada_w, ada_b).reshape(2, 8, 1, 6 * d)

    kr_ctx = jnp.pad(cache_mla_krope[:, 0].reshape(sb * past, MLA_ROPE), ((0, 0), (0, LANES - MLA_ROPE)))
    mla_k_ctx, mla_v_ctx = _ctx_mla(cache_mla_ckv[:, 0].reshape(sb * past, MLA_RANK), kr_ctx, P)
    ctx = {
        "past": past,
        "da_k": cache_da_k[:, 0].reshape(sb * past, -1), "da_v": cache_da_v[:, 0].reshape(sb * past, -1),
        "mla_k": mla_k_ctx, "mla_v": mla_v_ctx,
        "gq_k": cache_gq_k[:, 0].reshape(sb * past, -1), "gq_v": cache_gq_v[:, 0].reshape(sb * past, -1),
    }

    prompt = Trunk(groups=1, seq=ps, batch=pb, mod_row0=0, rope=False)
    sample = Trunk(groups=sb, seq=ss, batch=sb, mod_row0=1, rope=True)
    y_p, (new_da_k, new_da_v, new_ckv, new_kr), (new_gq_k, new_gq_v) = _run_trunk(
        x_prompt.reshape(pb * ps, d), prompt, "prompt", mod, P, None)
    y_s, _, _ = _run_trunk(x_sample.reshape(sb * ss, d), sample, "sample", mod, P, ctx)

    return (y_p.reshape(pb, ps, d), y_s.reshape(sb, ss, d),
            new_da_k.reshape(pb, 1, ps, DA_HEADS, HEAD_W), new_da_v.reshape(pb, 1, ps, DA_HEADS, HEAD_W),
            new_ckv.reshape(pb, 1, ps, MLA_RANK), new_kr.reshape(pb, 1, ps, MLA_ROPE),
            new_gq_k.reshape(pb, 1, ps, GQ_KV_HEADS, GQ_DIM), new_gq_v.reshape(pb, 1, ps, GQ_KV_HEADS, GQ_DIM))
```

```python
import functools
import math
from typing import NamedTuple

import jax
import jax.numpy as jnp
from jax import lax
from jax.experimental import pallas as pl
from jax.experimental.pallas import tpu as pltpu

F32 = jnp.float32
BF16 = jnp.bfloat16

D_MODEL = 2048
GRID_W = 64
ROPE_BASE = 10000.0
EPS = 1e-6
NEG_INF = -1e30
LOG2E = math.log2(math.e)
DA_HEADS = 8
DA_QK_DIM = 64
DA_W = DA_HEADS * 2 * DA_QK_DIM
MLA_HEADS = 8
MLA_RANK = 512
MLA_NOPE = 128
MLA_ROPE = 64
MLA_V = 128
MLA_QK = MLA_NOPE + MLA_ROPE
MLA_SLAB = 256
GQ_HEADS = 16
GQ_KV_HEADS = 4
GQ_GROUP = GQ_HEADS // GQ_KV_HEADS
GQ_DIM = 128
WINDOW = 128
HEAD_W = 128
AB_IN = 4160
AB_IN_PAD = 4224

LANES = 128
MXU_N = 256
VMEM_CAP_BYTES = 60 * 1024 * 1024


class Trunk(NamedTuple):
    groups: int
    seq: int
    batch: int
    mod_row0: int
    rope: bool

    @property
    def rows(self):
        return self.batch * self.seq


def _vmem_limit(block_bytes, scratch_bytes=0, temp_bytes=0):
    est = 2 * block_bytes + scratch_bytes + temp_bytes + (4 << 20)
    return int(min(max(est, 16 << 20), VMEM_CAP_BYTES))


def _nbytes(shape, dtype):
    return math.prod(shape) * jnp.dtype(dtype).itemsize


def _params(sem, vmem):
    return pltpu.CompilerParams(dimension_semantics=sem, vmem_limit_bytes=vmem)


def _resident(shape):
    return pl.BlockSpec(shape, lambda *_: (0,) * len(shape), pipeline_mode=pl.Buffered(1))


def _const(shape):
    return pl.BlockSpec(shape, lambda *_: (0,) * len(shape))


def _with_side_cast(kernel, n_in, n_out):
    def wrapped(*refs):
        refs[n_in + 1 + n_out][...] = refs[n_in][...].astype(BF16)
        kernel(*refs[:n_in], *refs[n_in + 1:n_in + 1 + n_out], *refs[n_in + 2 + n_out:])

    return wrapped


def _side_cast_specs(cast, steps, step_of):
    w, layer = cast
    _, rows, cols = w.shape
    blk = rows // steps
    in_spec = pl.BlockSpec((None, blk, cols), lambda *ids: (layer, step_of(*ids), 0))
    out_spec = pl.BlockSpec((blk, cols), lambda *ids: (step_of(*ids), 0))
    return (in_spec, out_spec, jax.ShapeDtypeStruct((rows, cols), BF16),
            _nbytes((blk, cols), F32) + _nbytes((blk, cols), BF16))


def _ada_kernel(c_ref, w_ref, b_ref, o_ref):
    c = c_ref[...]
    s = (c / (1.0 + jnp.exp(-c))).astype(BF16)
    o_ref[0] = jnp.dot(s, w_ref[0].astype(BF16), preferred_element_type=F32) + b_ref[0]


def _ada_mod(cond8, ada_w, ada_b):
    depth, d, n = ada_w.shape
    tn = 1024
    blocks = _nbytes((d, tn), F32) + _nbytes((8, d), F32) + _nbytes((8, tn), F32)
    return pl.pallas_call(
        _ada_kernel,
        grid=(depth, n // tn),
        in_specs=[pl.BlockSpec((8, d), lambda l, j: (0, 0)),
                  pl.BlockSpec((1, d, tn), lambda l, j: (l, 0, j)),
                  pl.BlockSpec((1, 1, tn), lambda l, j: (l, 0, j))],
        out_specs=pl.BlockSpec((1, 8, tn), lambda l, j: (l, 0, j)),
        out_shape=jax.ShapeDtypeStruct((depth, 8, n), F32),
        compiler_params=_params(("arbitrary", "arbitrary"),
                                _vmem_limit(blocks, temp_bytes=_nbytes((d, tn), BF16))),
        name="ada_mod",
    )(cond8, ada_w, ada_b.reshape(depth, 1, n))


def _mod_spec(layer, chunk, tm, trunk, tile_of=lambda i, *_: i):
    per = trunk.rows // trunk.groups // tm

    def idx(*ids):
        return (layer, trunk.mod_row0 + tile_of(*ids) // per, 0, chunk)

    return pl.BlockSpec((1, 1, 1, D_MODEL), idx)


def _normmod(x, g, sc, sh):
    ms = jnp.mean(x * x, axis=-1, keepdims=True)
    y = x * lax.rsqrt(ms + EPS) * g
    return y * (1.0 + sc) + sh


def _rms(x, g):
    return x * lax.rsqrt(jnp.mean(x * x, axis=-1, keepdims=True) + EPS) * g


def _outproj_kernel(n_in, x_ref, g_ref, *refs):
    o_ref = refs[-1]
    acc = None
    for k in range(n_in):
        part = jnp.dot(refs[2 * k][...], refs[2 * k + 1][...], preferred_element_type=F32)
        acc = part if acc is None else acc + part
    o_ref[...] = x_ref[...] + g_ref[0, 0] * acc


def _outproj(x, mod, layer, pairs, trunk, tag):
    t, d = x.shape
    tm = 512
    in_specs = [pl.BlockSpec((tm, d), lambda i: (i, 0)),
                _mod_spec(layer, 2, tm, trunk)]
    args = [x, mod]
    blocks = 2 * _nbytes((tm, d), F32)
    for o, w, blk in pairs:
        k = o.shape[1]
        in_specs += [pl.BlockSpec((tm, k), lambda i: (i, 0)),
                     pl.BlockSpec((k, d), lambda i, blk=blk: (blk, 0))]
        args += [o, w]
        blocks += _nbytes((tm, k), BF16) + _nbytes((k, d), BF16)
    return pl.pallas_call(
        functools.partial(_outproj_kernel, len(pairs)),
        grid=(t // tm,),
        in_specs=in_specs,
        out_specs=pl.BlockSpec((tm, d), lambda i: (i, 0)),
        out_shape=jax.ShapeDtypeStruct((t, d), F32),
        compiler_params=_params(("arbitrary",), _vmem_limit(blocks, 0, 2 * _nbytes((tm, d), F32))),
        name=f"outproj_l{layer}_{tag}",
    )(*args)


def _ffn_kernel(nf, xa_ref, xb_ref, g_ref, sha_ref, sca_ref, shb_ref, scb_ref, gate_ref,
                w1_ref, w2_ref, o_ref, ha_ref, hb_ref, acc_ref):
    p, s = pl.program_id(0), pl.program_id(1)
    rs = xa_ref.shape[0] // nf

    def norm_rows(x_ref, sh_ref, sc_ref, h_ref, rows):
        h_ref[rows] = _normmod(x_ref[rows], g_ref[...], sc_ref[0, 0], sh_ref[0, 0]).astype(BF16)

    def mlp_step(h_ref):
        a = jnp.dot(h_ref[...], w1_ref[...], preferred_element_type=F32)
        a = jnp.square(jnp.maximum(a, 0.0)).astype(BF16)
        acc_ref[...] += jnp.dot(a, w2_ref[...], preferred_element_type=F32)

    def finish(x_ref):
        o_ref[...] = x_ref[...] + gate_ref[0, 0] * acc_ref[...]
        acc_ref[...] = jnp.zeros_like(acc_ref)

    @pl.when((p == 0) & (s == 0))
    def _():
        norm_rows(xa_ref, sha_ref, sca_ref, ha_ref, slice(None))
        acc_ref[...] = jnp.zeros_like(acc_ref)

    @pl.when(s < nf)
    def _():
        mlp_step(ha_ref)
        norm_rows(xb_ref, shb_ref, scb_ref, hb_ref, pl.ds(pl.multiple_of(s * rs, rs), rs))

    @pl.when(s == nf - 1)
    def _():
        finish(xa_ref)

    @pl.when(s >= nf)
    def _():
        mlp_step(hb_ref)
        norm_rows(xa_ref, sha_ref, sca_ref, ha_ref, pl.ds(pl.multiple_of((s - nf) * rs, rs), rs))

    @pl.when(s == 2 * nf - 1)
    def _():
        finish(xb_ref)


def _ffn(x, norm_g, mod, layer, w1, w2, trunk, tag):
    t, d = x.shape
    ff = w1.shape[1]
    tm, tf = 512, 1024
    nf = ff // tf
    n_tiles = t // tm
    assert n_tiles % 2 == 0
    tile_a = lambda p, s: jnp.minimum(2 * (p + s // nf), n_tiles - 2)
    tile_b = lambda p, s: 2 * p + 1
    tile_cur = lambda p, s: 2 * p + s // nf
    blocks = (3 * _nbytes((tm, d), F32) + _nbytes((d, tf), BF16) + _nbytes((tf, d), BF16)
              + 6 * _nbytes((1, d), F32))
    scratch = 2 * _nbytes((tm, d), BF16) + _nbytes((tm, d), F32)
    return pl.pallas_call(
        functools.partial(_ffn_kernel, nf),
        grid=(n_tiles // 2, 2 * nf),
        in_specs=[pl.BlockSpec((tm, d), lambda p, s: (tile_a(p, s), 0)),
                  pl.BlockSpec((tm, d), lambda p, s: (tile_b(p, s), 0)),
                  _const((1, d)),
                  _mod_spec(layer, 3, tm, trunk, tile_a),
                  _mod_spec(layer, 4, tm, trunk, tile_a),
                  _mod_spec(layer, 3, tm, trunk, tile_b),
                  _mod_spec(layer, 4, tm, trunk, tile_b),
                  _mod_spec(layer, 5, tm, trunk, tile_cur),
                  pl.BlockSpec((d, tf), lambda p, s: (0, s % nf)),
                  pl.BlockSpec((tf, d), lambda p, s: (s % nf, 0))],
        out_specs=pl.BlockSpec((tm, d), lambda p, s: (tile_cur(p, s), 0)),
        out_shape=jax.ShapeDtypeStruct((t, d), F32),
        scratch_shapes=[pltpu.VMEM((tm, d), BF16), pltpu.VMEM((tm, d), BF16), pltpu.VMEM((tm, d), F32)],
        compiler_params=_params(("arbitrary", "arbitrary"),
                                _vmem_limit(blocks, scratch, _nbytes((tm, tf), F32) * 2)),
        name=f"ffn_l{layer}_{tag}",
    )(x, x, norm_g.reshape(1, d), mod, mod, mod, mod, mod, w1, w2)


def _rope_tables(seq, pattern):
    pos_row = (jnp.arange(seq) // GRID_W).astype(F32)
    pos_col = (jnp.arange(seq) % GRID_W).astype(F32)
    cos_cols, sa_cols, sb_cols = [], [], []
    for width, kind in pattern:
        if kind == "none":
            cos_cols.append(jnp.ones((seq, width), F32))
            sa_cols.append(jnp.zeros((seq, width), F32))
            sb_cols.append(jnp.zeros((seq, width), F32))
            continue
        half = width // 2
        inv = ROPE_BASE ** (-jnp.arange(half, dtype=F32) / half)
        pos = pos_row if kind == "row" else pos_col
        ang = pos[:, None] * inv
        cos, sin = jnp.cos(ang), jnp.sin(ang)
        zero = jnp.zeros_like(sin)
        cos_cols += [cos, cos]
        sa_cols += [-sin, zero]
        sb_cols += [zero, sin]
    tabs = [jnp.concatenate(c, axis=1) for c in (cos_cols, sa_cols, sb_cols)]
    assert tabs[0].shape == (seq, LANES)
    return tabs


def _rope(x, tabs, half):
    cos, sin_a, sin_b = tabs
    return (x * cos + pltpu.roll(x, LANES - half, 1) * sin_a + pltpu.roll(x, half, 1) * sin_b)


def _mla_keys(kv_dot, kr, krg, gkn_ref, km_ref, vm_ref, rows):
    ss_kr = jnp.sum(kr * kr, axis=-1, keepdims=True)
    for pair in range(MLA_HEADS // 2):
        kn2 = kv_dot(pair * MXU_N)
        for s in range(2):
            h = 2 * pair + s
            kn = kn2[:, s * LANES:(s + 1) * LANES]
            r = lax.rsqrt((jnp.sum(kn * kn, axis=-1, keepdims=True) + ss_kr) / MLA_QK + EPS)
            km_ref[rows, h * MLA_SLAB:h * MLA_SLAB + LANES] = (kn * r * gkn_ref[...]).astype(BF16)
            km_ref[rows, h * MLA_SLAB + LANES:(h + 1) * MLA_SLAB] = (krg * r).astype(BF16)
    for c in range(MLA_HEADS * MLA_V // MXU_N):
        v = kv_dot(MLA_HEADS * MLA_NOPE + c * MXU_N)
        vm_ref[rows, c * MXU_N:(c + 1) * MXU_N] = v.astype(BF16)


def _front_ab_kernel(rope, n_chunks, x_ref, g_ref, sh_ref, sc_ref, w_ref, qn_ref, kn_ref, mqn_ref,
                     wq_ref, mkvn_ref, wkv_ref, gq_ref, gkn_ref, gkr_ref, *refs):
    if rope:
        tab_refs, outs, h_ref = refs[:6], refs[6:12], refs[12]
    else:
        tab_refs, outs, h_ref = (), refs[:10], refs[10]
    qda_ref, kda_ref, vda_ref, qm_ref, km_ref, vm_ref = outs[:6]
    rc = x_ref.shape[0] // n_chunks
    lo = lax.broadcasted_iota(jnp.int32, (1, HEAD_W), 1) < DA_QK_DIM
    qn_c = qn_ref[...] * (DA_QK_DIM ** -0.5 * LOG2E)
    gq_c = gq_ref[...] * (MLA_QK ** -0.5 * LOG2E)

    def da_norm(x, g):
        sq = x * x
        s_lo = jnp.sum(jnp.where(lo, sq, 0.0), axis=-1, keepdims=True)
        s_hi = jnp.sum(jnp.where(lo, 0.0, sq), axis=-1, keepdims=True)
        r = jnp.where(lo, lax.rsqrt(s_lo / DA_QK_DIM + EPS), lax.rsqrt(s_hi / DA_QK_DIM + EPS))
        return x * r * g

    for c in range(n_chunks):
        rows = slice(c * rc, (c + 1) * rc)
        h_ref[rows] = _normmod(x_ref[rows], g_ref[...], sc_ref[0, 0], sh_ref[0, 0]).astype(BF16)

        def hdot(c0, width=MXU_N):
            return jnp.dot(h_ref[rows], w_ref[:, c0:c0 + width], preferred_element_type=F32)

        if rope:
            tabs_da = [t[rows] for t in tab_refs[:3]]
            tabs_mla = [t[rows] for t in tab_refs[3:]]

        for pair in range(DA_HEADS // 2):
            pq = hdot(pair * MXU_N)
            pk = hdot(DA_W + pair * MXU_N)
            for s in range(2):
                sl = slice((2 * pair + s) * HEAD_W, (2 * pair + s + 1) * HEAD_W)
                q = da_norm(pq[:, s * HEAD_W:(s + 1) * HEAD_W], qn_c)
                k = da_norm(pk[:, s * HEAD_W:(s + 1) * HEAD_W], kn_ref[...])
                if rope:
                    q = _rope(q, tabs_da, DA_QK_DIM // 4)
                    k = _rope(k, tabs_da, DA_QK_DIM // 4)
                else:
                    outs[6][rows, sl] = k
                qda_ref[rows, sl] = q.astype(BF16)
                kda_ref[rows, sl] = k.astype(BF16)
        for cc in range(DA_W // MXU_N):
            dv = hdot(2 * DA_W + cc * MXU_N)
            vda_ref[rows, cc * MXU_N:(cc + 1) * MXU_N] = dv.astype(BF16)
            if not rope:
                outs[7][rows, cc * MXU_N:(cc + 1) * MXU_N] = dv

        mq = _rms(hdot(3 * DA_W, MLA_RANK), mqn_ref[...]).astype(BF16)
        for h in range(MLA_HEADS):
            qf = jnp.dot(mq, wq_ref[:, h * MLA_SLAB:(h + 1) * MLA_SLAB], preferred_element_type=F32)
            a, b = qf[:, :LANES], qf[:, LANES:]
            ss = jnp.sum(a * a, axis=-1, keepdims=True) + jnp.sum(b * b, axis=-1, keepdims=True)
            r = lax.rsqrt(ss / MLA_QK + EPS)
            a = a * r * gq_c[:, :LANES]
            b = b * r * gq_c[:, LANES:]
            if rope:
                b = _rope(b, tabs_mla, MLA_ROPE // 4)
            qm_ref[rows, h * MLA_SLAB:h * MLA_SLAB + LANES] = a.astype(BF16)
            qm_ref[rows, h * MLA_SLAB + LANES:(h + 1) * MLA_SLAB] = b.astype(BF16)

        ckv = _rms(hdot(3 * DA_W + MLA_RANK, MLA_RANK), mkvn_ref[...])
        kr = hdot(3 * DA_W + 2 * MLA_RANK, LANES)
        if not rope:
            outs[8][rows] = ckv
            outs[9][rows] = kr[:, :MLA_ROPE]
        ckv_b = ckv.astype(BF16)
        krg = kr * gkr_ref[...]
        if rope:
            krg = _rope(krg, tabs_mla, MLA_ROPE // 4)

        def kv_dot(c0):
            return jnp.dot(ckv_b, wkv_ref[:, c0:c0 + MXU_N], preferred_element_type=F32)

        _mla_keys(kv_dot, kr, krg, gkn_ref, km_ref, vm_ref, rows)


def _front_ab(x, norm_g, mod, trunk, w, tabs, tag):
    tm, n_chunks = 256, 2
    rows, rope, d = trunk.rows, trunk.rope, D_MODEL
    in_specs = [pl.BlockSpec((tm, d), lambda i: (i, 0)), _const((1, d)),
                _mod_spec(0, 0, tm, trunk), _mod_spec(0, 1, tm, trunk),
                _resident((d, AB_IN_PAD)),
                _const((1, HEAD_W)), _const((1, HEAD_W)), _const((1, MLA_RANK)),
                _resident((MLA_RANK, MLA_HEADS * MLA_SLAB)), _const((1, MLA_RANK)),
                _resident((MLA_RANK, MLA_HEADS * (MLA_NOPE + MLA_V))),
                _const((1, MLA_SLAB)), _const((1, LANES)), _const((1, LANES))]
    args = [x, norm_g.reshape(1, d), mod, mod, w["w_in0"], w["da_qn"], w["da_kn"], w["mq_norm"],
            w["wq"], w["mkv_norm"], w["wkv"], w["gq"], w["gkn"], w["gkr"]]
    if rope:
        per = trunk.seq // tm
        in_specs += [pl.BlockSpec((tm, LANES), lambda i: (i % per, 0))] * 6
        args += list(tabs)
    row = lambda n: pl.BlockSpec((tm, n), lambda i: (i, 0))
    widths = (DA_W, DA_W, DA_W, MLA_HEADS * MLA_SLAB, MLA_HEADS * MLA_SLAB, MLA_HEADS * MLA_V)
    out_specs = [row(n) for n in widths]
    out_shape = [jax.ShapeDtypeStruct((rows, n), BF16) for n in widths]
    out_bytes = sum(_nbytes((tm, n), BF16) for n in widths)
    if not rope:
        cache_w = (DA_W, DA_W, MLA_RANK, MLA_ROPE)
        out_specs += [row(n) for n in cache_w]
        out_shape += [jax.ShapeDtypeStruct((rows, n), F32) for n in cache_w]
        out_bytes += sum(_nbytes((tm, n), F32) for n in cache_w)
    resident = _nbytes((d, AB_IN_PAD), BF16) + 2 * _nbytes((MLA_RANK, 2048), BF16)
    blocks = _nbytes((tm, d), F32) + out_bytes + 6 * _nbytes((tm, LANES), F32)
    return pl.pallas_call(
        functools.partial(_front_ab_kernel, rope, n_chunks),
        grid=(rows // tm,),
        in_specs=in_specs,
        out_specs=out_specs,
        out_shape=out_shape,
        scratch_shapes=[pltpu.VMEM((tm, d), BF16)],
        compiler_params=_params(("arbitrary",),
                                _vmem_limit(blocks, resident + _nbytes((tm, d), BF16),
                                            4 * _nbytes((tm, d), F32))),
        name=f"front_ab_{tag}",
    )(*args)


def _ctx_mla_kernel(ckv_ref, kr_ref, wkv_ref, gkn_ref, gkr_ref, km_ref, vm_ref):
    ckv_b = ckv_ref[...].astype(BF16)
    kr = kr_ref[...]

    def kv_dot(c0):
        return jnp.dot(ckv_b, wkv_ref[:, c0:c0 + MXU_N], preferred_element_type=F32)

    _mla_keys(kv_dot, kr, kr * gkr_ref[...], gkn_ref, km_ref, vm_ref, slice(None))


def _ctx_mla(ckv, kr128, w):
    rows = ckv.shape[0]
    tm = 256
    blocks = (_nbytes((tm, MLA_RANK + LANES), F32) + _nbytes((MLA_RANK, 2048), BF16)
              + _nbytes((tm, 3072), BF16))
    return pl.pallas_call(
        _ctx_mla_kernel,
        grid=(rows // tm,),
        in_specs=[pl.BlockSpec((tm, MLA_RANK), lambda i: (i, 0)),
                  pl.BlockSpec((tm, LANES), lambda i: (i, 0)),
                  _const((MLA_RANK, 2048)), _const((1, LANES)), _const((1, LANES))],
        out_specs=[pl.BlockSpec((tm, 2048), lambda i: (i, 0)),
                   pl.BlockSpec((tm, 1024), lambda i: (i, 0))],
        out_shape=[jax.ShapeDtypeStruct((rows, 2048), BF16),
                   jax.ShapeDtypeStruct((rows, 1024), BF16)],
        compiler_params=_params(("arbitrary",), _vmem_limit(blocks, 0, 2 * _nbytes((tm, 2048), F32))),
        name="ctx_mla",
    )(ckv, kr128, w["wkv"], w["gkn"], w["gkr"])


def _front_c_kernel(rope, n_chunks, x_ref, g_ref, sh_ref, sc_ref, w_ref, qn_ref, kn_ref, *refs):
    if rope:
        tab_refs, outs, h_ref = refs[:3], refs[3:6], refs[6]
    else:
        tab_refs, outs, h_ref = (), refs[:5], refs[5]
    q_ref, k_ref, v_ref = outs[:3]
    rc = x_ref.shape[0] // n_chunks
    nq = GQ_HEADS * GQ_DIM
    nk = GQ_KV_HEADS * GQ_DIM
    qn_c = qn_ref[...] * (GQ_DIM ** -0.5 * LOG2E)
    for c in range(n_chunks):
        rows = slice(c * rc, (c + 1) * rc)
        h_ref[rows] = _normmod(x_ref[rows], g_ref[...], sc_ref[0, 0], sh_ref[0, 0]).astype(BF16)

        def hdot(c0):
            return jnp.dot(h_ref[rows], w_ref[:, c0:c0 + MXU_N], preferred_element_type=F32)

        if rope:
            tabs = [t[rows] for t in tab_refs]
        for pair in range(GQ_HEADS // 2):
            pq = hdot(pair * MXU_N)
            for s in range(2):
                sl = slice((2 * pair + s) * GQ_DIM, (2 * pair + s + 1) * GQ_DIM)
                q = _rms(pq[:, s * GQ_DIM:(s + 1) * GQ_DIM], qn_c)
                if rope:
                    q = _rope(q, tabs, GQ_DIM // 4)
                q_ref[rows, sl] = q.astype(BF16)
        for pair in range(GQ_KV_HEADS // 2):
            pk = hdot(nq + pair * MXU_N)
            for s in range(2):
                sl = slice((2 * pair + s) * GQ_DIM, (2 * pair + s + 1) * GQ_DIM)
                k = _rms(pk[:, s * GQ_DIM:(s + 1) * GQ_DIM], kn_ref[...])
                if rope:
                    k = _rope(k, tabs, GQ_DIM // 4)
                else:
                    outs[3][rows, sl] = k
                k_ref[rows, sl] = k.astype(BF16)
        for cc in range(nk // MXU_N):
            v = hdot(nq + nk + cc * MXU_N)
            v_ref[rows, cc * MXU_N:(cc + 1) * MXU_N] = v.astype(BF16)
            if not rope:
                outs[4][rows, cc * MXU_N:(cc + 1) * MXU_N] = v


def _front_c(x, norm_g, mod, trunk, w, tabs, tag, cast=None):
    tm, n_chunks = 256, 2
    rows, rope, d = trunk.rows, trunk.rope, D_MODEL
    nq = GQ_HEADS * GQ_DIM
    nk = GQ_KV_HEADS * GQ_DIM
    n = nq + 2 * nk
    in_specs = [pl.BlockSpec((tm, d), lambda i: (i, 0)), _const((1, d)),
                _mod_spec(1, 0, tm, trunk), _mod_spec(1, 1, tm, trunk),
                _resident((d, n)), _const((1, GQ_DIM)), _const((1, GQ_DIM))]
    args = [x, norm_g.reshape(1, d), mod, mod, w["w_in1"], w["gq_qn"], w["gq_kn"]]
    if rope:
        per = trunk.seq // tm
        in_specs += [pl.BlockSpec((tm, LANES), lambda i: (i % per, 0))] * 3
        args += list(tabs)
    row = lambda wd: pl.BlockSpec((tm, wd), lambda i: (i, 0))
    out_specs = [row(nq), row(nk), row(nk)]
    out_shape = [jax.ShapeDtypeStruct((rows, wd), BF16) for wd in (nq, nk, nk)]
    out_bytes = _nbytes((tm, n), BF16)
    if not rope:
        out_specs += [row(nk), row(nk)]
        out_shape += [jax.ShapeDtypeStruct((rows, nk), F32)] * 2
        out_bytes += 2 * _nbytes((tm, nk), F32)
    blocks = _nbytes((tm, d), F32) + out_bytes + 3 * _nbytes((tm, LANES), F32)
    body = functools.partial(_front_c_kernel, rope, n_chunks)
    if cast is not None:
        c_in, c_out, c_shape, c_bytes = _side_cast_specs(cast, rows // tm, lambda i: i)
        body = _with_side_cast(body, len(args), len(out_specs))
        in_specs.append(c_in)
        args.append(cast[0])
        out_specs.append(c_out)
        out_shape.append(c_shape)
        blocks += c_bytes
    return pl.pallas_call(
        body,
        grid=(rows // tm,),
        in_specs=in_specs,
        out_specs=out_specs,
        out_shape=out_shape,
        scratch_shapes=[pltpu.VMEM((tm, d), BF16)],
        compiler_params=_params(("arbitrary",),
                                _vmem_limit(blocks, _nbytes((d, n), BF16) + _nbytes((tm, d), BF16),
                                            4 * _nbytes((tm, d), F32))),
        name=f"front_c_{tag}",
    )(*args)


def _dot_nt(a, b):
    return lax.dot_general(a, b, (((1,), (1,)), ((), ())), preferred_element_type=F32)


def _attn_rows(q, parts):
    scores = [_dot_nt(q, k) for k, _ in parts]
    m = functools.reduce(jnp.maximum, [jnp.max(s, axis=-1, keepdims=True) for s in scores])
    o1 = None
    for s, (_, v) in zip(scores, parts):
        e = jnp.exp2(s - m).astype(BF16)
        v1 = jnp.concatenate([v, jnp.ones((v.shape[0], LANES), BF16)], axis=1)
        part = jnp.dot(e, v1, preferred_element_type=F32)
        o1 = part if o1 is None else o1 + part
    dv = parts[0][1].shape[1]
    return o1[:, :dv], o1[:, dv:]


ONES_ROWS = 16


def _attn_cols(q, parts, mask=None, sink=None):
    scores = [_dot_nt(k, q) for k, _ in parts]
    if mask is not None:
        scores[-1] = jnp.where(mask, scores[-1], NEG_INF)
    m = functools.reduce(jnp.maximum, [jnp.max(s, axis=0, keepdims=True) for s in scores])
    if sink is not None:
        sink2 = sink * LOG2E
        m = jnp.maximum(m, sink2)
    o1 = None
    for s, (_, v) in zip(scores, parts):
        e = jnp.exp2(s - m).astype(BF16)
        v_t1 = jnp.concatenate([v.astype(F32).T.astype(BF16),
                                jnp.ones((ONES_ROWS, v.shape[0]), BF16)], axis=0)
        part = jnp.dot(v_t1, e, preferred_element_type=F32)
        o1 = part if o1 is None else o1 + part
    dv = parts[0][1].shape[1]
    den = o1[dv:dv + 1]
    if sink is not None:
        den = den + jnp.exp2(sink2 - m)
    return o1[:dv], den


def _kv_parts(refs, n_parts, sl_k, sl_v):
    return [(refs[2 * p][:, sl_k].astype(BF16), refs[2 * p + 1][:, sl_v].astype(BF16))
            for p in range(n_parts)]


def _da_attn_kernel(n_parts, lam_init, lam_ref, gsub_ref, q_ref, *refs):
    o_ref = refs[-1]
    tq = q_ref.shape[0]
    lv = lam_ref[...]
    lam = (jnp.exp(jnp.sum(lv[0:1] * lv[1:2], axis=-1, keepdims=True))
           - jnp.exp(jnp.sum(lv[2:3] * lv[3:4], axis=-1, keepdims=True)) + lam_init)
    lo = lax.broadcasted_iota(jnp.int32, (1, HEAD_W), 1) < DA_QK_DIM
    for h in range(DA_HEADS):
        sl = slice(h * HEAD_W, (h + 1) * HEAD_W)
        q = q_ref[:, sl]
        zero = jnp.zeros_like(q)
        q12 = jnp.concatenate([jnp.where(lo, q, zero), jnp.where(lo, zero, q)], axis=0)
        o12, d12 = _attn_rows(q12, _kv_parts(refs, n_parts, sl, sl))
        o12 = o12 * (1.0 / d12)
        o = o12[:tq] - lam * o12[tq:]
        y = o * lax.rsqrt(jnp.mean(o * o, axis=-1, keepdims=True) + EPS) * gsub_ref[...]
        o_ref[:, sl] = (y * (1.0 - lam_init)).astype(BF16)


def _mla_attn_kernel(n_parts, q_ref, *refs):
    o_ref = refs[-1]
    for h in range(MLA_HEADS):
        sl_k = slice(h * MLA_SLAB, (h + 1) * MLA_SLAB)
        sl_v = slice(h * MLA_V, (h + 1) * MLA_V)
        o, den = _attn_rows(q_ref[:, sl_k], _kv_parts(refs, n_parts, sl_k, sl_v))
        o_ref[:, sl_v] = (o * (1.0 / den)).astype(BF16)


def _gq_attn_kernel(n_parts, seq, kw, sink_ref, q_ref, *refs):
    o_ref = refs[-1]
    tq = q_ref.shape[0]
    qi = pl.program_id(1)
    mask = None
    if n_parts == 2:
        start = pl.multiple_of(jnp.clip(qi * tq - WINDOW, 0, seq - kw), WINDOW)
        keys = start + lax.broadcasted_iota(jnp.int32, (kw, GQ_GROUP * tq), 0)
        qrows = qi * tq + (lax.broadcasted_iota(jnp.int32, (kw, GQ_GROUP * tq), 1) & (tq - 1))
        mask = jnp.abs(qrows - keys) <= WINDOW
    for g in range(GQ_KV_HEADS):
        sl = slice(g * GQ_DIM, (g + 1) * GQ_DIM)
        heads = range(g * GQ_GROUP, (g + 1) * GQ_GROUP)
        q4 = jnp.concatenate([q_ref[:, j * GQ_DIM:(j + 1) * GQ_DIM] for j in heads], axis=0)
        sink = jnp.concatenate([jnp.broadcast_to(sink_ref[j:j + 1, 0:1], (1, tq)) for j in heads], axis=1)
        if n_parts == 2:
            parts = [(refs[0][:, sl].astype(BF16), refs[1][:, sl]),
                     (refs[2][pl.ds(start, kw), sl], refs[3][pl.ds(start, kw), sl])]
        else:
            parts = [(refs[0][:, sl], refs[1][:, sl])]
        o, den = _attn_cols(q4, parts, mask=mask, sink=sink)
        o = o * (1.0 / den)
        for n, j in enumerate(heads):
            o_ref[:, j * GQ_DIM:(j + 1) * GQ_DIM] = o[:, n * tq:(n + 1) * tq].T.astype(BF16)


def _attention(kernel, name, q, kv_parts, trunk, tq, out_w, stacked_rows, extra_in=(), cast=None):
    nq = trunk.seq // tq
    in_specs = [_const(a.shape) for a in extra_in]
    args = list(extra_in)
    qw = q.shape[1]
    in_specs.append(pl.BlockSpec((tq, qw), lambda b, i: (b * nq + i, 0)))
    args.append(q)
    blocks = _nbytes((tq, qw), BF16) + _nbytes((tq, out_w), BF16)
    total_l = 0
    for k, v, l in kv_parts:
        in_specs += [pl.BlockSpec((l, k.shape[1]), lambda b, i: (b, 0)),
                     pl.BlockSpec((l, v.shape[1]), lambda b, i: (b, 0))]
        args += [k, v]
        blocks += _nbytes((l, k.shape[1]), k.dtype) + _nbytes((l, v.shape[1]), v.dtype)
        total_l += l
    out_specs = [pl.BlockSpec((tq, out_w), lambda b, i: (b * nq + i, 0))]
    out_shape = [jax.ShapeDtypeStruct((trunk.rows, out_w), BF16)]
    if cast is not None:
        c_in, c_out, c_shape, c_bytes = _side_cast_specs(cast, trunk.batch * nq, lambda b, i: b * nq + i)
        kernel = _with_side_cast(kernel, len(args), 1)
        in_specs.append(c_in)
        args.append(cast[0])
        out_specs.append(c_out)
        out_shape.append(c_shape)
        blocks += c_bytes
    out = pl.pallas_call(
        kernel,
        grid=(trunk.batch, nq),
        in_specs=in_specs,
        out_specs=out_specs,
        out_shape=out_shape,
        compiler_params=_params(("arbitrary",) * 2,
                                _vmem_limit(blocks, 0, 6 * _nbytes((stacked_rows, total_l), F32))),
        name=name,
    )(*args)
    return out if cast is not None else out[0]


def _run_trunk(x, trunk, tag, mod, P, ctx, ffw):
    tq = min(trunk.seq, 256)
    casting = ffw is None
    ffw = dict(ffw or {})
    cast = lambda name, layer: (P[name], layer) if casting else None
    front = _front_ab(x, P["norm1_g"][0], mod, trunk, P, P["tabs_ab"] if trunk.rope else None, tag)
    qda, kda, vda, qm, km, vm = front[:6]
    da_parts, mla_parts = [(kda, vda, trunk.seq)], [(km, vm, trunk.seq)]
    if ctx is not None:
        da_parts = [(ctx["da_k"], ctx["da_v"], ctx["past"])] + da_parts
        mla_parts = [(ctx["mla_k"], ctx["mla_v"], ctx["past"])] + mla_parts
    o_da = _attention(functools.partial(_da_attn_kernel, len(da_parts), P["lam_init"]),
                      f"da_attn_{tag}", qda, da_parts, trunk, tq, DA_HEADS * HEAD_W, 2 * tq,
                      extra_in=(P["lam4"], P["gsub"]), cast=cast("ff1_f32", 0))
    o_m = _attention(functools.partial(_mla_attn_kernel, len(mla_parts)),
                     f"mla_attn_{tag}", qm, mla_parts, trunk, tq, MLA_HEADS * MLA_V, tq,
                     cast=cast("ff2_f32", 0))
    if casting:
        (o_da, ffw["ff1_0"]), (o_m, ffw["ff2_0"]) = o_da, o_m
    x = _outproj(x, mod, 0, [(o_da, P["w_out0"], 0), (o_m, P["w_out0"], 1)], trunk, tag)
    x = _ffn(x, P["norm2_g"][0], mod, 0, ffw["ff1_0"], ffw["ff2_0"], trunk, tag)

    front_c = _front_c(x, P["norm1_g"][1], mod, trunk, P, P["tabs_gq"] if trunk.rope else None, tag,
                       cast=cast("ff2_f32", 1))
    if casting:
        front_c, ffw["ff2_1"] = front_c[:-1], front_c[-1]
    qc, kc, vc = front_c[:3]
    gq_parts = [(kc, vc, trunk.seq)]
    if ctx is not None:
        gq_parts = [(ctx["gq_k"], ctx["gq_v"], ctx["past"])] + gq_parts
    kw = min(trunk.seq, tq + 2 * WINDOW)
    o_c = _attention(functools.partial(_gq_attn_kernel, len(gq_parts), trunk.seq, kw),
                     f"gq_attn_{tag}", qc, gq_parts, trunk, tq, GQ_HEADS * GQ_DIM, GQ_GROUP * tq,
                     extra_in=(P["sink"],), cast=cast("ff1_f32", 1))
    if casting:
        o_c, ffw["ff1_1"] = o_c
    x = _outproj(x, mod, 1, [(o_c, P["w_out1"], 0)], trunk, tag)
    x = _ffn(x, P["norm2_g"][1], mod, 1, ffw["ff1_1"], ffw["ff2_1"], trunk, tag)
    return x, front[6:], front_c[3:], ffw


def kernel(x_prompt, x_sample, cache_da_k, cache_da_v, cache_mla_ckv, cache_mla_krope, cache_gq_k, cache_gq_v, c, c_ctx, norm1_g, norm2_g, ada_w, ada_b, ff1_w, ff2_w, ab_w_in, ab_w_out, da_lambda_q1, da_lambda_k1, da_lambda_q2, da_lambda_k2, da_q_norm, da_k_norm, da_subln, mla_q_a_norm, mla_w_q_up, mla_kv_a_norm, mla_w_kv_up, mla_q_norm, mla_k_norm, c_w_in, c_w_out, gq_q_norm, gq_k_norm, gq_sink):
    pb, ps, d = x_prompt.shape
    sb, ss, _ = x_sample.shape
    past = cache_da_k.shape[2]
    assert sb + 1 <= 8 and d == D_MODEL

    cond8 = jnp.concatenate([c_ctx[None], c, jnp.zeros((8 - 1 - sb, d), F32)], axis=0)
    wq = jnp.pad(mla_w_q_up[0].reshape(MLA_RANK, MLA_HEADS, MLA_QK),
                 ((0, 0), (0, 0), (0, MLA_SLAB - MLA_QK))).reshape(MLA_RANK, MLA_HEADS * MLA_SLAB)
    wkv3 = mla_w_kv_up[0].reshape(MLA_RANK, MLA_HEADS, MLA_NOPE + MLA_V)
    wkv = jnp.concatenate([wkv3[..., :MLA_NOPE].reshape(MLA_RANK, -1),
                           wkv3[..., MLA_NOPE:].reshape(MLA_RANK, -1)], axis=1)
    P = {
        "norm1_g": norm1_g, "norm2_g": norm2_g,
        "w_in0": jnp.pad(ab_w_in[0], ((0, 0), (0, AB_IN_PAD - AB_IN))).astype(BF16),
        "w_out0": ab_w_out[0].astype(BF16),
        "w_in1": c_w_in[0].astype(BF16), "w_out1": c_w_out[0].astype(BF16),
        "ff1_f32": ff1_w, "ff2_f32": ff2_w,
        "da_qn": jnp.tile(da_q_norm[0], 2).reshape(1, HEAD_W),
        "da_kn": jnp.tile(da_k_norm[0], 2).reshape(1, HEAD_W),
        "mq_norm": mla_q_a_norm[0].reshape(1, MLA_RANK),
        "mkv_norm": mla_kv_a_norm[0].reshape(1, MLA_RANK),
        "wq": wq.astype(BF16), "wkv": wkv.astype(BF16),
        "gq": jnp.pad(mla_q_norm[0], (0, MLA_SLAB - MLA_QK)).reshape(1, MLA_SLAB),
        "gkn": mla_k_norm[0, :MLA_NOPE].reshape(1, LANES),
        "gkr": jnp.pad(mla_k_norm[0, MLA_NOPE:], (0, LANES - MLA_ROPE)).reshape(1, LANES),
        "lam4": jnp.stack([da_lambda_q1[0], da_lambda_k1[0], da_lambda_q2[0], da_lambda_k2[0]]),
        "gsub": da_subln[0].reshape(1, HEAD_W),
        "lam_init": 0.8 - 0.6 * math.exp(-0.3 * 0),
        "gq_qn": gq_q_norm[0].reshape(1, GQ_DIM), "gq_kn": gq_k_norm[0].reshape(1, GQ_DIM),
        "sink": jnp.broadcast_to(gq_sink[0].reshape(GQ_HEADS, 1), (GQ_HEADS, LANES)),
        "tabs_ab": (_rope_tables(ss, [(32, "row"), (32, "col"), (32, "row"), (32, "col")])
                    + _rope_tables(ss, [(32, "row"), (32, "col"), (32, "none"), (32, "none")])),
        "tabs_gq": _rope_tables(ss, [(64, "row"), (64, "col")]),
    }

    mod = _ada_mod(cond8, ada_w, ada_b).reshape(2, 8, 1, 6 * d)

    kr_ctx = jnp.pad(cache_mla_krope[:, 0].reshape(sb * past, MLA_ROPE), ((0, 0), (0, LANES - MLA_ROPE)))
    mla_k_ctx, mla_v_ctx = _ctx_mla(cache_mla_ckv[:, 0].reshape(sb * past, MLA_RANK), kr_ctx, P)
    ctx = {
        "past": past,
        "da_k": cache_da_k[:, 0].reshape(sb * past, -1), "da_v": cache_da_v[:, 0].reshape(sb * past, -1),
        "mla_k": mla_k_ctx, "mla_v": mla_v_ctx,
        "gq_k": cache_gq_k[:, 0].reshape(sb * past, -1), "gq_v": cache_gq_v[:, 0].reshape(sb * past, -1),
    }

    prompt = Trunk(groups=1, seq=ps, batch=pb, mod_row0=0, rope=False)
    sample = Trunk(groups=sb, seq=ss, batch=sb, mod_row0=1, rope=True)
    y_p, (new_da_k, new_da_v, new_ckv, new_kr), (new_gq_k, new_gq_v), ffw = _run_trunk(
        x_prompt.reshape(pb * ps, d), prompt, "prompt", mod, P, None, None)
    y_s, _, _, _ = _run_trunk(x_sample.reshape(sb * ss, d), sample, "sample", mod, P, ctx, ffw)

    return (y_p.reshape(pb, ps, d), y_s.reshape(sb, ss, d),
            new_da_k.reshape(pb, 1, ps, DA_HEADS, HEAD_W), new_da_v.reshape(pb, 1, ps, DA_HEADS, HEAD_W),
            new_ckv.reshape(pb, 1, ps, MLA_RANK), new_kr.reshape(pb, 1, ps, MLA_ROPE),
            new_gq_k.reshape(pb, 1, ps, GQ_KV_HEADS, GQ_DIM), new_gq_v.reshape(pb, 1, ps, GQ_KV_HEADS, GQ_DIM))
```

```python
import functools
import math
from typing import NamedTuple

import jax
import jax.numpy as jnp
import numpy as np
from jax import lax
from jax.experimental import pallas as pl
from jax.experimental.pallas import tpu as pltpu

F32 = jnp.float32
BF16 = jnp.bfloat16

D_MODEL = 2048
GRID_W = 64
ROPE_BASE = 10000.0
EPS = 1e-6
NEG_INF = -1e30
LOG2E = math.log2(math.e)
DA_HEADS = 8
DA_QK_DIM = 64
DA_W = DA_HEADS * 2 * DA_QK_DIM
MLA_HEADS = 8
MLA_RANK = 512
MLA_NOPE = 128
MLA_ROPE = 64
MLA_V = 128
MLA_QK = MLA_NOPE + MLA_ROPE
MLA_SLAB = 256
GQ_HEADS = 16
GQ_KV_HEADS = 4
GQ_GROUP = GQ_HEADS // GQ_KV_HEADS
GQ_DIM = 128
WINDOW = 128
HEAD_W = 128
AB_IN = 4160
AB_IN_PAD = 4224

LANES = 128
MXU_N = 256
VMEM_CAP_BYTES = 60 * 1024 * 1024


class Trunk(NamedTuple):
    groups: int
    seq: int
    batch: int
    mod_row0: int
    rope: bool

    @property
    def rows(self):
        return self.batch * self.seq


def _vmem_limit(block_bytes, scratch_bytes=0, temp_bytes=0):
    est = 2 * block_bytes + scratch_bytes + temp_bytes + (4 << 20)
    return int(min(max(est, 16 << 20), VMEM_CAP_BYTES))


def _nbytes(shape, dtype):
    return math.prod(shape) * jnp.dtype(dtype).itemsize


def _params(sem, vmem):
    return pltpu.CompilerParams(dimension_semantics=sem, vmem_limit_bytes=vmem)


def _resident(shape):
    return pl.BlockSpec(shape, lambda *_: (0,) * len(shape), pipeline_mode=pl.Buffered(1))


def _const(shape):
    return pl.BlockSpec(shape, lambda *_: (0,) * len(shape))


def _add_side_casts(kernel, casts, steps, step_of, in_specs, args, out_specs, out_shape):
    n_in, n_out, n_jobs = len(args), len(out_specs), len(casts)
    extra = 0
    for w, layer in casts:
        _, rows, cols = w.shape
        blk = rows // steps
        in_specs.append(pl.BlockSpec((None, blk, cols), lambda *ids, l=layer: (l, step_of(*ids), 0)))
        args.append(w)
        out_specs.append(pl.BlockSpec((blk, cols), lambda *ids: (step_of(*ids), 0)))
        out_shape.append(jax.ShapeDtypeStruct((rows, cols), BF16))
        extra += _nbytes((blk, cols), F32) + _nbytes((blk, cols), BF16)

    def wrapped(*refs):
        srcs = refs[n_in:n_in + n_jobs]
        dsts = refs[n_in + n_jobs + n_out:n_in + 2 * n_jobs + n_out]
        for src, dst in zip(srcs, dsts):
            dst[...] = src[...].astype(BF16)
        kernel(*refs[:n_in], *refs[n_in + n_jobs:n_in + n_jobs + n_out], *refs[n_in + 2 * n_jobs + n_out:])

    return (wrapped if casts else kernel), extra


def _ada_kernel(c_ref, w_ref, b_ref, o_ref):
    c = c_ref[...]
    s = (c / (1.0 + jnp.exp(-c))).astype(BF16)
    o_ref[0] = jnp.dot(s, w_ref[0].astype(BF16), preferred_element_type=F32) + b_ref[0]


def _ada_mod(cond8, ada_w, ada_b):
    depth, d, n = ada_w.shape
    tn = 1024
    blocks = _nbytes((d, tn), F32) + _nbytes((8, d), F32) + _nbytes((8, tn), F32)
    return pl.pallas_call(
        _ada_kernel,
        grid=(depth, n // tn),
        in_specs=[pl.BlockSpec((8, d), lambda l, j: (0, 0)),
                  pl.BlockSpec((1, d, tn), lambda l, j: (l, 0, j)),
                  pl.BlockSpec((1, 1, tn), lambda l, j: (l, 0, j))],
        out_specs=pl.BlockSpec((1, 8, tn), lambda l, j: (l, 0, j)),
        out_shape=jax.ShapeDtypeStruct((depth, 8, n), F32),
        compiler_params=_params(("arbitrary", "arbitrary"),
                                _vmem_limit(blocks, temp_bytes=_nbytes((d, tn), BF16))),
        name="ada_mod",
    )(cond8, ada_w, ada_b.reshape(depth, 1, n))


def _mod_spec(layer, chunk, tm, trunk):
    per = trunk.rows // trunk.groups // tm

    def idx(i, *_):
        return (layer, trunk.mod_row0 + i // per, 0, chunk)

    return pl.BlockSpec((1, 1, 1, D_MODEL), idx)


def _normmod(x, g, sc, sh):
    ms = jnp.mean(x * x, axis=-1, keepdims=True)
    y = x * lax.rsqrt(ms + EPS) * g
    return y * (1.0 + sc) + sh


def _rms(x, g):
    return x * lax.rsqrt(jnp.mean(x * x, axis=-1, keepdims=True) + EPS) * g


def _outproj_kernel(n_in, x_ref, g_ref, *refs):
    o_ref = refs[-1]
    acc = None
    for k in range(n_in):
        part = jnp.dot(refs[2 * k][...], refs[2 * k + 1][...], preferred_element_type=F32)
        acc = part if acc is None else acc + part
    o_ref[...] = x_ref[...] + g_ref[0, 0] * acc


def _outproj(x, mod, layer, pairs, trunk, tag):
    t, d = x.shape
    tm = 512
    in_specs = [pl.BlockSpec((tm, d), lambda i: (i, 0)),
                _mod_spec(layer, 2, tm, trunk)]
    args = [x, mod]
    blocks = 2 * _nbytes((tm, d), F32)
    for o, w, blk in pairs:
        k = o.shape[1]
        in_specs += [pl.BlockSpec((tm, k), lambda i: (i, 0)),
                     pl.BlockSpec((k, d), lambda i, blk=blk: (blk, 0))]
        args += [o, w]
        blocks += _nbytes((tm, k), BF16) + _nbytes((k, d), BF16)
    return pl.pallas_call(
        functools.partial(_outproj_kernel, len(pairs)),
        grid=(t // tm,),
        in_specs=in_specs,
        out_specs=pl.BlockSpec((tm, d), lambda i: (i, 0)),
        out_shape=jax.ShapeDtypeStruct((t, d), F32),
        compiler_params=_params(("arbitrary",), _vmem_limit(blocks, 0, 2 * _nbytes((tm, d), F32))),
        name=f"outproj_l{layer}_{tag}",
    )(*args)


def _ffn_kernel(x_ref, g_ref, sh_ref, sc_ref, gate_ref, w1_ref, w2_ref, o_ref, h_ref, acc_ref):
    f = pl.program_id(1)

    @pl.when(f == 0)
    def _():
        h_ref[...] = _normmod(x_ref[...], g_ref[...], sc_ref[0, 0], sh_ref[0, 0]).astype(BF16)
        acc_ref[...] = jnp.zeros_like(acc_ref)

    a = jnp.dot(h_ref[...], w1_ref[...], preferred_element_type=F32)
    a = jnp.square(jnp.maximum(a, 0.0)).astype(BF16)
    acc_ref[...] += jnp.dot(a, w2_ref[...], preferred_element_type=F32)

    @pl.when(f == pl.num_programs(1) - 1)
    def _():
        o_ref[...] = x_ref[...] + gate_ref[0, 0] * acc_ref[...]


def _ffn(x, norm_g, mod, layer, w1, w2, trunk, tag):
    t, d = x.shape
    ff = w1.shape[1]
    tm, tf = 512, 1024
    blocks = (2 * _nbytes((tm, d), F32) + _nbytes((d, tf), BF16) + _nbytes((tf, d), BF16)
              + 4 * _nbytes((1, d), F32))
    scratch = _nbytes((tm, d), BF16) + _nbytes((tm, d), F32)
    return pl.pallas_call(
        _ffn_kernel,
        grid=(t // tm, ff // tf),
        in_specs=[pl.BlockSpec((tm, d), lambda i, f: (i, 0)),
                  _const((1, d)),
                  _mod_spec(layer, 3, tm, trunk),
                  _mod_spec(layer, 4, tm, trunk),
                  _mod_spec(layer, 5, tm, trunk),
                  pl.BlockSpec((d, tf), lambda i, f: (0, f)),
                  pl.BlockSpec((tf, d), lambda i, f: (f, 0))],
        out_specs=pl.BlockSpec((tm, d), lambda i, f: (i, 0)),
        out_shape=jax.ShapeDtypeStruct((t, d), F32),
        scratch_shapes=[pltpu.VMEM((tm, d), BF16), pltpu.VMEM((tm, d), F32)],
        compiler_params=_params(("arbitrary", "arbitrary"),
                                _vmem_limit(blocks, scratch,
                                            _nbytes((tm, tf), F32) * 2 + _nbytes((tm, d), F32))),
        name=f"ffn_l{layer}_{tag}",
    )(x, norm_g.reshape(1, d), mod, mod, mod, w1, w2)


def _rope_tables(seq, pattern):
    pos_row = (np.arange(seq) // GRID_W).astype(np.float64)
    pos_col = (np.arange(seq) % GRID_W).astype(np.float64)
    cos_cols, sa_cols, sb_cols = [], [], []
    for width, kind in pattern:
        if kind == "none":
            cos_cols.append(np.ones((seq, width)))
            sa_cols.append(np.zeros((seq, width)))
            sb_cols.append(np.zeros((seq, width)))
            continue
        half = width // 2
        inv = ROPE_BASE ** (-np.arange(half, dtype=np.float64) / half)
        pos = pos_row if kind == "row" else pos_col
        ang = pos[:, None] * inv
        cos, sin = np.cos(ang), np.sin(ang)
        zero = np.zeros_like(sin)
        cos_cols += [cos, cos]
        sa_cols += [-sin, zero]
        sb_cols += [zero, sin]
    tabs = [jnp.asarray(np.concatenate(c, axis=1), F32) for c in (cos_cols, sa_cols, sb_cols)]
    assert tabs[0].shape == (seq, LANES)
    return tabs


def _rope(x, tabs, half):
    cos, sin_a, sin_b = tabs
    return (x * cos + pltpu.roll(x, LANES - half, 1) * sin_a + pltpu.roll(x, half, 1) * sin_b)


def _mla_keys(kv_dot, kr, krg, gkn_ref, km_ref, vm_ref, rows):
    ss_kr = jnp.sum(kr * kr, axis=-1, keepdims=True)
    for pair in range(MLA_HEADS // 2):
        kn2 = kv_dot(pair * MXU_N)
        for s in range(2):
            h = 2 * pair + s
            kn = kn2[:, s * LANES:(s + 1) * LANES]
            r = lax.rsqrt((jnp.sum(kn * kn, axis=-1, keepdims=True) + ss_kr) / MLA_QK + EPS)
            km_ref[rows, h * MLA_SLAB:h * MLA_SLAB + LANES] = (kn * r * gkn_ref[...]).astype(BF16)
            km_ref[rows, h * MLA_SLAB + LANES:(h + 1) * MLA_SLAB] = (krg * r).astype(BF16)
    for c in range(MLA_HEADS * MLA_V // MXU_N):
        v = kv_dot(MLA_HEADS * MLA_NOPE + c * MXU_N)
        vm_ref[rows, c * MXU_N:(c + 1) * MXU_N] = v.astype(BF16)


def _front_ab_kernel(rope, n_chunks, x_ref, g_ref, sh_ref, sc_ref, w_ref, qn_ref, kn_ref, mqn_ref,
                     wq_ref, mkvn_ref, wkv_ref, gq_ref, gkn_ref, gkr_ref, *refs):
    if rope:
        tab_refs, outs, h_ref = refs[:6], refs[6:12], refs[12]
    else:
        tab_refs, outs, h_ref = (), refs[:10], refs[10]
    qda_ref, kda_ref, vda_ref, qm_ref, km_ref, vm_ref = outs[:6]
    rc = x_ref.shape[0] // n_chunks
    lo = lax.broadcasted_iota(jnp.int32, (1, HEAD_W), 1) < DA_QK_DIM
    qn_c = qn_ref[...] * (DA_QK_DIM ** -0.5 * LOG2E)
    gq_c = gq_ref[...] * (MLA_QK ** -0.5 * LOG2E)

    def da_norm(x, g):
        sq = x * x
        s_lo = jnp.sum(jnp.where(lo, sq, 0.0), axis=-1, keepdims=True)
        s_hi = jnp.sum(jnp.where(lo, 0.0, sq), axis=-1, keepdims=True)
        r = jnp.where(lo, lax.rsqrt(s_lo / DA_QK_DIM + EPS), lax.rsqrt(s_hi / DA_QK_DIM + EPS))
        return x * r * g

    for c in range(n_chunks):
        rows = slice(c * rc, (c + 1) * rc)
        h_ref[rows] = _normmod(x_ref[rows], g_ref[...], sc_ref[0, 0], sh_ref[0, 0]).astype(BF16)

        def hdot(c0, width=MXU_N):
            return jnp.dot(h_ref[rows], w_ref[:, c0:c0 + width], preferred_element_type=F32)

        if rope:
            tabs_da = [t[rows] for t in tab_refs[:3]]
            tabs_mla = [t[rows] for t in tab_refs[3:]]

        mq = _rms(hdot(3 * DA_W, MLA_RANK), mqn_ref[...]).astype(BF16)
        for h in range(MLA_HEADS):
            qf = jnp.dot(mq, wq_ref[:, h * MLA_SLAB:(h + 1) * MLA_SLAB], preferred_element_type=F32)
            a, b = qf[:, :LANES], qf[:, LANES:]
            ss = jnp.sum(a * a, axis=-1, keepdims=True) + jnp.sum(b * b, axis=-1, keepdims=True)
            r = lax.rsqrt(ss / MLA_QK + EPS)
            a = a * r * gq_c[:, :LANES]
            b = b * r * gq_c[:, LANES:]
            if rope:
                b = _rope(b, tabs_mla, MLA_ROPE // 4)
            qm_ref[rows, h * MLA_SLAB:h * MLA_SLAB + LANES] = a.astype(BF16)
            qm_ref[rows, h * MLA_SLAB + LANES:(h + 1) * MLA_SLAB] = b.astype(BF16)

        ckv = _rms(hdot(3 * DA_W + MLA_RANK, MLA_RANK), mkvn_ref[...])
        kr = hdot(3 * DA_W + 2 * MLA_RANK, LANES)
        if not rope:
            outs[8][rows] = ckv
            outs[9][rows] = kr[:, :MLA_ROPE]
        ckv_b = ckv.astype(BF16)
        krg = kr * gkr_ref[...]
        if rope:
            krg = _rope(krg, tabs_mla, MLA_ROPE // 4)

        def kv_dot(c0):
            return jnp.dot(ckv_b, wkv_ref[:, c0:c0 + MXU_N], preferred_element_type=F32)

        _mla_keys(kv_dot, kr, krg, gkn_ref, km_ref, vm_ref, rows)

        for pair in range(DA_HEADS // 2):
            pq = hdot(pair * MXU_N)
            pk = hdot(DA_W + pair * MXU_N)
            for s in range(2):
                sl = slice((2 * pair + s) * HEAD_W, (2 * pair + s + 1) * HEAD_W)
                q = da_norm(pq[:, s * HEAD_W:(s + 1) * HEAD_W], qn_c)
                k = da_norm(pk[:, s * HEAD_W:(s + 1) * HEAD_W], kn_ref[...])
                if rope:
                    q = _rope(q, tabs_da, DA_QK_DIM // 4)
                    k = _rope(k, tabs_da, DA_QK_DIM // 4)
                else:
                    outs[6][rows, sl] = k
                qda_ref[rows, sl] = q.astype(BF16)
                kda_ref[rows, sl] = k.astype(BF16)
        for cc in range(DA_W // MXU_N):
            dv = hdot(2 * DA_W + cc * MXU_N)
            vda_ref[rows, cc * MXU_N:(cc + 1) * MXU_N] = dv.astype(BF16)
            if not rope:
                outs[7][rows, cc * MXU_N:(cc + 1) * MXU_N] = dv


def _front_ab(x, norm_g, mod, trunk, w, tabs, tag, casts=()):
    tm, n_chunks = 256, 2
    rows, rope, d = trunk.rows, trunk.rope, D_MODEL
    in_specs = [pl.BlockSpec((tm, d), lambda i: (i, 0)), _const((1, d)),
                _mod_spec(0, 0, tm, trunk), _mod_spec(0, 1, tm, trunk),
                _resident((d, AB_IN_PAD)),
                _const((1, HEAD_W)), _const((1, HEAD_W)), _const((1, MLA_RANK)),
                _resident((MLA_RANK, MLA_HEADS * MLA_SLAB)), _const((1, MLA_RANK)),
                _resident((MLA_RANK, MLA_HEADS * (MLA_NOPE + MLA_V))),
                _const((1, MLA_SLAB)), _const((1, LANES)), _const((1, LANES))]
    args = [x, norm_g.reshape(1, d), mod, mod, w["w_in0"], w["da_qn"], w["da_kn"], w["mq_norm"],
            w["wq"], w["mkv_norm"], w["wkv"], w["gq"], w["gkn"], w["gkr"]]
    if rope:
        per = trunk.seq // tm
        in_specs += [pl.BlockSpec((tm, LANES), lambda i: (i % per, 0))] * 6
        args += list(tabs)
    row = lambda n: pl.BlockSpec((tm, n), lambda i: (i, 0))
    widths = (DA_W, DA_W, DA_W, MLA_HEADS * MLA_SLAB, MLA_HEADS * MLA_SLAB, MLA_HEADS * MLA_V)
    out_specs = [row(n) for n in widths]
    out_shape = [jax.ShapeDtypeStruct((rows, n), BF16) for n in widths]
    out_bytes = sum(_nbytes((tm, n), BF16) for n in widths)
    if not rope:
        cache_w = (DA_W, DA_W, MLA_RANK, MLA_ROPE)
        out_specs += [row(n) for n in cache_w]
        out_shape += [jax.ShapeDtypeStruct((rows, n), F32) for n in cache_w]
        out_bytes += sum(_nbytes((tm, n), F32) for n in cache_w)
    resident = _nbytes((d, AB_IN_PAD), BF16) + 2 * _nbytes((MLA_RANK, 2048), BF16)
    blocks = _nbytes((tm, d), F32) + out_bytes + 6 * _nbytes((tm, LANES), F32)
    body, c_bytes = _add_side_casts(functools.partial(_front_ab_kernel, rope, n_chunks), casts,
                                    rows // tm, lambda i: i, in_specs, args, out_specs, out_shape)
    blocks += c_bytes
    return pl.pallas_call(
        body,
        grid=(rows // tm,),
        in_specs=in_specs,
        out_specs=out_specs,
        out_shape=out_shape,
        scratch_shapes=[pltpu.VMEM((tm, d), BF16)],
        compiler_params=_params(("arbitrary",),
                                _vmem_limit(blocks, resident + _nbytes((tm, d), BF16),
                                            4 * _nbytes((tm, d), F32))),
        name=f"front_ab_{tag}",
    )(*args)


def _ctx_mla_kernel(ckv_ref, kr_ref, wkv_ref, gkn_ref, gkr_ref, km_ref, vm_ref):
    ckv_b = ckv_ref[...].astype(BF16)
    kr = kr_ref[...]

    def kv_dot(c0):
        return jnp.dot(ckv_b, wkv_ref[:, c0:c0 + MXU_N], preferred_element_type=F32)

    _mla_keys(kv_dot, kr, kr * gkr_ref[...], gkn_ref, km_ref, vm_ref, slice(None))


def _ctx_mla(ckv, kr128, w):
    rows = ckv.shape[0]
    tm = 256
    blocks = (_nbytes((tm, MLA_RANK + LANES), F32) + _nbytes((MLA_RANK, 2048), BF16)
              + _nbytes((tm, 3072), BF16))
    return pl.pallas_call(
        _ctx_mla_kernel,
        grid=(rows // tm,),
        in_specs=[pl.BlockSpec((tm, MLA_RANK), lambda i: (i, 0)),
                  pl.BlockSpec((tm, LANES), lambda i: (i, 0)),
                  _const((MLA_RANK, 2048)), _const((1, LANES)), _const((1, LANES))],
        out_specs=[pl.BlockSpec((tm, 2048), lambda i: (i, 0)),
                   pl.BlockSpec((tm, 1024), lambda i: (i, 0))],
        out_shape=[jax.ShapeDtypeStruct((rows, 2048), BF16),
                   jax.ShapeDtypeStruct((rows, 1024), BF16)],
        compiler_params=_params(("arbitrary",), _vmem_limit(blocks, 0, 2 * _nbytes((tm, 2048), F32))),
        name="ctx_mla",
    )(ckv, kr128, w["wkv"], w["gkn"], w["gkr"])


def _front_c_kernel(rope, n_chunks, x_ref, g_ref, sh_ref, sc_ref, w_ref, qn_ref, kn_ref, *refs):
    if rope:
        tab_refs, outs, h_ref = refs[:3], refs[3:6], refs[6]
    else:
        tab_refs, outs, h_ref = (), refs[:5], refs[5]
    q_ref, k_ref, v_ref = outs[:3]
    rc = x_ref.shape[0] // n_chunks
    nq = GQ_HEADS * GQ_DIM
    nk = GQ_KV_HEADS * GQ_DIM
    qn_c = qn_ref[...] * (GQ_DIM ** -0.5 * LOG2E)
    for c in range(n_chunks):
        rows = slice(c * rc, (c + 1) * rc)
        h_ref[rows] = _normmod(x_ref[rows], g_ref[...], sc_ref[0, 0], sh_ref[0, 0]).astype(BF16)

        def hdot(c0):
            return jnp.dot(h_ref[rows], w_ref[:, c0:c0 + MXU_N], preferred_element_type=F32)

        if rope:
            tabs = [t[rows] for t in tab_refs]
        for pair in range(GQ_HEADS // 2):
            pq = hdot(pair * MXU_N)
            for s in range(2):
                sl = slice((2 * pair + s) * GQ_DIM, (2 * pair + s + 1) * GQ_DIM)
                q = _rms(pq[:, s * GQ_DIM:(s + 1) * GQ_DIM], qn_c)
                if rope:
                    q = _rope(q, tabs, GQ_DIM // 4)
                q_ref[rows, sl] = q.astype(BF16)
        for pair in range(GQ_KV_HEADS // 2):
            pk = hdot(nq + pair * MXU_N)
            for s in range(2):
                sl = slice((2 * pair + s) * GQ_DIM, (2 * pair + s + 1) * GQ_DIM)
                k = _rms(pk[:, s * GQ_DIM:(s + 1) * GQ_DIM], kn_ref[...])
                if rope:
                    k = _rope(k, tabs, GQ_DIM // 4)
                else:
                    outs[3][rows, sl] = k
                k_ref[rows, sl] = k.astype(BF16)
        for cc in range(nk // MXU_N):
            v = hdot(nq + nk + cc * MXU_N)
            v_ref[rows, cc * MXU_N:(cc + 1) * MXU_N] = v.astype(BF16)
            if not rope:
                outs[4][rows, cc * MXU_N:(cc + 1) * MXU_N] = v


def _front_c(x, norm_g, mod, trunk, w_in, w, tabs, tag, casts=()):
    tm, n_chunks = 256, 2
    rows, rope, d = trunk.rows, trunk.rope, D_MODEL
    nq = GQ_HEADS * GQ_DIM
    nk = GQ_KV_HEADS * GQ_DIM
    n = nq + 2 * nk
    in_specs = [pl.BlockSpec((tm, d), lambda i: (i, 0)), _const((1, d)),
                _mod_spec(1, 0, tm, trunk), _mod_spec(1, 1, tm, trunk),
                _resident((d, n)), _const((1, GQ_DIM)), _const((1, GQ_DIM))]
    args = [x, norm_g.reshape(1, d), mod, mod, w_in, w["gq_qn"], w["gq_kn"]]
    if rope:
        per = trunk.seq // tm
        in_specs += [pl.BlockSpec((tm, LANES), lambda i: (i % per, 0))] * 3
        args += list(tabs)
    row = lambda wd: pl.BlockSpec((tm, wd), lambda i: (i, 0))
    out_specs = [row(nq), row(nk), row(nk)]
    out_shape = [jax.ShapeDtypeStruct((rows, wd), BF16) for wd in (nq, nk, nk)]
    out_bytes = _nbytes((tm, n), BF16)
    if not rope:
        out_specs += [row(nk), row(nk)]
        out_shape += [jax.ShapeDtypeStruct((rows, nk), F32)] * 2
        out_bytes += 2 * _nbytes((tm, nk), F32)
    blocks = _nbytes((tm, d), F32) + out_bytes + 3 * _nbytes((tm, LANES), F32)
    body, c_bytes = _add_side_casts(functools.partial(_front_c_kernel, rope, n_chunks), casts,
                                    rows // tm, lambda i: i, in_specs, args, out_specs, out_shape)
    blocks += c_bytes
    return pl.pallas_call(
        body,
        grid=(rows // tm,),
        in_specs=in_specs,
        out_specs=out_specs,
        out_shape=out_shape,
        scratch_shapes=[pltpu.VMEM((tm, d), BF16)],
        compiler_params=_params(("arbitrary",),
                                _vmem_limit(blocks, _nbytes((d, n), BF16) + _nbytes((tm, d), BF16),
                                            4 * _nbytes((tm, d), F32))),
        name=f"front_c_{tag}",
    )(*args)


def _dot_nt(a, b):
    return lax.dot_general(a, b, (((1,), (1,)), ((), ())), preferred_element_type=F32)


def _attn_rows(q, parts):
    scores = [_dot_nt(q, k) for k, _ in parts]
    m = functools.reduce(jnp.maximum, [jnp.max(s, axis=-1, keepdims=True) for s in scores])
    o1 = None
    for s, (_, v) in zip(scores, parts):
        e = jnp.exp2(s - m).astype(BF16)
        v1 = jnp.concatenate([v, jnp.ones((v.shape[0], LANES), BF16)], axis=1)
        part = jnp.dot(e, v1, preferred_element_type=F32)
        o1 = part if o1 is None else o1 + part
    dv = parts[0][1].shape[1]
    return o1[:, :dv], o1[:, dv:]


ONES_ROWS = 16


def _attn_cols(q, parts, mask=None, sink=None):
    scores = [_dot_nt(k, q) for k, _ in parts]
    if mask is not None:
        scores[-1] = jnp.where(mask, scores[-1], NEG_INF)
    m = functools.reduce(jnp.maximum, [jnp.max(s, axis=0, keepdims=True) for s in scores])
    if sink is not None:
        sink2 = sink * LOG2E
        m = jnp.maximum(m, sink2)
    o1 = None
    for s, (_, v) in zip(scores, parts):
        e = jnp.exp2(s - m).astype(BF16)
        v_t1 = jnp.concatenate([v.astype(F32).T.astype(BF16),
                                jnp.ones((ONES_ROWS, v.shape[0]), BF16)], axis=0)
        part = jnp.dot(v_t1, e, preferred_element_type=F32)
        o1 = part if o1 is None else o1 + part
    dv = parts[0][1].shape[1]
    den = o1[dv:dv + 1]
    if sink is not None:
        den = den + jnp.exp2(sink2 - m)
    return o1[:dv], den


def _kv_parts(refs, n_parts, sl_k, sl_v):
    return [(refs[2 * p][:, sl_k].astype(BF16), refs[2 * p + 1][:, sl_v].astype(BF16))
            for p in range(n_parts)]


def _da_attn_kernel(n_parts, lam_init, lam_ref, gsub_ref, q_ref, *refs):
    o_ref = refs[-1]
    tq = q_ref.shape[0]
    lv = lam_ref[...]
    lam = (jnp.exp(jnp.sum(lv[0:1] * lv[1:2], axis=-1, keepdims=True))
           - jnp.exp(jnp.sum(lv[2:3] * lv[3:4], axis=-1, keepdims=True)) + lam_init)
    lo = lax.broadcasted_iota(jnp.int32, (1, HEAD_W), 1) < DA_QK_DIM
    for h in range(DA_HEADS):
        sl = slice(h * HEAD_W, (h + 1) * HEAD_W)
        q = q_ref[:, sl]
        zero = jnp.zeros_like(q)
        q12 = jnp.concatenate([jnp.where(lo, q, zero), jnp.where(lo, zero, q)], axis=0)
        o12, d12 = _attn_rows(q12, _kv_parts(refs, n_parts, sl, sl))
        o12 = o12 * (1.0 / d12)
        o = o12[:tq] - lam * o12[tq:]
        y = o * lax.rsqrt(jnp.mean(o * o, axis=-1, keepdims=True) + EPS) * gsub_ref[...]
        o_ref[:, sl] = (y * (1.0 - lam_init)).astype(BF16)


def _mla_attn_kernel(n_parts, q_ref, *refs):
    o_ref = refs[-1]
    for h in range(MLA_HEADS):
        sl_k = slice(h * MLA_SLAB, (h + 1) * MLA_SLAB)
        sl_v = slice(h * MLA_V, (h + 1) * MLA_V)
        o, den = _attn_rows(q_ref[:, sl_k], _kv_parts(refs, n_parts, sl_k, sl_v))
        o_ref[:, sl_v] = (o * (1.0 / den)).astype(BF16)


def _gq_attn_kernel(n_parts, seq, kw, sink_ref, q_ref, *refs):
    o_ref = refs[-1]
    tq = q_ref.shape[0]
    qi = pl.program_id(1)
    mask = None
    if n_parts == 2:
        start = pl.multiple_of(jnp.clip(qi * tq - WINDOW, 0, seq - kw), WINDOW)
        keys = start + lax.broadcasted_iota(jnp.int32, (kw, GQ_GROUP * tq), 0)
        qrows = qi * tq + (lax.broadcasted_iota(jnp.int32, (kw, GQ_GROUP * tq), 1) & (tq - 1))
        mask = jnp.abs(qrows - keys) <= WINDOW
    for g in range(GQ_KV_HEADS):
        sl = slice(g * GQ_DIM, (g + 1) * GQ_DIM)
        heads = range(g * GQ_GROUP, (g + 1) * GQ_GROUP)
        q4 = jnp.concatenate([q_ref[:, j * GQ_DIM:(j + 1) * GQ_DIM] for j in heads], axis=0)
        sink = jnp.concatenate([jnp.broadcast_to(sink_ref[j:j + 1, 0:1], (1, tq)) for j in heads], axis=1)
        if n_parts == 2:
            parts = [(refs[0][:, sl].astype(BF16), refs[1][:, sl]),
                     (refs[2][pl.ds(start, kw), sl], refs[3][pl.ds(start, kw), sl])]
        else:
            parts = [(refs[0][:, sl], refs[1][:, sl])]
        o, den = _attn_cols(q4, parts, mask=mask, sink=sink)
        o = o * (1.0 / den)
        for n, j in enumerate(heads):
            o_ref[:, j * GQ_DIM:(j + 1) * GQ_DIM] = o[:, n * tq:(n + 1) * tq].T.astype(BF16)


def _attention(kernel, name, q, kv_parts, trunk, tq, out_w, stacked_rows, extra_in=(), casts=()):
    nq = trunk.seq // tq
    in_specs = [_const(a.shape) for a in extra_in]
    args = list(extra_in)
    qw = q.shape[1]
    in_specs.append(pl.BlockSpec((tq, qw), lambda b, i: (b * nq + i, 0)))
    args.append(q)
    blocks = _nbytes((tq, qw), BF16) + _nbytes((tq, out_w), BF16)
    total_l = 0
    for k, v, l in kv_parts:
        in_specs += [pl.BlockSpec((l, k.shape[1]), lambda b, i: (b, 0)),
                     pl.BlockSpec((l, v.shape[1]), lambda b, i: (b, 0))]
        args += [k, v]
        blocks += _nbytes((l, k.shape[1]), k.dtype) + _nbytes((l, v.shape[1]), v.dtype)
        total_l += l
    out_specs = [pl.BlockSpec((tq, out_w), lambda b, i: (b * nq + i, 0))]
    out_shape = [jax.ShapeDtypeStruct((trunk.rows, out_w), BF16)]
    kernel, c_bytes = _add_side_casts(kernel, casts, trunk.batch * nq, lambda b, i: b * nq + i,
                                      in_specs, args, out_specs, out_shape)
    blocks += c_bytes
    out = pl.pallas_call(
        kernel,
        grid=(trunk.batch, nq),
        in_specs=in_specs,
        out_specs=out_specs,
        out_shape=out_shape,
        compiler_params=_params(("arbitrary",) * 2,
                                _vmem_limit(blocks, 0, 6 * _nbytes((stacked_rows, total_l), F32))),
        name=name,
    )(*args)
    return out if casts else out[0]


CAST_PLAN = {
    "front_ab": {"ff1_0": ("ff1_f32", 0), "w_out0": ("w_out0_f32", 0)},
    "da_attn": {"ff2_0": ("ff2_f32", 0)},
    "mla_attn": {"w_in1": ("w_in1_f32", 0)},
    "front_c": {"ff2_1": ("ff2_f32", 1), "w_out1": ("w_out1_f32", 0)},
    "gq_attn": {"ff1_1": ("ff1_f32", 1)},
}


def _run_trunk(x, trunk, tag, mod, P, ctx, wb):
    tq = min(trunk.seq, 256)
    casting = wb is None
    wb = dict(wb or {})

    def jobs(call):
        return [(P[src], layer) for src, layer in CAST_PLAN[call].values()] if casting else []

    def split(call, outs, n_main):
        if not casting:
            return outs
        for name, w in zip(CAST_PLAN[call], outs[n_main:]):
            wb[name] = w
        return outs[:n_main] if n_main > 1 else outs[0]

    n_front = 6 if trunk.rope else 10
    front = split("front_ab", _front_ab(x, P["norm1_g"][0], mod, trunk, P,
                                        P["tabs_ab"] if trunk.rope else None, tag,
                                        casts=jobs("front_ab")), n_front)
    qda, kda, vda, qm, km, vm = front[:6]
    da_parts, mla_parts = [(kda, vda, trunk.seq)], [(km, vm, trunk.seq)]
    if ctx is not None:
        da_parts = [(ctx["da_k"], ctx["da_v"], ctx["past"])] + da_parts
        mla_parts = [(ctx["mla_k"], ctx["mla_v"], ctx["past"])] + mla_parts
    o_da = split("da_attn", _attention(functools.partial(_da_attn_kernel, len(da_parts), P["lam_init"]),
                                       f"da_attn_{tag}", qda, da_parts, trunk, tq, DA_HEADS * HEAD_W,
                                       2 * tq, extra_in=(P["lam4"], P["gsub"]), casts=jobs("da_attn")), 1)
    o_m = split("mla_attn", _attention(functools.partial(_mla_attn_kernel, len(mla_parts)),
                                       f"mla_attn_{tag}", qm, mla_parts, trunk, tq, MLA_HEADS * MLA_V,
                                       tq, casts=jobs("mla_attn")), 1)
    x = _outproj(x, mod, 0, [(o_da, wb["w_out0"], 0), (o_m, wb["w_out0"], 1)], trunk, tag)
    x = _ffn(x, P["norm2_g"][0], mod, 0, wb["ff1_0"], wb["ff2_0"], trunk, tag)

    n_front_c = 3 if trunk.rope else 5
    front_c = split("front_c", _front_c(x, P["norm1_g"][1], mod, trunk, wb["w_in1"], P,
                                        P["tabs_gq"] if trunk.rope else None, tag,
                                        casts=jobs("front_c")), n_front_c)
    qc, kc, vc = front_c[:3]
    gq_parts = [(kc, vc, trunk.seq)]
    if ctx is not None:
        gq_parts = [(ctx["gq_k"], ctx["gq_v"], ctx["past"])] + gq_parts
    kw = min(trunk.seq, tq + 2 * WINDOW)
    o_c = split("gq_attn", _attention(functools.partial(_gq_attn_kernel, len(gq_parts), trunk.seq, kw),
                                      f"gq_attn_{tag}", qc, gq_parts, trunk, tq, GQ_HEADS * GQ_DIM,
                                      GQ_GROUP * tq, extra_in=(P["sink"],), casts=jobs("gq_attn")), 1)
    x = _outproj(x, mod, 1, [(o_c, wb["w_out1"], 0)], trunk, tag)
    x = _ffn(x, P["norm2_g"][1], mod, 1, wb["ff1_1"], wb["ff2_1"], trunk, tag)
    return x, front[6:], front_c[3:], wb


def kernel(x_prompt, x_sample, cache_da_k, cache_da_v, cache_mla_ckv, cache_mla_krope, cache_gq_k, cache_gq_v, c, c_ctx, norm1_g, norm2_g, ada_w, ada_b, ff1_w, ff2_w, ab_w_in, ab_w_out, da_lambda_q1, da_lambda_k1, da_lambda_q2, da_lambda_k2, da_q_norm, da_k_norm, da_subln, mla_q_a_norm, mla_w_q_up, mla_kv_a_norm, mla_w_kv_up, mla_q_norm, mla_k_norm, c_w_in, c_w_out, gq_q_norm, gq_k_norm, gq_sink):
    pb, ps, d = x_prompt.shape
    sb, ss, _ = x_sample.shape
    past = cache_da_k.shape[2]
    assert sb + 1 <= 8 and d == D_MODEL

    cond8 = jnp.concatenate([c_ctx[None], c, jnp.zeros((8 - 1 - sb, d), F32)], axis=0)
    wq = jnp.pad(mla_w_q_up[0].reshape(MLA_RANK, MLA_HEADS, MLA_QK),
                 ((0, 0), (0, 0), (0, MLA_SLAB - MLA_QK))).reshape(MLA_RANK, MLA_HEADS * MLA_SLAB)
    wkv3 = mla_w_kv_up[0].reshape(MLA_RANK, MLA_HEADS, MLA_NOPE + MLA_V)
    wkv = jnp.concatenate([wkv3[..., :MLA_NOPE].reshape(MLA_RANK, -1),
                           wkv3[..., MLA_NOPE:].reshape(MLA_RANK, -1)], axis=1)
    P = {
        "norm1_g": norm1_g, "norm2_g": norm2_g,
        "w_in0": jnp.pad(ab_w_in[0], ((0, 0), (0, AB_IN_PAD - AB_IN))).astype(BF16),
        "ff1_f32": ff1_w, "ff2_f32": ff2_w,
        "w_out0_f32": ab_w_out, "w_in1_f32": c_w_in, "w_out1_f32": c_w_out,
        "da_qn": jnp.tile(da_q_norm[0], 2).reshape(1, HEAD_W),
        "da_kn": jnp.tile(da_k_norm[0], 2).reshape(1, HEAD_W),
        "mq_norm": mla_q_a_norm[0].reshape(1, MLA_RANK),
        "mkv_norm": mla_kv_a_norm[0].reshape(1, MLA_RANK),
        "wq": wq.astype(BF16), "wkv": wkv.astype(BF16),
        "gq": jnp.pad(mla_q_norm[0], (0, MLA_SLAB - MLA_QK)).reshape(1, MLA_SLAB),
        "gkn": mla_k_norm[0, :MLA_NOPE].reshape(1, LANES),
        "gkr": jnp.pad(mla_k_norm[0, MLA_NOPE:], (0, LANES - MLA_ROPE)).reshape(1, LANES),
        "lam4": jnp.stack([da_lambda_q1[0], da_lambda_k1[0], da_lambda_q2[0], da_lambda_k2[0]]),
        "gsub": da_subln[0].reshape(1, HEAD_W),
        "lam_init": 0.8 - 0.6 * math.exp(-0.3 * 0),
        "gq_qn": gq_q_norm[0].reshape(1, GQ_DIM), "gq_kn": gq_k_norm[0].reshape(1, GQ_DIM),
        "sink": jnp.broadcast_to(gq_sink[0].reshape(GQ_HEADS, 1), (GQ_HEADS, LANES)),
        "tabs_ab": (_rope_tables(ss, [(32, "row"), (32, "col"), (32, "row"), (32, "col")])
                    + _rope_tables(ss, [(32, "row"), (32, "col"), (32, "none"), (32, "none")])),
        "tabs_gq": _rope_tables(ss, [(64, "row"), (64, "col")]),
    }

    mod = _ada_mod(cond8, ada_w, ada_b).reshape(2, 8, 1, 6 * d)

    kr_ctx = jnp.pad(cache_mla_krope[:, 0].reshape(sb * past, MLA_ROPE), ((0, 0), (0, LANES - MLA_ROPE)))
    mla_k_ctx, mla_v_ctx = _ctx_mla(cache_mla_ckv[:, 0].reshape(sb * past, MLA_RANK), kr_ctx, P)
    ctx = {
        "past": past,
        "da_k": cache_da_k[:, 0].reshape(sb * past, -1), "da_v": cache_da_v[:, 0].reshape(sb * past, -1),
        "mla_k": mla_k_ctx, "mla_v": mla_v_ctx,
        "gq_k": cache_gq_k[:, 0].reshape(sb * past, -1), "gq_v": cache_gq_v[:, 0].reshape(sb * past, -1),
    }

    prompt = Trunk(groups=1, seq=ps, batch=pb, mod_row0=0, rope=False)
    sample = Trunk(groups=sb, seq=ss, batch=sb, mod_row0=1, rope=True)
    y_p, (new_da_k, new_da_v, new_ckv, new_kr), (new_gq_k, new_gq_v), ffw = _run_trunk(
        x_prompt.reshape(pb * ps, d), prompt, "prompt", mod, P, None, None)
    y_s, _, _, _ = _run_trunk(x_sample.reshape(sb * ss, d), sample, "sample", mod, P, ctx, ffw)

    return (y_p.reshape(pb, ps, d), y_s.reshape(sb, ss, d),
            new_da_k.reshape(pb, 1, ps, DA_HEADS, HEAD_W), new_da_v.reshape(pb, 1, ps, DA_HEADS, HEAD_W),
            new_ckv.reshape(pb, 1, ps, MLA_RANK), new_kr.reshape(pb, 1, ps, MLA_ROPE),
            new_gq_k.reshape(pb, 1, ps, GQ_KV_HEADS, GQ_DIM), new_gq_v.reshape(pb, 1, ps, GQ_KV_HEADS, GQ_DIM))
```

```python
import functools
import math
from typing import NamedTuple

import jax
import jax.numpy as jnp
import numpy as np
from jax import lax
from jax.experimental import pallas as pl
from jax.experimental.pallas import tpu as pltpu

F32 = jnp.float32
BF16 = jnp.bfloat16

D_MODEL = 2048
GRID_W = 64
ROPE_BASE = 10000.0
EPS = 1e-6
NEG_INF = -1e30
LOG2E = math.log2(math.e)
DA_HEADS = 8
DA_QK_DIM = 64
DA_W = DA_HEADS * 2 * DA_QK_DIM
MLA_HEADS = 8
MLA_RANK = 512
MLA_NOPE = 128
MLA_ROPE = 64
MLA_V = 128
MLA_QK = MLA_NOPE + MLA_ROPE
MLA_SLAB = 256
GQ_HEADS = 16
GQ_KV_HEADS = 4
GQ_GROUP = GQ_HEADS // GQ_KV_HEADS
GQ_DIM = 128
WINDOW = 128
HEAD_W = 128
AB_IN = 4160
AB_IN_PAD = 4224

LANES = 128
MXU_N = 256
VMEM_CAP_BYTES = 60 * 1024 * 1024


class Trunk(NamedTuple):
    groups: int
    seq: int
    batch: int
    mod_row0: int
    rope: bool

    @property
    def rows(self):
        return self.batch * self.seq


def _vmem_limit(block_bytes, scratch_bytes=0, temp_bytes=0):
    del block_bytes, scratch_bytes, temp_bytes
    return VMEM_CAP_BYTES


def _nbytes(shape, dtype):
    return math.prod(shape) * jnp.dtype(dtype).itemsize


def _params(sem, vmem):
    return pltpu.CompilerParams(dimension_semantics=sem, vmem_limit_bytes=vmem)


def _resident(shape):
    return pl.BlockSpec(shape, lambda *_: (0,) * len(shape), pipeline_mode=pl.Buffered(1))


def _const(shape):
    return pl.BlockSpec(shape, lambda *_: (0,) * len(shape))


def _add_side_casts(kernel, casts, steps, step_of, in_specs, args, out_specs, out_shape):
    n_in, n_out, n_jobs = len(args), len(out_specs), len(casts)
    extra = 0
    for w, layer in casts:
        _, rows, cols = w.shape
        blk = rows // steps
        in_specs.append(pl.BlockSpec((None, blk, cols), lambda *ids, l=layer: (l, step_of(*ids), 0)))
        args.append(w)
        out_specs.append(pl.BlockSpec((blk, cols), lambda *ids: (step_of(*ids), 0)))
        out_shape.append(jax.ShapeDtypeStruct((rows, cols), BF16))
        extra += _nbytes((blk, cols), F32) + _nbytes((blk, cols), BF16)

    def wrapped(*refs):
        srcs = refs[n_in:n_in + n_jobs]
        dsts = refs[n_in + n_jobs + n_out:n_in + 2 * n_jobs + n_out]
        for src, dst in zip(srcs, dsts):
            dst[...] = src[...].astype(BF16)
        kernel(*refs[:n_in], *refs[n_in + n_jobs:n_in + n_jobs + n_out], *refs[n_in + 2 * n_jobs + n_out:])

    return (wrapped if casts else kernel), extra


def _ada_kernel(c_ref, w_ref, b_ref, o_ref):
    c = c_ref[...]
    s = (c / (1.0 + jnp.exp(-c))).astype(BF16)
    o_ref[0] = jnp.dot(s, w_ref[0].astype(BF16), preferred_element_type=F32) + b_ref[0]


def _ada_mod(cond8, ada_w, ada_b):
    depth, d, n = ada_w.shape
    tn = 1024
    blocks = _nbytes((d, tn), F32) + _nbytes((8, d), F32) + _nbytes((8, tn), F32)
    return pl.pallas_call(
        _ada_kernel,
        grid=(depth, n // tn),
        in_specs=[pl.BlockSpec((8, d), lambda l, j: (0, 0)),
                  pl.BlockSpec((1, d, tn), lambda l, j: (l, 0, j)),
                  pl.BlockSpec((1, 1, tn), lambda l, j: (l, 0, j))],
        out_specs=pl.BlockSpec((1, 8, tn), lambda l, j: (l, 0, j)),
        out_shape=jax.ShapeDtypeStruct((depth, 8, n), F32),
        compiler_params=_params(("arbitrary", "arbitrary"),
                                _vmem_limit(blocks, temp_bytes=_nbytes((d, tn), BF16))),
        name="ada_mod",
    )(cond8, ada_w, ada_b.reshape(depth, 1, n))


def _mod_spec(layer, chunk, tm, trunk):
    per = trunk.rows // trunk.groups // tm

    def idx(i, *_):
        return (layer, trunk.mod_row0 + i // per, 0, chunk)

    return pl.BlockSpec((1, 1, 1, D_MODEL), idx)


def _normmod(x, g, sc, sh):
    ms = jnp.mean(x * x, axis=-1, keepdims=True)
    y = x * lax.rsqrt(ms + EPS) * g
    return y * (1.0 + sc) + sh


def _rms(x, g):
    return x * lax.rsqrt(jnp.mean(x * x, axis=-1, keepdims=True) + EPS) * g


def _outproj_kernel(n_in, x_ref, g_ref, *refs):
    o_ref = refs[-1]
    acc = None
    for k in range(n_in):
        part = jnp.dot(refs[2 * k][...], refs[2 * k + 1][...], preferred_element_type=F32)
        acc = part if acc is None else acc + part
    o_ref[...] = x_ref[...] + g_ref[0, 0] * acc


def _outproj(x, mod, layer, pairs, trunk, tag):
    t, d = x.shape
    tm = 512
    in_specs = [pl.BlockSpec((tm, d), lambda i: (i, 0)),
                _mod_spec(layer, 2, tm, trunk)]
    args = [x, mod]
    blocks = 2 * _nbytes((tm, d), F32)
    for o, w, blk in pairs:
        k = o.shape[1]
        in_specs += [pl.BlockSpec((tm, k), lambda i: (i, 0)),
                     pl.BlockSpec((k, d), lambda i, blk=blk: (blk, 0))]
        args += [o, w]
        blocks += _nbytes((tm, k), BF16) + _nbytes((k, d), BF16)
    return pl.pallas_call(
        functools.partial(_outproj_kernel, len(pairs)),
        grid=(t // tm,),
        in_specs=in_specs,
        out_specs=pl.BlockSpec((tm, d), lambda i: (i, 0)),
        out_shape=jax.ShapeDtypeStruct((t, d), F32),
        compiler_params=_params(("arbitrary",), _vmem_limit(blocks, 0, 2 * _nbytes((tm, d), F32))),
        name=f"outproj_l{layer}_{tag}",
    )(*args)


def _ffn_kernel(x_ref, g_ref, sh_ref, sc_ref, gate_ref, w1_ref, w2_ref, o_ref, h_ref, acc_ref):
    f = pl.program_id(1)

    @pl.when(f == 0)
    def _():
        h_ref[...] = _normmod(x_ref[...], g_ref[...], sc_ref[0, 0], sh_ref[0, 0]).astype(BF16)
        acc_ref[...] = jnp.zeros_like(acc_ref)

    a = jnp.dot(h_ref[...], w1_ref[...], preferred_element_type=F32)
    a = jnp.square(jnp.maximum(a, 0.0)).astype(BF16)
    acc_ref[...] += jnp.dot(a, w2_ref[...], preferred_element_type=F32)

    @pl.when(f == pl.num_programs(1) - 1)
    def _():
        o_ref[...] = x_ref[...] + gate_ref[0, 0] * acc_ref[...]


def _ffn(x, norm_g, mod, layer, w1, w2, trunk, tag):
    t, d = x.shape
    ff = w1.shape[1]
    tm, tf = 512, 1024
    blocks = (2 * _nbytes((tm, d), F32) + _nbytes((d, tf), BF16) + _nbytes((tf, d), BF16)
              + 4 * _nbytes((1, d), F32))
    scratch = _nbytes((tm, d), BF16) + _nbytes((tm, d), F32)
    return pl.pallas_call(
        _ffn_kernel,
        grid=(t // tm, ff // tf),
        in_specs=[pl.BlockSpec((tm, d), lambda i, f: (i, 0)),
                  _const((1, d)),
                  _mod_spec(layer, 3, tm, trunk),
                  _mod_spec(layer, 4, tm, trunk),
                  _mod_spec(layer, 5, tm, trunk),
                  pl.BlockSpec((d, tf), lambda i, f: (0, f)),
                  pl.BlockSpec((tf, d), lambda i, f: (f, 0))],
        out_specs=pl.BlockSpec((tm, d), lambda i, f: (i, 0)),
        out_shape=jax.ShapeDtypeStruct((t, d), F32),
        scratch_shapes=[pltpu.VMEM((tm, d), BF16), pltpu.VMEM((tm, d), F32)],
        compiler_params=_params(("arbitrary", "arbitrary"),
                                _vmem_limit(blocks, scratch,
                                            _nbytes((tm, tf), F32) * 2 + _nbytes((tm, d), F32))),
        name=f"ffn_l{layer}_{tag}",
    )(x, norm_g.reshape(1, d), mod, mod, mod, w1, w2)


def _rope_tables(seq, pattern):
    pos_row = (np.arange(seq) // GRID_W).astype(np.float64)
    pos_col = (np.arange(seq) % GRID_W).astype(np.float64)
    cos_cols, sa_cols, sb_cols = [], [], []
    for width, kind in pattern:
        if kind == "none":
            cos_cols.append(np.ones((seq, width)))
            sa_cols.append(np.zeros((seq, width)))
            sb_cols.append(np.zeros((seq, width)))
            continue
        half = width // 2
        inv = ROPE_BASE ** (-np.arange(half, dtype=np.float64) / half)
        pos = pos_row if kind == "row" else pos_col
        ang = pos[:, None] * inv
        cos, sin = np.cos(ang), np.sin(ang)
        zero = np.zeros_like(sin)
        cos_cols += [cos, cos]
        sa_cols += [-sin, zero]
        sb_cols += [zero, sin]
    tabs = [jnp.asarray(np.concatenate(c, axis=1), F32) for c in (cos_cols, sa_cols, sb_cols)]
    assert tabs[0].shape == (seq, LANES)
    return tabs


def _rope(x, tabs, half):
    cos, sin_a, sin_b = tabs
    return (x * cos + pltpu.roll(x, LANES - half, 1) * sin_a + pltpu.roll(x, half, 1) * sin_b)


def _mla_keys(kv_dot, kr, krg, gkn_ref, km_ref, vm_ref, rows):
    ss_kr = jnp.sum(kr * kr, axis=-1, keepdims=True)
    for pair in range(MLA_HEADS // 2):
        kn2 = kv_dot(pair * MXU_N)
        for s in range(2):
            h = 2 * pair + s
            kn = kn2[:, s * LANES:(s + 1) * LANES]
            r = lax.rsqrt((jnp.sum(kn * kn, axis=-1, keepdims=True) + ss_kr) / MLA_QK + EPS)
            km_ref[rows, h * MLA_SLAB:h * MLA_SLAB + LANES] = (kn * r * gkn_ref[...]).astype(BF16)
            km_ref[rows, h * MLA_SLAB + LANES:(h + 1) * MLA_SLAB] = (krg * r).astype(BF16)
    for c in range(MLA_HEADS * MLA_V // MXU_N):
        v = kv_dot(MLA_HEADS * MLA_NOPE + c * MXU_N)
        vm_ref[rows, c * MXU_N:(c + 1) * MXU_N] = v.astype(BF16)


def _norm_ahead(x0_ref, xn_ref, g_ref, sh0_ref, sc0_ref, shn_ref, scn_ref, h_ref, hn_ref):
    @pl.when(pl.program_id(0) == 0)
    def _():
        h_ref[...] = _normmod(x0_ref[...], g_ref[...], sc0_ref[0, 0], sh0_ref[0, 0]).astype(BF16)

    hn_ref[...] = _normmod(xn_ref[...], g_ref[...], scn_ref[0, 0], shn_ref[0, 0]).astype(BF16)


def _norm_ahead_specs(layer, tm, trunk, d):
    n_tiles = trunk.rows // tm
    per = trunk.rows // trunk.groups // tm
    nxt = lambda i: jnp.minimum(i + 1, n_tiles - 1)

    def mod(chunk, tile_of):
        return pl.BlockSpec((1, 1, 1, d), lambda i: (layer, trunk.mod_row0 + tile_of(i) // per, 0, chunk))

    return [pl.BlockSpec((tm, d), lambda i: (0, 0), pipeline_mode=pl.Buffered(1)),
            pl.BlockSpec((tm, d), lambda i: (nxt(i), 0)), _const((1, d)),
            mod(0, lambda i: 0), mod(1, lambda i: 0), mod(0, nxt), mod(1, nxt)]


def _front_ab_kernel(rope, n_chunks, x0_ref, xn_ref, g_ref, sh0_ref, sc0_ref, shn_ref, scn_ref, w_ref,
                     qn_ref, kn_ref, mqn_ref, wq_ref, mkvn_ref, wkv_ref, gq_ref, gkn_ref, gkr_ref, *refs):
    if rope:
        tab_refs, outs, (h_ref, hn_ref) = refs[:6], refs[6:12], refs[12:]
    else:
        tab_refs, outs, (h_ref, hn_ref) = (), refs[:10], refs[10:]
    _norm_ahead(x0_ref, xn_ref, g_ref, sh0_ref, sc0_ref, shn_ref, scn_ref, h_ref, hn_ref)
    qda_ref, kda_ref, vda_ref, qm_ref, km_ref, vm_ref = outs[:6]
    rc = h_ref.shape[0] // n_chunks
    lo = lax.broadcasted_iota(jnp.int32, (1, HEAD_W), 1) < DA_QK_DIM
    qn_c = qn_ref[...] * (DA_QK_DIM ** -0.5 * LOG2E)
    gq_c = gq_ref[...] * (MLA_QK ** -0.5 * LOG2E)

    def da_norm(x, g):
        sq = x * x
        s_lo = jnp.sum(jnp.where(lo, sq, 0.0), axis=-1, keepdims=True)
        s_hi = jnp.sum(jnp.where(lo, 0.0, sq), axis=-1, keepdims=True)
        r = jnp.where(lo, lax.rsqrt(s_lo / DA_QK_DIM + EPS), lax.rsqrt(s_hi / DA_QK_DIM + EPS))
        return x * r * g

    for c in range(n_chunks):
        rows = slice(c * rc, (c + 1) * rc)

        def hdot(c0, width=MXU_N):
            return jnp.dot(h_ref[rows], w_ref[:, c0:c0 + width], preferred_element_type=F32)

        if rope:
            tabs_da = [t[rows] for t in tab_refs[:3]]
            tabs_mla = [t[rows] for t in tab_refs[3:]]

        def da_pairs(pairs):
            for pair in pairs:
                pq = hdot(pair * MXU_N)
                pk = hdot(DA_W + pair * MXU_N)
                for s in range(2):
                    sl = slice((2 * pair + s) * HEAD_W, (2 * pair + s + 1) * HEAD_W)
                    q = da_norm(pq[:, s * HEAD_W:(s + 1) * HEAD_W], qn_c)
                    k = da_norm(pk[:, s * HEAD_W:(s + 1) * HEAD_W], kn_ref[...])
                    if rope:
                        q = _rope(q, tabs_da, DA_QK_DIM // 4)
                        k = _rope(k, tabs_da, DA_QK_DIM // 4)
                    else:
                        outs[6][rows, sl] = k
                    qda_ref[rows, sl] = q.astype(BF16)
                    kda_ref[rows, sl] = k.astype(BF16)

        def mla_q_heads(mq, heads):
            for h in heads:
                qf = jnp.dot(mq, wq_ref[:, h * MLA_SLAB:(h + 1) * MLA_SLAB], preferred_element_type=F32)
                a, b = qf[:, :LANES], qf[:, LANES:]
                ss = jnp.sum(a * a, axis=-1, keepdims=True) + jnp.sum(b * b, axis=-1, keepdims=True)
                r = lax.rsqrt(ss / MLA_QK + EPS)
                a = a * r * gq_c[:, :LANES]
                b = b * r * gq_c[:, LANES:]
                if rope:
                    b = _rope(b, tabs_mla, MLA_ROPE // 4)
                qm_ref[rows, h * MLA_SLAB:h * MLA_SLAB + LANES] = a.astype(BF16)
                qm_ref[rows, h * MLA_SLAB + LANES:(h + 1) * MLA_SLAB] = b.astype(BF16)

        mq = _rms(hdot(3 * DA_W, MLA_RANK), mqn_ref[...]).astype(BF16)
        ckv = _rms(hdot(3 * DA_W + MLA_RANK, MLA_RANK), mkvn_ref[...])
        kr = hdot(3 * DA_W + 2 * MLA_RANK, LANES)
        if not rope:
            outs[8][rows] = ckv
            outs[9][rows] = kr[:, :MLA_ROPE]
        ckv_b = ckv.astype(BF16)
        da_pairs(range(0, DA_HEADS // 4))
        mla_q_heads(mq, range(0, MLA_HEADS // 2))
        da_pairs(range(DA_HEADS // 4, DA_HEADS // 2))
        mla_q_heads(mq, range(MLA_HEADS // 2, MLA_HEADS))
        for cc in range(DA_W // MXU_N):
            dv = hdot(2 * DA_W + cc * MXU_N)
            vda_ref[rows, cc * MXU_N:(cc + 1) * MXU_N] = dv.astype(BF16)
            if not rope:
                outs[7][rows, cc * MXU_N:(cc + 1) * MXU_N] = dv
        krg = kr * gkr_ref[...]
        if rope:
            krg = _rope(krg, tabs_mla, MLA_ROPE // 4)

        def kv_dot(c0):
            return jnp.dot(ckv_b, wkv_ref[:, c0:c0 + MXU_N], preferred_element_type=F32)

        _mla_keys(kv_dot, kr, krg, gkn_ref, km_ref, vm_ref, rows)

    h_ref[...] = hn_ref[...]


def _front_ab(x, norm_g, mod, trunk, w, tabs, tag, casts=()):
    tm, n_chunks = 256, 1
    rows, rope, d = trunk.rows, trunk.rope, D_MODEL
    in_specs = _norm_ahead_specs(0, tm, trunk, d) + [
                _resident((d, AB_IN_PAD)),
                _const((1, HEAD_W)), _const((1, HEAD_W)), _const((1, MLA_RANK)),
                _resident((MLA_RANK, MLA_HEADS * MLA_SLAB)), _const((1, MLA_RANK)),
                _resident((MLA_RANK, MLA_HEADS * (MLA_NOPE + MLA_V))),
                _const((1, MLA_SLAB)), _const((1, LANES)), _const((1, LANES))]
    args = [x, x, norm_g.reshape(1, d), mod, mod, mod, mod, w["w_in0"], w["da_qn"], w["da_kn"],
            w["mq_norm"], w["wq"], w["mkv_norm"], w["wkv"], w["gq"], w["gkn"], w["gkr"]]
    if rope:
        per = trunk.seq // tm
        in_specs += [pl.BlockSpec((tm, LANES), lambda i: (i % per, 0))] * 6
        args += list(tabs)
    row = lambda n: pl.BlockSpec((tm, n), lambda i: (i, 0))
    widths = (DA_W, DA_W, DA_W, MLA_HEADS * MLA_SLAB, MLA_HEADS * MLA_SLAB, MLA_HEADS * MLA_V)
    out_specs = [row(n) for n in widths]
    out_shape = [jax.ShapeDtypeStruct((rows, n), BF16) for n in widths]
    out_bytes = sum(_nbytes((tm, n), BF16) for n in widths)
    if not rope:
        cache_w = (DA_W, DA_W, MLA_RANK, MLA_ROPE)
        out_specs += [row(n) for n in cache_w]
        out_shape += [jax.ShapeDtypeStruct((rows, n), F32) for n in cache_w]
        out_bytes += sum(_nbytes((tm, n), F32) for n in cache_w)
    resident = _nbytes((d, AB_IN_PAD), BF16) + 2 * _nbytes((MLA_RANK, 2048), BF16)
    blocks = _nbytes((tm, d), F32) + out_bytes + 6 * _nbytes((tm, LANES), F32)
    body, c_bytes = _add_side_casts(functools.partial(_front_ab_kernel, rope, n_chunks), casts,
                                    rows // tm, lambda i: i, in_specs, args, out_specs, out_shape)
    blocks += c_bytes
    return pl.pallas_call(
        body,
        grid=(rows // tm,),
        in_specs=in_specs,
        out_specs=out_specs,
        out_shape=out_shape,
        scratch_shapes=[pltpu.VMEM((tm, d), BF16), pltpu.VMEM((tm, d), BF16)],
        compiler_params=_params(("arbitrary",),
                                _vmem_limit(blocks, resident + 2 * _nbytes((tm, d), BF16),
                                            4 * _nbytes((tm, d), F32))),
        name=f"front_ab_{tag}",
    )(*args)


def _ctx_mla_kernel(ckv_ref, kr_ref, wkv_ref, gkn_ref, gkr_ref, km_ref, vm_ref):
    ckv_b = ckv_ref[...].astype(BF16)
    kr = kr_ref[...]

    def kv_dot(c0):
        return jnp.dot(ckv_b, wkv_ref[:, c0:c0 + MXU_N], preferred_element_type=F32)

    _mla_keys(kv_dot, kr, kr * gkr_ref[...], gkn_ref, km_ref, vm_ref, slice(None))


def _ctx_mla(ckv, kr128, w):
    rows = ckv.shape[0]
    tm = 256
    blocks = (_nbytes((tm, MLA_RANK + LANES), F32) + _nbytes((MLA_RANK, 2048), BF16)
              + _nbytes((tm, 3072), BF16))
    return pl.pallas_call(
        _ctx_mla_kernel,
        grid=(rows // tm,),
        in_specs=[pl.BlockSpec((tm, MLA_RANK), lambda i: (i, 0)),
                  pl.BlockSpec((tm, LANES), lambda i: (i, 0)),
                  _const((MLA_RANK, 2048)), _const((1, LANES)), _const((1, LANES))],
        out_specs=[pl.BlockSpec((tm, 2048), lambda i: (i, 0)),
                   pl.BlockSpec((tm, 1024), lambda i: (i, 0))],
        out_shape=[jax.ShapeDtypeStruct((rows, 2048), BF16),
                   jax.ShapeDtypeStruct((rows, 1024), BF16)],
        compiler_params=_params(("arbitrary",), _vmem_limit(blocks, 0, 2 * _nbytes((tm, 2048), F32))),
        name="ctx_mla",
    )(ckv, kr128, w["wkv"], w["gkn"], w["gkr"])


def _front_c_kernel(rope, n_chunks, x0_ref, xn_ref, g_ref, sh0_ref, sc0_ref, shn_ref, scn_ref, w_ref,
                    qn_ref, kn_ref, *refs):
    if rope:
        tab_refs, outs, (h_ref, hn_ref) = refs[:3], refs[3:6], refs[6:]
    else:
        tab_refs, outs, (h_ref, hn_ref) = (), refs[:5], refs[5:]
    _norm_ahead(x0_ref, xn_ref, g_ref, sh0_ref, sc0_ref, shn_ref, scn_ref, h_ref, hn_ref)
    q_ref, k_ref, v_ref = outs[:3]
    rc = h_ref.shape[0] // n_chunks
    nq = GQ_HEADS * GQ_DIM
    nk = GQ_KV_HEADS * GQ_DIM
    qn_c = qn_ref[...] * (GQ_DIM ** -0.5 * LOG2E)
    for c in range(n_chunks):
        rows = slice(c * rc, (c + 1) * rc)

        def hdot(c0):
            return jnp.dot(h_ref[rows], w_ref[:, c0:c0 + MXU_N], preferred_element_type=F32)

        if rope:
            tabs = [t[rows] for t in tab_refs]
        for pair in range(GQ_HEADS // 2):
            pq = hdot(pair * MXU_N)
            for s in range(2):
                sl = slice((2 * pair + s) * GQ_DIM, (2 * pair + s + 1) * GQ_DIM)
                q = _rms(pq[:, s * GQ_DIM:(s + 1) * GQ_DIM], qn_c)
                if rope:
                    q = _rope(q, tabs, GQ_DIM // 4)
                q_ref[rows, sl] = q.astype(BF16)
        for pair in range(GQ_KV_HEADS // 2):
            pk = hdot(nq + pair * MXU_N)
            for s in range(2):
                sl = slice((2 * pair + s) * GQ_DIM, (2 * pair + s + 1) * GQ_DIM)
                k = _rms(pk[:, s * GQ_DIM:(s + 1) * GQ_DIM], kn_ref[...])
                if rope:
                    k = _rope(k, tabs, GQ_DIM // 4)
                else:
                    outs[3][rows, sl] = k
                k_ref[rows, sl] = k.astype(BF16)
        for cc in range(nk // MXU_N):
            v = hdot(nq + nk + cc * MXU_N)
            v_ref[rows, cc * MXU_N:(cc + 1) * MXU_N] = v.astype(BF16)
            if not rope:
                outs[4][rows, cc * MXU_N:(cc + 1) * MXU_N] = v

    h_ref[...] = hn_ref[...]


def _front_c(x, norm_g, mod, trunk, w_in, w, tabs, tag, casts=()):
    tm, n_chunks = 256, 1
    rows, rope, d = trunk.rows, trunk.rope, D_MODEL
    nq = GQ_HEADS * GQ_DIM
    nk = GQ_KV_HEADS * GQ_DIM
    n = nq + 2 * nk
    in_specs = _norm_ahead_specs(1, tm, trunk, d) + [
                _resident((d, n)), _const((1, GQ_DIM)), _const((1, GQ_DIM))]
    args = [x, x, norm_g.reshape(1, d), mod, mod, mod, mod, w_in, w["gq_qn"], w["gq_kn"]]
    if rope:
        per = trunk.seq // tm
        in_specs += [pl.BlockSpec((tm, LANES), lambda i: (i % per, 0))] * 3
        args += list(tabs)
    row = lambda wd: pl.BlockSpec((tm, wd), lambda i: (i, 0))
    out_specs = [row(nq), row(nk), row(nk)]
    out_shape = [jax.ShapeDtypeStruct((rows, wd), BF16) for wd in (nq, nk, nk)]
    out_bytes = _nbytes((tm, n), BF16)
    if not rope:
        out_specs += [row(nk), row(nk)]
        out_shape += [jax.ShapeDtypeStruct((rows, nk), F32)] * 2
        out_bytes += 2 * _nbytes((tm, nk), F32)
    blocks = _nbytes((tm, d), F32) + out_bytes + 3 * _nbytes((tm, LANES), F32)
    body, c_bytes = _add_side_casts(functools.partial(_front_c_kernel, rope, n_chunks), casts,
                                    rows // tm, lambda i: i, in_specs, args, out_specs, out_shape)
    blocks += c_bytes
    return pl.pallas_call(
        body,
        grid=(rows // tm,),
        in_specs=in_specs,
        out_specs=out_specs,
        out_shape=out_shape,
        scratch_shapes=[pltpu.VMEM((tm, d), BF16), pltpu.VMEM((tm, d), BF16)],
        compiler_params=_params(("arbitrary",),
                                _vmem_limit(blocks, _nbytes((d, n), BF16) + 2 * _nbytes((tm, d), BF16),
                                            4 * _nbytes((tm, d), F32))),
        name=f"front_c_{tag}",
    )(*args)


def _dot_nt(a, b):
    return lax.dot_general(a, b, (((1,), (1,)), ((), ())), preferred_element_type=F32)


def _attn_rows(q, parts):
    scores = [_dot_nt(q, k) for k, _ in parts]
    m = functools.reduce(jnp.maximum, [jnp.max(s, axis=-1, keepdims=True) for s in scores])
    o1 = None
    for s, (_, v) in zip(scores, parts):
        e = jnp.exp2(s - m).astype(BF16)
        v1 = jnp.concatenate([v, jnp.ones((v.shape[0], LANES), BF16)], axis=1)
        part = jnp.dot(e, v1, preferred_element_type=F32)
        o1 = part if o1 is None else o1 + part
    dv = parts[0][1].shape[1]
    return o1[:, :dv], o1[:, dv:]


ONES_ROWS = 16


def _attn_cols(q, parts, mask=None, sink=None):
    scores = [_dot_nt(k, q) for k, _ in parts]
    if mask is not None:
        scores[-1] = jnp.where(mask, scores[-1], NEG_INF)
    m = functools.reduce(jnp.maximum, [jnp.max(s, axis=0, keepdims=True) for s in scores])
    if sink is not None:
        sink2 = sink * LOG2E
        m = jnp.maximum(m, sink2)
    o1 = None
    for s, (_, v) in zip(scores, parts):
        e = jnp.exp2(s - m).astype(BF16)
        v_t1 = jnp.concatenate([v.astype(F32).T.astype(BF16),
                                jnp.ones((ONES_ROWS, v.shape[0]), BF16)], axis=0)
        part = jnp.dot(v_t1, e, preferred_element_type=F32)
        o1 = part if o1 is None else o1 + part
    dv = parts[0][1].shape[1]
    den = o1[dv:dv + 1]
    if sink is not None:
        den = den + jnp.exp2(sink2 - m)
    return o1[:dv], den


def _kv_parts(refs, n_parts, sl_k, sl_v):
    return [(refs[2 * p][:, sl_k].astype(BF16), refs[2 * p + 1][:, sl_v].astype(BF16))
            for p in range(n_parts)]


def _da_attn_kernel(n_parts, lam_init, lam_ref, gsub_ref, q_ref, *refs):
    o_ref = refs[-1]
    tq = q_ref.shape[0]
    lv = lam_ref[...]
    lam = (jnp.exp(jnp.sum(lv[0:1] * lv[1:2], axis=-1, keepdims=True))
           - jnp.exp(jnp.sum(lv[2:3] * lv[3:4], axis=-1, keepdims=True)) + lam_init)
    lo = lax.broadcasted_iota(jnp.int32, (1, HEAD_W), 1) < DA_QK_DIM
    for h in range(DA_HEADS):
        sl = slice(h * HEAD_W, (h + 1) * HEAD_W)
        q = q_ref[:, sl]
        zero = jnp.zeros_like(q)
        q12 = jnp.concatenate([jnp.where(lo, q, zero), jnp.where(lo, zero, q)], axis=0)
        o12, d12 = _attn_rows(q12, _kv_parts(refs, n_parts, sl, sl))
        o12 = o12 * (1.0 / d12)
        o = o12[:tq] - lam * o12[tq:]
        y = o * lax.rsqrt(jnp.mean(o * o, axis=-1, keepdims=True) + EPS) * gsub_ref[...]
        o_ref[:, sl] = (y * (1.0 - lam_init)).astype(BF16)


def _mla_attn_kernel(n_parts, q_ref, *refs):
    o_ref = refs[-1]
    for h in range(MLA_HEADS):
        sl_k = slice(h * MLA_SLAB, (h + 1) * MLA_SLAB)
        sl_v = slice(h * MLA_V, (h + 1) * MLA_V)
        o, den = _attn_rows(q_ref[:, sl_k], _kv_parts(refs, n_parts, sl_k, sl_v))
        o_ref[:, sl_v] = (o * (1.0 / den)).astype(BF16)


def _gq_attn_kernel(n_parts, seq, kw, sink_ref, q_ref, *refs):
    o_ref = refs[-1]
    tq = q_ref.shape[0]
    qi = pl.program_id(1)
    mask = None
    if n_parts == 2:
        start = pl.multiple_of(jnp.clip(qi * tq - WINDOW, 0, seq - kw), WINDOW)
        keys = start + lax.broadcasted_iota(jnp.int32, (kw, GQ_GROUP * tq), 0)
        qrows = qi * tq + (lax.broadcasted_iota(jnp.int32, (kw, GQ_GROUP * tq), 1) & (tq - 1))
        mask = jnp.abs(qrows - keys) <= WINDOW
    for g in range(GQ_KV_HEADS):
        sl = slice(g * GQ_DIM, (g + 1) * GQ_DIM)
        heads = range(g * GQ_GROUP, (g + 1) * GQ_GROUP)
        q4 = jnp.concatenate([q_ref[:, j * GQ_DIM:(j + 1) * GQ_DIM] for j in heads], axis=0)
        sink = jnp.concatenate([jnp.broadcast_to(sink_ref[j:j + 1, 0:1], (1, tq)) for j in heads], axis=1)
        if n_parts == 2:
            parts = [(refs[0][:, sl].astype(BF16), refs[1][:, sl]),
                     (refs[2][pl.ds(start, kw), sl], refs[3][pl.ds(start, kw), sl])]
        else:
            parts = [(refs[0][:, sl], refs[1][:, sl])]
        o, den = _attn_cols(q4, parts, mask=mask, sink=sink)
        o = o * (1.0 / den)
        for n, j in enumerate(heads):
            o_ref[:, j * GQ_DIM:(j + 1) * GQ_DIM] = o[:, n * tq:(n + 1) * tq].T.astype(BF16)


def _attention(kernel, name, q, kv_parts, trunk, tq, out_w, stacked_rows, extra_in=(), casts=()):
    nq = trunk.seq // tq
    in_specs = [_const(a.shape) for a in extra_in]
    args = list(extra_in)
    qw = q.shape[1]
    in_specs.append(pl.BlockSpec((tq, qw), lambda b, i: (b * nq + i, 0)))
    args.append(q)
    blocks = _nbytes((tq, qw), BF16) + _nbytes((tq, out_w), BF16)
    total_l = 0
    for k, v, l in kv_parts:
        in_specs += [pl.BlockSpec((l, k.shape[1]), lambda b, i: (b, 0)),
                     pl.BlockSpec((l, v.shape[1]), lambda b, i: (b, 0))]
        args += [k, v]
        blocks += _nbytes((l, k.shape[1]), k.dtype) + _nbytes((l, v.shape[1]), v.dtype)
        total_l += l
    out_specs = [pl.BlockSpec((tq, out_w), lambda b, i: (b * nq + i, 0))]
    out_shape = [jax.ShapeDtypeStruct((trunk.rows, out_w), BF16)]
    kernel, c_bytes = _add_side_casts(kernel, casts, trunk.batch * nq, lambda b, i: b * nq + i,
                                      in_specs, args, out_specs, out_shape)
    blocks += c_bytes
    out = pl.pallas_call(
        kernel,
        grid=(trunk.batch, nq),
        in_specs=in_specs,
        out_specs=out_specs,
        out_shape=out_shape,
        compiler_params=_params(("arbitrary",) * 2,
                                _vmem_limit(blocks, 0, 6 * _nbytes((stacked_rows, total_l), F32))),
        name=name,
    )(*args)
    return out if casts else out[0]


CAST_PLAN = {
    "front_ab": {"ff1_0": ("ff1_f32", 0), "w_out0": ("w_out0_f32", 0)},
    "da_attn": {"ff2_0": ("ff2_f32", 0)},
    "mla_attn": {"w_in1": ("w_in1_f32", 0)},
    "front_c": {"ff2_1": ("ff2_f32", 1), "w_out1": ("w_out1_f32", 0)},
    "gq_attn": {"ff1_1": ("ff1_f32", 1)},
}


def _run_trunk(x, trunk, tag, mod, P, ctx, wb):
    tq = min(trunk.seq, 256)
    casting = wb is None
    wb = dict(wb or {})

    def jobs(call):
        return [(P[src], layer) for src, layer in CAST_PLAN[call].values()] if casting else []

    def split(call, outs, n_main):
        if not casting:
            return outs
        for name, w in zip(CAST_PLAN[call], outs[n_main:]):
            wb[name] = w
        return outs[:n_main] if n_main > 1 else outs[0]

    n_front = 6 if trunk.rope else 10
    front = split("front_ab", _front_ab(x, P["norm1_g"][0], mod, trunk, P,
                                        P["tabs_ab"] if trunk.rope else None, tag,
                                        casts=jobs("front_ab")), n_front)
    qda, kda, vda, qm, km, vm = front[:6]
    da_parts, mla_parts = [(kda, vda, trunk.seq)], [(km, vm, trunk.seq)]
    if ctx is not None:
        da_parts = [(ctx["da_k"], ctx["da_v"], ctx["past"])] + da_parts
        mla_parts = [(ctx["mla_k"], ctx["mla_v"], ctx["past"])] + mla_parts
    o_da = split("da_attn", _attention(functools.partial(_da_attn_kernel, len(da_parts), P["lam_init"]),
                                       f"da_attn_{tag}", qda, da_parts, trunk, tq, DA_HEADS * HEAD_W,
                                       2 * tq, extra_in=(P["lam4"], P["gsub"]), casts=jobs("da_attn")), 1)
    o_m = split("mla_attn", _attention(functools.partial(_mla_attn_kernel, len(mla_parts)),
                                       f"mla_attn_{tag}", qm, mla_parts, trunk, tq, MLA_HEADS * MLA_V,
                                       tq, casts=jobs("mla_attn")), 1)
    x = _outproj(x, mod, 0, [(o_da, wb["w_out0"], 0), (o_m, wb["w_out0"], 1)], trunk, tag)
    x = _ffn(x, P["norm2_g"][0], mod, 0, wb["ff1_0"], wb["ff2_0"], trunk, tag)

    n_front_c = 3 if trunk.rope else 5
    front_c = split("front_c", _front_c(x, P["norm1_g"][1], mod, trunk, wb["w_in1"], P,
                                        P["tabs_gq"] if trunk.rope else None, tag,
                                        casts=jobs("front_c")), n_front_c)
    qc, kc, vc = front_c[:3]
    gq_parts = [(kc, vc, trunk.seq)]
    if ctx is not None:
        gq_parts = [(ctx["gq_k"], ctx["gq_v"], ctx["past"])] + gq_parts
    kw = min(trunk.seq, tq + 2 * WINDOW)
    o_c = split("gq_attn", _attention(functools.partial(_gq_attn_kernel, len(gq_parts), trunk.seq, kw),
                                      f"gq_attn_{tag}", qc, gq_parts, trunk, tq, GQ_HEADS * GQ_DIM,
                                      GQ_GROUP * tq, extra_in=(P["sink"],), casts=jobs("gq_attn")), 1)
    x = _outproj(x, mod, 1, [(o_c, wb["w_out1"], 0)], trunk, tag)
    x = _ffn(x, P["norm2_g"][1], mod, 1, wb["ff1_1"], wb["ff2_1"], trunk, tag)
    return x, front[6:], front_c[3:], wb


def kernel(x_prompt, x_sample, cache_da_k, cache_da_v, cache_mla_ckv, cache_mla_krope, cache_gq_k, cache_gq_v, c, c_ctx, norm1_g, norm2_g, ada_w, ada_b, ff1_w, ff2_w, ab_w_in, ab_w_out, da_lambda_q1, da_lambda_k1, da_lambda_q2, da_lambda_k2, da_q_norm, da_k_norm, da_subln, mla_q_a_norm, mla_w_q_up, mla_kv_a_norm, mla_w_kv_up, mla_q_norm, mla_k_norm, c_w_in, c_w_out, gq_q_norm, gq_k_norm, gq_sink):
    pb, ps, d = x_prompt.shape
    sb, ss, _ = x_sample.shape
    past = cache_da_k.shape[2]
    assert sb + 1 <= 8 and d == D_MODEL

    cond8 = jnp.concatenate([c_ctx[None], c, jnp.zeros((8 - 1 - sb, d), F32)], axis=0)
    wq = jnp.pad(mla_w_q_up[0].reshape(MLA_RANK, MLA_HEADS, MLA_QK),
                 ((0, 0), (0, 0), (0, MLA_SLAB - MLA_QK))).reshape(MLA_RANK, MLA_HEADS * MLA_SLAB)
    wkv3 = mla_w_kv_up[0].reshape(MLA_RANK, MLA_HEADS, MLA_NOPE + MLA_V)
    wkv = jnp.concatenate([wkv3[..., :MLA_NOPE].reshape(MLA_RANK, -1),
                           wkv3[..., MLA_NOPE:].reshape(MLA_RANK, -1)], axis=1)
    P = {
        "norm1_g": norm1_g, "norm2_g": norm2_g,
        "w_in0": jnp.pad(ab_w_in[0], ((0, 0), (0, AB_IN_PAD - AB_IN))).astype(BF16),
        "ff1_f32": ff1_w, "ff2_f32": ff2_w,
        "w_out0_f32": ab_w_out, "w_in1_f32": c_w_in, "w_out1_f32": c_w_out,
        "da_qn": jnp.tile(da_q_norm[0], 2).reshape(1, HEAD_W),
        "da_kn": jnp.tile(da_k_norm[0], 2).reshape(1, HEAD_W),
        "mq_norm": mla_q_a_norm[0].reshape(1, MLA_RANK),
        "mkv_norm": mla_kv_a_norm[0].reshape(1, MLA_RANK),
        "wq": wq.astype(BF16), "wkv": wkv.astype(BF16),
        "gq": jnp.pad(mla_q_norm[0], (0, MLA_SLAB - MLA_QK)).reshape(1, MLA_SLAB),
        "gkn": mla_k_norm[0, :MLA_NOPE].reshape(1, LANES),
        "gkr": jnp.pad(mla_k_norm[0, MLA_NOPE:], (0, LANES - MLA_ROPE)).reshape(1, LANES),
        "lam4": jnp.stack([da_lambda_q1[0], da_lambda_k1[0], da_lambda_q2[0], da_lambda_k2[0]]),
        "gsub": da_subln[0].reshape(1, HEAD_W),
        "lam_init": 0.8 - 0.6 * math.exp(-0.3 * 0),
        "gq_qn": gq_q_norm[0].reshape(1, GQ_DIM), "gq_kn": gq_k_norm[0].reshape(1, GQ_DIM),
        "sink": jnp.broadcast_to(gq_sink[0].reshape(GQ_HEADS, 1), (GQ_HEADS, LANES)),
        "tabs_ab": (_rope_tables(ss, [(32, "row"), (32, "col"), (32, "row"), (32, "col")])
                    + _rope_tables(ss, [(32, "row"), (32, "col"), (32, "none"), (32, "none")])),
        "tabs_gq": _rope_tables(ss, [(64, "row"), (64, "col")]),
    }

    mod = _ada_mod(cond8, ada_w, ada_b).reshape(2, 8, 1, 6 * d)

    kr_ctx = jnp.pad(cache_mla_krope[:, 0].reshape(sb * past, MLA_ROPE), ((0, 0), (0, LANES - MLA_ROPE)))
    mla_k_ctx, mla_v_ctx = _ctx_mla(cache_mla_ckv[:, 0].reshape(sb * past, MLA_RANK), kr_ctx, P)
    ctx = {
        "past": past,
        "da_k": cache_da_k[:, 0].reshape(sb * past, -1), "da_v": cache_da_v[:, 0].reshape(sb * past, -1),
        "mla_k": mla_k_ctx, "mla_v": mla_v_ctx,
        "gq_k": cache_gq_k[:, 0].reshape(sb * past, -1), "gq_v": cache_gq_v[:, 0].reshape(sb * past, -1),
    }

    prompt = Trunk(groups=1, seq=ps, batch=pb, mod_row0=0, rope=False)
    sample = Trunk(groups=sb, seq=ss, batch=sb, mod_row0=1, rope=True)
    y_p, (new_da_k, new_da_v, new_ckv, new_kr), (new_gq_k, new_gq_v), ffw = _run_trunk(
        x_prompt.reshape(pb * ps, d), prompt, "prompt", mod, P, None, None)
    y_s, _, _, _ = _run_trunk(x_sample.reshape(sb * ss, d), sample, "sample", mod, P, ctx, ffw)

    return (y_p.reshape(pb, ps, d), y_s.reshape(sb, ss, d),
            new_da_k.reshape(pb, 1, ps, DA_HEADS, HEAD_W), new_da_v.reshape(pb, 1, ps, DA_HEADS, HEAD_W),
            new_ckv.reshape(pb, 1, ps, MLA_RANK), new_kr.reshape(pb, 1, ps, MLA_ROPE),
            new_gq_k.reshape(pb, 1, ps, GQ_KV_HEADS, GQ_DIM), new_gq_v.reshape(pb, 1, ps, GQ_KV_HEADS, GQ_DIM))
```

```python
import functools
import math
from typing import NamedTuple

import jax
import jax.numpy as jnp
import numpy as np
from jax import lax
from jax.experimental import pallas as pl
from jax.experimental.pallas import tpu as pltpu

F32 = jnp.float32
BF16 = jnp.bfloat16

D_MODEL = 2048
GRID_W = 64
ROPE_BASE = 10000.0
EPS = 1e-6
NEG_INF = -1e30
LOG2E = math.log2(math.e)
DA_HEADS = 8
DA_QK_DIM = 64
DA_W = DA_HEADS * 2 * DA_QK_DIM
MLA_HEADS = 8
MLA_RANK = 512
MLA_NOPE = 128
MLA_ROPE = 64
MLA_V = 128
MLA_QK = MLA_NOPE + MLA_ROPE
MLA_SLAB = 256
GQ_HEADS = 16
GQ_KV_HEADS = 4
GQ_GROUP = GQ_HEADS // GQ_KV_HEADS
GQ_DIM = 128
WINDOW = 128
HEAD_W = 128
AB_IN = 4160
AB_IN_PAD = 4224

LANES = 128
MXU_N = 256
VMEM_CAP_BYTES = 60 * 1024 * 1024


class Trunk(NamedTuple):
    groups: int
    seq: int
    batch: int
    mod_row0: int
    rope: bool

    @property
    def rows(self):
        return self.batch * self.seq


def _vmem_limit(block_bytes, scratch_bytes=0, temp_bytes=0, reserve_all=True):
    if reserve_all:
        return VMEM_CAP_BYTES
    est = 2 * block_bytes + scratch_bytes + temp_bytes + (4 << 20)
    return int(min(max(est, 16 << 20), VMEM_CAP_BYTES))


def _nbytes(shape, dtype):
    return math.prod(shape) * jnp.dtype(dtype).itemsize


def _params(sem, vmem):
    return pltpu.CompilerParams(dimension_semantics=sem, vmem_limit_bytes=vmem)


def _resident(shape):
    return pl.BlockSpec(shape, lambda *_: (0,) * len(shape), pipeline_mode=pl.Buffered(1))


def _const(shape):
    return pl.BlockSpec(shape, lambda *_: (0,) * len(shape))


def _add_side_casts(kernel, casts, steps, step_of, in_specs, args, out_specs, out_shape):
    n_in, n_out, n_jobs = len(args), len(out_specs), len(casts)
    extra = 0
    for w, layer in casts:
        _, rows, cols = w.shape
        blk = rows // steps
        in_specs.append(pl.BlockSpec((None, blk, cols), lambda *ids, l=layer: (l, step_of(*ids), 0)))
        args.append(w)
        out_specs.append(pl.BlockSpec((blk, cols), lambda *ids: (step_of(*ids), 0)))
        out_shape.append(jax.ShapeDtypeStruct((rows, cols), BF16))
        extra += _nbytes((blk, cols), F32) + _nbytes((blk, cols), BF16)

    def wrapped(*refs):
        srcs = refs[n_in:n_in + n_jobs]
        dsts = refs[n_in + n_jobs + n_out:n_in + 2 * n_jobs + n_out]
        for src, dst in zip(srcs, dsts):
            dst[...] = src[...].astype(BF16)
        kernel(*refs[:n_in], *refs[n_in + n_jobs:n_in + n_jobs + n_out], *refs[n_in + 2 * n_jobs + n_out:])

    return (wrapped if casts else kernel), extra


def _ada_kernel(c_ref, w_ref, b_ref, o_ref):
    c = c_ref[...]
    s = (c / (1.0 + jnp.exp(-c))).astype(BF16)
    o_ref[0] = jnp.dot(s, w_ref[0].astype(BF16), preferred_element_type=F32) + b_ref[0]


def _ada_mod(cond8, ada_w, ada_b):
    depth, d, n = ada_w.shape
    tn = 1024
    blocks = _nbytes((d, tn), F32) + _nbytes((8, d), F32) + _nbytes((8, tn), F32)
    return pl.pallas_call(
        _ada_kernel,
        grid=(depth, n // tn),
        in_specs=[pl.BlockSpec((8, d), lambda l, j: (0, 0)),
                  pl.BlockSpec((1, d, tn), lambda l, j: (l, 0, j)),
                  pl.BlockSpec((1, 1, tn), lambda l, j: (l, 0, j))],
        out_specs=pl.BlockSpec((1, 8, tn), lambda l, j: (l, 0, j)),
        out_shape=jax.ShapeDtypeStruct((depth, 8, n), F32),
        compiler_params=_params(("arbitrary", "arbitrary"),
                                _vmem_limit(blocks, temp_bytes=_nbytes((d, tn), BF16))),
        name="ada_mod",
    )(cond8, ada_w, ada_b.reshape(depth, 1, n))


def _mod_spec(layer, chunk, tm, trunk):
    per = trunk.rows // trunk.groups // tm

    def idx(i, *_):
        return (layer, trunk.mod_row0 + i // per, 0, chunk)

    return pl.BlockSpec((1, 1, 1, D_MODEL), idx)


def _normmod(x, g, sc, sh):
    ms = jnp.mean(x * x, axis=-1, keepdims=True)
    y = x * lax.rsqrt(ms + EPS) * g
    return y * (1.0 + sc) + sh


def _rms(x, g):
    return x * lax.rsqrt(jnp.mean(x * x, axis=-1, keepdims=True) + EPS) * g


def _outproj_kernel(n_in, n_chunks, x_ref, gate_ref, g2_ref, sh2_ref, sc2_ref, *refs):
    o_ref, h_ref = refs[-2], refs[-1]
    rc = x_ref.shape[0] // n_chunks
    for c in range(n_chunks):
        rows = slice(c * rc, (c + 1) * rc)
        acc = None
        for k in range(n_in):
            part = jnp.dot(refs[2 * k][rows], refs[2 * k + 1][...], preferred_element_type=F32)
            acc = part if acc is None else acc + part
        x1 = x_ref[rows] + gate_ref[0, 0] * acc
        o_ref[rows] = x1
        h_ref[rows] = _normmod(x1, g2_ref[...], sc2_ref[0, 0], sh2_ref[0, 0]).astype(BF16)


def _outproj(x, norm2_g, mod, layer, pairs, trunk, tag):
    t, d = x.shape
    tm, n_chunks = 512, 2
    in_specs = [pl.BlockSpec((tm, d), lambda i: (i, 0)),
                _mod_spec(layer, 2, tm, trunk), _const((1, d)),
                _mod_spec(layer, 3, tm, trunk), _mod_spec(layer, 4, tm, trunk)]
    args = [x, mod, norm2_g.reshape(1, d), mod, mod]
    blocks = 2 * _nbytes((tm, d), F32) + _nbytes((tm, d), BF16)
    for o, w, blk in pairs:
        k = o.shape[1]
        in_specs += [pl.BlockSpec((tm, k), lambda i: (i, 0)),
                     pl.BlockSpec((k, d), lambda i, blk=blk: (blk, 0))]
        args += [o, w]
        blocks += _nbytes((tm, k), BF16) + _nbytes((k, d), BF16)
    return pl.pallas_call(
        functools.partial(_outproj_kernel, len(pairs), n_chunks),
        grid=(t // tm,),
        in_specs=in_specs,
        out_specs=[pl.BlockSpec((tm, d), lambda i: (i, 0)), pl.BlockSpec((tm, d), lambda i: (i, 0))],
        out_shape=[jax.ShapeDtypeStruct((t, d), F32), jax.ShapeDtypeStruct((t, d), BF16)],
        compiler_params=_params(("arbitrary",), _vmem_limit(blocks, 0, 2 * _nbytes((tm, d), F32))),
        name=f"outproj_l{layer}_{tag}",
    )(*args)


def _ffn_kernel(x_ref, h_ref, gate_ref, w1_ref, w2_ref, o_ref, acc_ref):
    f = pl.program_id(1)

    @pl.when(f == 0)
    def _():
        acc_ref[...] = jnp.zeros_like(acc_ref)

    a = jnp.dot(h_ref[...], w1_ref[...], preferred_element_type=F32)
    a = jnp.square(jnp.maximum(a, 0.0)).astype(BF16)
    acc_ref[...] += jnp.dot(a, w2_ref[...], preferred_element_type=F32)

    @pl.when(f == pl.num_programs(1) - 1)
    def _():
        o_ref[...] = x_ref[...] + gate_ref[0, 0] * acc_ref[...]


def _ffn(x, h, mod, layer, w1, w2, trunk, tag):
    t, d = x.shape
    ff = w1.shape[1]
    tm, tf = 512, 1024
    blocks = (2 * _nbytes((tm, d), F32) + _nbytes((tm, d), BF16) + _nbytes((d, tf), BF16)
              + _nbytes((tf, d), BF16) + _nbytes((1, d), F32))
    return pl.pallas_call(
        _ffn_kernel,
        grid=(t // tm, ff // tf),
        in_specs=[pl.BlockSpec((tm, d), lambda i, f: (i, 0)),
                  pl.BlockSpec((tm, d), lambda i, f: (i, 0)),
                  _mod_spec(layer, 5, tm, trunk),
                  pl.BlockSpec((d, tf), lambda i, f: (0, f)),
                  pl.BlockSpec((tf, d), lambda i, f: (f, 0))],
        out_specs=pl.BlockSpec((tm, d), lambda i, f: (i, 0)),
        out_shape=jax.ShapeDtypeStruct((t, d), F32),
        scratch_shapes=[pltpu.VMEM((tm, d), F32)],
        compiler_params=_params(("arbitrary", "arbitrary"),
                                _vmem_limit(blocks, _nbytes((tm, d), F32),
                                            _nbytes((tm, tf), F32) * 2 + _nbytes((tm, d), F32))),
        name=f"ffn_l{layer}_{tag}",
    )(x, h, mod, w1, w2)


def _rope_tables(seq, pattern):
    pos_row = (np.arange(seq) // GRID_W).astype(np.float64)
    pos_col = (np.arange(seq) % GRID_W).astype(np.float64)
    cos_cols, sa_cols, sb_cols = [], [], []
    for width, kind in pattern:
        if kind == "none":
            cos_cols.append(np.ones((seq, width)))
            sa_cols.append(np.zeros((seq, width)))
            sb_cols.append(np.zeros((seq, width)))
            continue
        half = width // 2
        inv = ROPE_BASE ** (-np.arange(half, dtype=np.float64) / half)
        pos = pos_row if kind == "row" else pos_col
        ang = pos[:, None] * inv
        cos, sin = np.cos(ang), np.sin(ang)
        zero = np.zeros_like(sin)
        cos_cols += [cos, cos]
        sa_cols += [-sin, zero]
        sb_cols += [zero, sin]
    tabs = [jnp.asarray(np.concatenate(c, axis=1), F32) for c in (cos_cols, sa_cols, sb_cols)]
    assert tabs[0].shape == (seq, LANES)
    return tabs


def _rope(x, tabs, half):
    cos, sin_a, sin_b = tabs
    return (x * cos + pltpu.roll(x, LANES - half, 1) * sin_a + pltpu.roll(x, half, 1) * sin_b)


def _mla_keys(kv_dot, kr, krg, gkn_ref, km_ref, vm_ref, rows):
    ss_kr = jnp.sum(kr * kr, axis=-1, keepdims=True)
    for pair in range(MLA_HEADS // 2):
        kn2 = kv_dot(pair * MXU_N)
        for s in range(2):
            h = 2 * pair + s
            kn = kn2[:, s * LANES:(s + 1) * LANES]
            r = lax.rsqrt((jnp.sum(kn * kn, axis=-1, keepdims=True) + ss_kr) / MLA_QK + EPS)
            km_ref[rows, h * MLA_SLAB:h * MLA_SLAB + LANES] = (kn * r * gkn_ref[...]).astype(BF16)
            km_ref[rows, h * MLA_SLAB + LANES:(h + 1) * MLA_SLAB] = (krg * r).astype(BF16)
    for c in range(MLA_HEADS * MLA_V // MXU_N):
        v = kv_dot(MLA_HEADS * MLA_NOPE + c * MXU_N)
        vm_ref[rows, c * MXU_N:(c + 1) * MXU_N] = v.astype(BF16)


def _norm_ahead(x0_ref, xn_ref, g_ref, sh0_ref, sc0_ref, shn_ref, scn_ref, h_ref, hn_ref):
    @pl.when(pl.program_id(0) == 0)
    def _():
        h_ref[...] = _normmod(x0_ref[...], g_ref[...], sc0_ref[0, 0], sh0_ref[0, 0]).astype(BF16)

    hn_ref[...] = _normmod(xn_ref[...], g_ref[...], scn_ref[0, 0], shn_ref[0, 0]).astype(BF16)


def _norm_ahead_specs(layer, tm, trunk, d):
    n_tiles = trunk.rows // tm
    per = trunk.rows // trunk.groups // tm
    nxt = lambda i: jnp.minimum(i + 1, n_tiles - 1)

    def mod(chunk, tile_of):
        return pl.BlockSpec((1, 1, 1, d), lambda i: (layer, trunk.mod_row0 + tile_of(i) // per, 0, chunk))

    return [pl.BlockSpec((tm, d), lambda i: (0, 0), pipeline_mode=pl.Buffered(1)),
            pl.BlockSpec((tm, d), lambda i: (nxt(i), 0)), _const((1, d)),
            mod(0, lambda i: 0), mod(1, lambda i: 0), mod(0, nxt), mod(1, nxt)]


def _front_ab_kernel(rope, n_chunks, x0_ref, xn_ref, g_ref, sh0_ref, sc0_ref, shn_ref, scn_ref, w_ref,
                     qn_ref, kn_ref, mqn_ref, wq_ref, mkvn_ref, wkv_ref, gq_ref, gkn_ref, gkr_ref, *refs):
    if rope:
        tab_refs, outs, (h_ref, hn_ref) = refs[:6], refs[6:12], refs[12:]
    else:
        tab_refs, outs, (h_ref, hn_ref) = (), refs[:10], refs[10:]
    _norm_ahead(x0_ref, xn_ref, g_ref, sh0_ref, sc0_ref, shn_ref, scn_ref, h_ref, hn_ref)
    qda_ref, kda_ref, vda_ref, qm_ref, km_ref, vm_ref = outs[:6]
    rc = h_ref.shape[0] // n_chunks
    lo = lax.broadcasted_iota(jnp.int32, (1, HEAD_W), 1) < DA_QK_DIM
    qn_c = qn_ref[...] * (DA_QK_DIM ** -0.5 * LOG2E)
    gq_c = gq_ref[...] * (MLA_QK ** -0.5 * LOG2E)

    def da_norm(x, g):
        sq = x * x
        s_lo = jnp.sum(jnp.where(lo, sq, 0.0), axis=-1, keepdims=True)
        s_hi = jnp.sum(jnp.where(lo, 0.0, sq), axis=-1, keepdims=True)
        r = jnp.where(lo, lax.rsqrt(s_lo / DA_QK_DIM + EPS), lax.rsqrt(s_hi / DA_QK_DIM + EPS))
        return x * r * g

    for c in range(n_chunks):
        rows = slice(c * rc, (c + 1) * rc)

        def hdot(c0, width=MXU_N):
            return jnp.dot(h_ref[rows], w_ref[:, c0:c0 + width], preferred_element_type=F32)

        if rope:
            tabs_da = [t[rows] for t in tab_refs[:3]]
            tabs_mla = [t[rows] for t in tab_refs[3:]]

        def da_pairs(pairs):
            for pair in pairs:
                pq = hdot(pair * MXU_N)
                pk = hdot(DA_W + pair * MXU_N)
                for s in range(2):
                    sl = slice((2 * pair + s) * HEAD_W, (2 * pair + s + 1) * HEAD_W)
                    q = da_norm(pq[:, s * HEAD_W:(s + 1) * HEAD_W], qn_c)
                    k = da_norm(pk[:, s * HEAD_W:(s + 1) * HEAD_W], kn_ref[...])
                    if rope:
                        q = _rope(q, tabs_da, DA_QK_DIM // 4)
                        k = _rope(k, tabs_da, DA_QK_DIM // 4)
                    else:
                        outs[6][rows, sl] = k
                    qda_ref[rows, sl] = q.astype(BF16)
                    kda_ref[rows, sl] = k.astype(BF16)

        def mla_q_heads(mq, heads):
            for h in heads:
                qf = jnp.dot(mq, wq_ref[:, h * MLA_SLAB:(h + 1) * MLA_SLAB], preferred_element_type=F32)
                a, b = qf[:, :LANES], qf[:, LANES:]
                ss = jnp.sum(a * a, axis=-1, keepdims=True) + jnp.sum(b * b, axis=-1, keepdims=True)
                r = lax.rsqrt(ss / MLA_QK + EPS)
                a = a * r * gq_c[:, :LANES]
                b = b * r * gq_c[:, LANES:]
                if rope:
                    b = _rope(b, tabs_mla, MLA_ROPE // 4)
                qm_ref[rows, h * MLA_SLAB:h * MLA_SLAB + LANES] = a.astype(BF16)
                qm_ref[rows, h * MLA_SLAB + LANES:(h + 1) * MLA_SLAB] = b.astype(BF16)

        mq = _rms(hdot(3 * DA_W, MLA_RANK), mqn_ref[...]).astype(BF16)
        ckv = _rms(hdot(3 * DA_W + MLA_RANK, MLA_RANK), mkvn_ref[...])
        kr = hdot(3 * DA_W + 2 * MLA_RANK, LANES)
        if not rope:
            outs[8][rows] = ckv
            outs[9][rows] = kr[:, :MLA_ROPE]
        ckv_b = ckv.astype(BF16)
        da_pairs(range(0, DA_HEADS // 4))
        mla_q_heads(mq, range(0, MLA_HEADS // 2))
        da_pairs(range(DA_HEADS // 4, DA_HEADS // 2))
        mla_q_heads(mq, range(MLA_HEADS // 2, MLA_HEADS))
        for cc in range(DA_W // MXU_N):
            dv = hdot(2 * DA_W + cc * MXU_N)
            vda_ref[rows, cc * MXU_N:(cc + 1) * MXU_N] = dv.astype(BF16)
            if not rope:
                outs[7][rows, cc * MXU_N:(cc + 1) * MXU_N] = dv
        krg = kr * gkr_ref[...]
        if rope:
            krg = _rope(krg, tabs_mla, MLA_ROPE // 4)

        def kv_dot(c0):
            return jnp.dot(ckv_b, wkv_ref[:, c0:c0 + MXU_N], preferred_element_type=F32)

        _mla_keys(kv_dot, kr, krg, gkn_ref, km_ref, vm_ref, rows)

    h_ref[...] = hn_ref[...]


def _front_ab(x, norm_g, mod, trunk, w, tabs, tag, casts=()):
    tm, n_chunks = 256, 1
    rows, rope, d = trunk.rows, trunk.rope, D_MODEL
    in_specs = _norm_ahead_specs(0, tm, trunk, d) + [
                _resident((d, AB_IN_PAD)),
                _const((1, HEAD_W)), _const((1, HEAD_W)), _const((1, MLA_RANK)),
                _resident((MLA_RANK, MLA_HEADS * MLA_SLAB)), _const((1, MLA_RANK)),
                _resident((MLA_RANK, MLA_HEADS * (MLA_NOPE + MLA_V))),
                _const((1, MLA_SLAB)), _const((1, LANES)), _const((1, LANES))]
    args = [x, x, norm_g.reshape(1, d), mod, mod, mod, mod, w["w_in0"], w["da_qn"], w["da_kn"],
            w["mq_norm"], w["wq"], w["mkv_norm"], w["wkv"], w["gq"], w["gkn"], w["gkr"]]
    if rope:
        per = trunk.seq // tm
        in_specs += [pl.BlockSpec((tm, LANES), lambda i: (i % per, 0))] * 6
        args += list(tabs)
    row = lambda n: pl.BlockSpec((tm, n), lambda i: (i, 0))
    widths = (DA_W, DA_W, DA_W, MLA_HEADS * MLA_SLAB, MLA_HEADS * MLA_SLAB, MLA_HEADS * MLA_V)
    out_specs = [row(n) for n in widths]
    out_shape = [jax.ShapeDtypeStruct((rows, n), BF16) for n in widths]
    out_bytes = sum(_nbytes((tm, n), BF16) for n in widths)
    if not rope:
        cache_w = (DA_W, DA_W, MLA_RANK, MLA_ROPE)
        out_specs += [row(n) for n in cache_w]
        out_shape += [jax.ShapeDtypeStruct((rows, n), F32) for n in cache_w]
        out_bytes += sum(_nbytes((tm, n), F32) for n in cache_w)
    resident = _nbytes((d, AB_IN_PAD), BF16) + 2 * _nbytes((MLA_RANK, 2048), BF16)
    blocks = _nbytes((tm, d), F32) + out_bytes + 6 * _nbytes((tm, LANES), F32)
    body, c_bytes = _add_side_casts(functools.partial(_front_ab_kernel, rope, n_chunks), casts,
                                    rows // tm, lambda i: i, in_specs, args, out_specs, out_shape)
    blocks += c_bytes
    return pl.pallas_call(
        body,
        grid=(rows // tm,),
        in_specs=in_specs,
        out_specs=out_specs,
        out_shape=out_shape,
        scratch_shapes=[pltpu.VMEM((tm, d), BF16), pltpu.VMEM((tm, d), BF16)],
        compiler_params=_params(("arbitrary",),
                                _vmem_limit(blocks, resident + 2 * _nbytes((tm, d), BF16),
                                            4 * _nbytes((tm, d), F32))),
        name=f"front_ab_{tag}",
    )(*args)


def _ctx_mla_kernel(ckv_ref, kr_ref, wkv_ref, gkn_ref, gkr_ref, km_ref, vm_ref):
    ckv_b = ckv_ref[...].astype(BF16)
    kr = kr_ref[...]

    def kv_dot(c0):
        return jnp.dot(ckv_b, wkv_ref[:, c0:c0 + MXU_N], preferred_element_type=F32)

    _mla_keys(kv_dot, kr, kr * gkr_ref[...], gkn_ref, km_ref, vm_ref, slice(None))


def _ctx_mla(ckv, kr128, w):
    rows = ckv.shape[0]
    tm = 256
    blocks = (_nbytes((tm, MLA_RANK + LANES), F32) + _nbytes((MLA_RANK, 2048), BF16)
              + _nbytes((tm, 3072), BF16))
    return pl.pallas_call(
        _ctx_mla_kernel,
        grid=(rows // tm,),
        in_specs=[pl.BlockSpec((tm, MLA_RANK), lambda i: (i, 0)),
                  pl.BlockSpec((tm, LANES), lambda i: (i, 0)),
                  _const((MLA_RANK, 2048)), _const((1, LANES)), _const((1, LANES))],
        out_specs=[pl.BlockSpec((tm, 2048), lambda i: (i, 0)),
                   pl.BlockSpec((tm, 1024), lambda i: (i, 0))],
        out_shape=[jax.ShapeDtypeStruct((rows, 2048), BF16),
                   jax.ShapeDtypeStruct((rows, 1024), BF16)],
        compiler_params=_params(("arbitrary",),
                                _vmem_limit(blocks, 0, 2 * _nbytes((tm, 2048), F32), reserve_all=False)),
        name="ctx_mla",
    )(ckv, kr128, w["wkv"], w["gkn"], w["gkr"])


def _front_c_kernel(rope, n_chunks, x0_ref, xn_ref, g_ref, sh0_ref, sc0_ref, shn_ref, scn_ref, w_ref,
                    qn_ref, kn_ref, *refs):
    if rope:
        tab_refs, outs, (h_ref, hn_ref) = refs[:3], refs[3:6], refs[6:]
    else:
        tab_refs, outs, (h_ref, hn_ref) = (), refs[:5], refs[5:]
    _norm_ahead(x0_ref, xn_ref, g_ref, sh0_ref, sc0_ref, shn_ref, scn_ref, h_ref, hn_ref)
    q_ref, k_ref, v_ref = outs[:3]
    rc = h_ref.shape[0] // n_chunks
    nq = GQ_HEADS * GQ_DIM
    nk = GQ_KV_HEADS * GQ_DIM
    qn_c = qn_ref[...] * (GQ_DIM ** -0.5 * LOG2E)
    for c in range(n_chunks):
        rows = slice(c * rc, (c + 1) * rc)

        def hdot(c0):
            return jnp.dot(h_ref[rows], w_ref[:, c0:c0 + MXU_N], preferred_element_type=F32)

        if rope:
            tabs = [t[rows] for t in tab_refs]
        for pair in range(GQ_HEADS // 2):
            pq = hdot(pair * MXU_N)
            for s in range(2):
                sl = slice((2 * pair + s) * GQ_DIM, (2 * pair + s + 1) * GQ_DIM)
                q = _rms(pq[:, s * GQ_DIM:(s + 1) * GQ_DIM], qn_c)
                if rope:
                    q = _rope(q, tabs, GQ_DIM // 4)
                q_ref[rows, sl] = q.astype(BF16)
        for pair in range(GQ_KV_HEADS // 2):
            pk = hdot(nq + pair * MXU_N)
            for s in range(2):
                sl = slice((2 * pair + s) * GQ_DIM, (2 * pair + s + 1) * GQ_DIM)
                k = _rms(pk[:, s * GQ_DIM:(s + 1) * GQ_DIM], kn_ref[...])
                if rope:
                    k = _rope(k, tabs, GQ_DIM // 4)
                else:
                    outs[3][rows, sl] = k
                k_ref[rows, sl] = k.astype(BF16)
        for cc in range(nk // MXU_N):
            v = hdot(nq + nk + cc * MXU_N)
            v_ref[rows, cc * MXU_N:(cc + 1) * MXU_N] = v.astype(BF16)
            if not rope:
                outs[4][rows, cc * MXU_N:(cc + 1) * MXU_N] = v

    h_ref[...] = hn_ref[...]


def _front_c(x, norm_g, mod, trunk, w_in, w, tabs, tag, casts=()):
    tm, n_chunks = 256, 1
    rows, rope, d = trunk.rows, trunk.rope, D_MODEL
    nq = GQ_HEADS * GQ_DIM
    nk = GQ_KV_HEADS * GQ_DIM
    n = nq + 2 * nk
    in_specs = _norm_ahead_specs(1, tm, trunk, d) + [
                _resident((d, n)), _const((1, GQ_DIM)), _const((1, GQ_DIM))]
    args = [x, x, norm_g.reshape(1, d), mod, mod, mod, mod, w_in, w["gq_qn"], w["gq_kn"]]
    if rope:
        per = trunk.seq // tm
        in_specs += [pl.BlockSpec((tm, LANES), lambda i: (i % per, 0))] * 3
        args += list(tabs)
    row = lambda wd: pl.BlockSpec((tm, wd), lambda i: (i, 0))
    out_specs = [row(nq), row(nk), row(nk)]
    out_shape = [jax.ShapeDtypeStruct((rows, wd), BF16) for wd in (nq, nk, nk)]
    out_bytes = _nbytes((tm, n), BF16)
    if not rope:
        out_specs += [row(nk), row(nk)]
        out_shape += [jax.ShapeDtypeStruct((rows, nk), F32)] * 2
        out_bytes += 2 * _nbytes((tm, nk), F32)
    blocks = _nbytes((tm, d), F32) + out_bytes + 3 * _nbytes((tm, LANES), F32)
    body, c_bytes = _add_side_casts(functools.partial(_front_c_kernel, rope, n_chunks), casts,
                                    rows // tm, lambda i: i, in_specs, args, out_specs, out_shape)
    blocks += c_bytes
    return pl.pallas_call(
        body,
        grid=(rows // tm,),
        in_specs=in_specs,
        out_specs=out_specs,
        out_shape=out_shape,
        scratch_shapes=[pltpu.VMEM((tm, d), BF16), pltpu.VMEM((tm, d), BF16)],
        compiler_params=_params(("arbitrary",),
                                _vmem_limit(blocks, _nbytes((d, n), BF16) + 2 * _nbytes((tm, d), BF16),
                                            4 * _nbytes((tm, d), F32))),
        name=f"front_c_{tag}",
    )(*args)


def _dot_nt(a, b):
    return lax.dot_general(a, b, (((1,), (1,)), ((), ())), preferred_element_type=F32)


def _attn_rows(q, parts):
    scores = [_dot_nt(q, k) for k, _ in parts]
    m = functools.reduce(jnp.maximum, [jnp.max(s, axis=-1, keepdims=True) for s in scores])
    o1 = None
    for s, (_, v) in zip(scores, parts):
        e = jnp.exp2(s - m).astype(BF16)
        v1 = jnp.concatenate([v, jnp.ones((v.shape[0], LANES), BF16)], axis=1)
        part = jnp.dot(e, v1, preferred_element_type=F32)
        o1 = part if o1 is None else o1 + part
    dv = parts[0][1].shape[1]
    return o1[:, :dv], o1[:, dv:]


ONES_ROWS = 16


def _attn_cols(q, parts, mask=None, sink=None):
    scores = [_dot_nt(k, q) for k, _ in parts]
    if mask is not None:
        scores[-1] = jnp.where(mask, scores[-1], NEG_INF)
    m = functools.reduce(jnp.maximum, [jnp.max(s, axis=0, keepdims=True) for s in scores])
    if sink is not None:
        sink2 = sink * LOG2E
        m = jnp.maximum(m, sink2)
    o1 = None
    for s, (_, v) in zip(scores, parts):
        e = jnp.exp2(s - m).astype(BF16)
        v_t1 = jnp.concatenate([v.astype(F32).T.astype(BF16),
                                jnp.ones((ONES_ROWS, v.shape[0]), BF16)], axis=0)
        part = jnp.dot(v_t1, e, preferred_element_type=F32)
        o1 = part if o1 is None else o1 + part
    dv = parts[0][1].shape[1]
    den = o1[dv:dv + 1]
    if sink is not None:
        den = den + jnp.exp2(sink2 - m)
    return o1[:dv], den


def _kv_parts(refs, n_parts, sl_k, sl_v):
    return [(refs[2 * p][:, sl_k].astype(BF16), refs[2 * p + 1][:, sl_v].astype(BF16))
            for p in range(n_parts)]


def _da_attn_kernel(n_parts, lam_init, lam_ref, gsub_ref, q_ref, *refs):
    o_ref = refs[-1]
    tq = q_ref.shape[0]
    lv = lam_ref[...]
    lam = (jnp.exp(jnp.sum(lv[0:1] * lv[1:2], axis=-1, keepdims=True))
           - jnp.exp(jnp.sum(lv[2:3] * lv[3:4], axis=-1, keepdims=True)) + lam_init)
    lo = lax.broadcasted_iota(jnp.int32, (1, HEAD_W), 1) < DA_QK_DIM
    for h in range(DA_HEADS):
        sl = slice(h * HEAD_W, (h + 1) * HEAD_W)
        q = q_ref[:, sl]
        zero = jnp.zeros_like(q)
        q12 = jnp.concatenate([jnp.where(lo, q, zero), jnp.where(lo, zero, q)], axis=0)
        o12, d12 = _attn_rows(q12, _kv_parts(refs, n_parts, sl, sl))
        o12 = o12 * (1.0 / d12)
        o = o12[:tq] - lam * o12[tq:]
        y = o * lax.rsqrt(jnp.mean(o * o, axis=-1, keepdims=True) + EPS) * gsub_ref[...]
        o_ref[:, sl] = (y * (1.0 - lam_init)).astype(BF16)


def _mla_attn_kernel(n_parts, q_ref, *refs):
    o_ref = refs[-1]
    for h in range(MLA_HEADS):
        sl_k = slice(h * MLA_SLAB, (h + 1) * MLA_SLAB)
        sl_v = slice(h * MLA_V, (h + 1) * MLA_V)
        o, den = _attn_rows(q_ref[:, sl_k], _kv_parts(refs, n_parts, sl_k, sl_v))
        o_ref[:, sl_v] = (o * (1.0 / den)).astype(BF16)


def _gq_attn_kernel(n_parts, seq, kw, sink_ref, q_ref, *refs):
    o_ref = refs[-1]
    tq = q_ref.shape[0]
    qi = pl.program_id(1)
    mask = None
    if n_parts == 2:
        start = pl.multiple_of(jnp.clip(qi * tq - WINDOW, 0, seq - kw), WINDOW)
        keys = start + lax.broadcasted_iota(jnp.int32, (kw, GQ_GROUP * tq), 0)
        qrows = qi * tq + (lax.broadcasted_iota(jnp.int32, (kw, GQ_GROUP * tq), 1) & (tq - 1))
        mask = jnp.abs(qrows - keys) <= WINDOW
    for g in range(GQ_KV_HEADS):
        sl = slice(g * GQ_DIM, (g + 1) * GQ_DIM)
        heads = range(g * GQ_GROUP, (g + 1) * GQ_GROUP)
        q4 = jnp.concatenate([q_ref[:, j * GQ_DIM:(j + 1) * GQ_DIM] for j in heads], axis=0)
        sink = jnp.concatenate([jnp.broadcast_to(sink_ref[j:j + 1, 0:1], (1, tq)) for j in heads], axis=1)
        if n_parts == 2:
            parts = [(refs[0][:, sl].astype(BF16), refs[1][:, sl]),
                     (refs[2][pl.ds(start, kw), sl], refs[3][pl.ds(start, kw), sl])]
        else:
            parts = [(refs[0][:, sl], refs[1][:, sl])]
        o, den = _attn_cols(q4, parts, mask=mask, sink=sink)
        o = o * (1.0 / den)
        for n, j in enumerate(heads):
            o_ref[:, j * GQ_DIM:(j + 1) * GQ_DIM] = o[:, n * tq:(n + 1) * tq].T.astype(BF16)


def _attention(kernel, name, q, kv_parts, trunk, tq, out_w, stacked_rows, extra_in=(), casts=()):
    nq = trunk.seq // tq
    in_specs = [_const(a.shape) for a in extra_in]
    args = list(extra_in)
    qw = q.shape[1]
    in_specs.append(pl.BlockSpec((tq, qw), lambda b, i: (b * nq + i, 0)))
    args.append(q)
    blocks = _nbytes((tq, qw), BF16) + _nbytes((tq, out_w), BF16)
    total_l = 0
    for k, v, l in kv_parts:
        in_specs += [pl.BlockSpec((l, k.shape[1]), lambda b, i: (b, 0)),
                     pl.BlockSpec((l, v.shape[1]), lambda b, i: (b, 0))]
        args += [k, v]
        blocks += _nbytes((l, k.shape[1]), k.dtype) + _nbytes((l, v.shape[1]), v.dtype)
        total_l += l
    out_specs = [pl.BlockSpec((tq, out_w), lambda b, i: (b * nq + i, 0))]
    out_shape = [jax.ShapeDtypeStruct((trunk.rows, out_w), BF16)]
    kernel, c_bytes = _add_side_casts(kernel, casts, trunk.batch * nq, lambda b, i: b * nq + i,
                                      in_specs, args, out_specs, out_shape)
    blocks += c_bytes
    out = pl.pallas_call(
        kernel,
        grid=(trunk.batch, nq),
        in_specs=in_specs,
        out_specs=out_specs,
        out_shape=out_shape,
        compiler_params=_params(("arbitrary",) * 2,
                                _vmem_limit(blocks, 0, 6 * _nbytes((stacked_rows, total_l), F32))),
        name=name,
    )(*args)
    return out if casts else out[0]


CAST_PLAN = {
    "front_ab": {"ff1_0": ("ff1_f32", 0), "w_out0": ("w_out0_f32", 0)},
    "da_attn": {"ff2_0": ("ff2_f32", 0)},
    "mla_attn": {"w_in1": ("w_in1_f32", 0)},
    "front_c": {"ff2_1": ("ff2_f32", 1), "w_out1": ("w_out1_f32", 0)},
    "gq_attn": {"ff1_1": ("ff1_f32", 1)},
}


def _run_trunk(x, trunk, tag, mod, P, ctx, wb):
    tq = min(trunk.seq, 256)
    casting = wb is None
    wb = dict(wb or {})

    def jobs(call):
        return [(P[src], layer) for src, layer in CAST_PLAN[call].values()] if casting else []

    def split(call, outs, n_main):
        if not casting:
            return outs
        for name, w in zip(CAST_PLAN[call], outs[n_main:]):
            wb[name] = w
        return outs[:n_main] if n_main > 1 else outs[0]

    n_front = 6 if trunk.rope else 10
    front = split("front_ab", _front_ab(x, P["norm1_g"][0], mod, trunk, P,
                                        P["tabs_ab"] if trunk.rope else None, tag,
                                        casts=jobs("front_ab")), n_front)
    qda, kda, vda, qm, km, vm = front[:6]
    da_parts, mla_parts = [(kda, vda, trunk.seq)], [(km, vm, trunk.seq)]
    if ctx is not None:
        da_parts = [(ctx["da_k"], ctx["da_v"], ctx["past"])] + da_parts
        mla_parts = [(ctx["mla_k"], ctx["mla_v"], ctx["past"])] + mla_parts
    o_da = split("da_attn", _attention(functools.partial(_da_attn_kernel, len(da_parts), P["lam_init"]),
                                       f"da_attn_{tag}", qda, da_parts, trunk, tq, DA_HEADS * HEAD_W,
                                       2 * tq, extra_in=(P["lam4"], P["gsub"]), casts=jobs("da_attn")), 1)
    o_m = split("mla_attn", _attention(functools.partial(_mla_attn_kernel, len(mla_parts)),
                                       f"mla_attn_{tag}", qm, mla_parts, trunk, tq, MLA_HEADS * MLA_V,
                                       tq, casts=jobs("mla_attn")), 1)
    x, h = _outproj(x, P["norm2_g"][0], mod, 0, [(o_da, wb["w_out0"], 0), (o_m, wb["w_out0"], 1)],
                    trunk, tag)
    x = _ffn(x, h, mod, 0, wb["ff1_0"], wb["ff2_0"], trunk, tag)

    n_front_c = 3 if trunk.rope else 5
    front_c = split("front_c", _front_c(x, P["norm1_g"][1], mod, trunk, wb["w_in1"], P,
                                        P["tabs_gq"] if trunk.rope else None, tag,
                                        casts=jobs("front_c")), n_front_c)
    qc, kc, vc = front_c[:3]
    gq_parts = [(kc, vc, trunk.seq)]
    if ctx is not None:
        gq_parts = [(ctx["gq_k"], ctx["gq_v"], ctx["past"])] + gq_parts
    kw = min(trunk.seq, tq + 2 * WINDOW)
    o_c = split("gq_attn", _attention(functools.partial(_gq_attn_kernel, len(gq_parts), trunk.seq, kw),
                                      f"gq_attn_{tag}", qc, gq_parts, trunk, tq, GQ_HEADS * GQ_DIM,
                                      GQ_GROUP * tq, extra_in=(P["sink"],), casts=jobs("gq_attn")), 1)
    x, h = _outproj(x, P["norm2_g"][1], mod, 1, [(o_c, wb["w_out1"], 0)], trunk, tag)
    x = _ffn(x, h, mod, 1, wb["ff1_1"], wb["ff2_1"], trunk, tag)
    return x, front[6:], front_c[3:], wb


def kernel(x_prompt, x_sample, cache_da_k, cache_da_v, cache_mla_ckv, cache_mla_krope, cache_gq_k, cache_gq_v, c, c_ctx, norm1_g, norm2_g, ada_w, ada_b, ff1_w, ff2_w, ab_w_in, ab_w_out, da_lambda_q1, da_lambda_k1, da_lambda_q2, da_lambda_k2, da_q_norm, da_k_norm, da_subln, mla_q_a_norm, mla_w_q_up, mla_kv_a_norm, mla_w_kv_up, mla_q_norm, mla_k_norm, c_w_in, c_w_out, gq_q_norm, gq_k_norm, gq_sink):
    pb, ps, d = x_prompt.shape
    sb, ss, _ = x_sample.shape
    past = cache_da_k.shape[2]
    assert sb + 1 <= 8 and d == D_MODEL

    cond8 = jnp.concatenate([c_ctx[None], c, jnp.zeros((8 - 1 - sb, d), F32)], axis=0)
    wq = jnp.pad(mla_w_q_up[0].reshape(MLA_RANK, MLA_HEADS, MLA_QK),
                 ((0, 0), (0, 0), (0, MLA_SLAB - MLA_QK))).reshape(MLA_RANK, MLA_HEADS * MLA_SLAB)
    wkv3 = mla_w_kv_up[0].reshape(MLA_RANK, MLA_HEADS, MLA_NOPE + MLA_V)
    wkv = jnp.concatenate([wkv3[..., :MLA_NOPE].reshape(MLA_RANK, -1),
                           wkv3[..., MLA_NOPE:].reshape(MLA_RANK, -1)], axis=1)
    P = {
        "norm1_g": norm1_g, "norm2_g": norm2_g,
        "w_in0": jnp.pad(ab_w_in[0], ((0, 0), (0, AB_IN_PAD - AB_IN))).astype(BF16),
        "ff1_f32": ff1_w, "ff2_f32": ff2_w,
        "w_out0_f32": ab_w_out, "w_in1_f32": c_w_in, "w_out1_f32": c_w_out,
        "da_qn": jnp.tile(da_q_norm[0], 2).reshape(1, HEAD_W),
        "da_kn": jnp.tile(da_k_norm[0], 2).reshape(1, HEAD_W),
        "mq_norm": mla_q_a_norm[0].reshape(1, MLA_RANK),
        "mkv_norm": mla_kv_a_norm[0].reshape(1, MLA_RANK),
        "wq": wq.astype(BF16), "wkv": wkv.astype(BF16),
        "gq": jnp.pad(mla_q_norm[0], (0, MLA_SLAB - MLA_QK)).reshape(1, MLA_SLAB),
        "gkn": mla_k_norm[0, :MLA_NOPE].reshape(1, LANES),
        "gkr": jnp.pad(mla_k_norm[0, MLA_NOPE:], (0, LANES - MLA_ROPE)).reshape(1, LANES),
        "lam4": jnp.stack([da_lambda_q1[0], da_lambda_k1[0], da_lambda_q2[0], da_lambda_k2[0]]),
        "gsub": da_subln[0].reshape(1, HEAD_W),
        "lam_init": 0.8 - 0.6 * math.exp(-0.3 * 0),
        "gq_qn": gq_q_norm[0].reshape(1, GQ_DIM), "gq_kn": gq_k_norm[0].reshape(1, GQ_DIM),
        "sink": jnp.broadcast_to(gq_sink[0].reshape(GQ_HEADS, 1), (GQ_HEADS, LANES)),
        "tabs_ab": (_rope_tables(ss, [(32, "row"), (32, "col"), (32, "row"), (32, "col")])
                    + _rope_tables(ss, [(32, "row"), (32, "col"), (32, "none"), (32, "none")])),
        "tabs_gq": _rope_tables(ss, [(64, "row"), (64, "col")]),
    }

    mod = _ada_mod(cond8, ada_w, ada_b).reshape(2, 8, 1, 6 * d)

    kr_ctx = jnp.pad(cache_mla_krope[:, 0].reshape(sb * past, MLA_ROPE), ((0, 0), (0, LANES - MLA_ROPE)))
    mla_k_ctx, mla_v_ctx = _ctx_mla(cache_mla_ckv[:, 0].reshape(sb * past, MLA_RANK), kr_ctx, P)
    ctx = {
        "past": past,
        "da_k": cache_da_k[:, 0].reshape(sb * past, -1), "da_v": cache_da_v[:, 0].reshape(sb * past, -1),
        "mla_k": mla_k_ctx, "mla_v": mla_v_ctx,
        "gq_k": cache_gq_k[:, 0].reshape(sb * past, -1), "gq_v": cache_gq_v[:, 0].reshape(sb * past, -1),
    }

    prompt = Trunk(groups=1, seq=ps, batch=pb, mod_row0=0, rope=False)
    sample = Trunk(groups=sb, seq=ss, batch=sb, mod_row0=1, rope=True)
    y_p, (new_da_k, new_da_v, new_ckv, new_kr), (new_gq_k, new_gq_v), ffw = _run_trunk(
        x_prompt.reshape(pb * ps, d), prompt, "prompt", mod, P, None, None)
    y_s, _, _, _ = _run_trunk(x_sample.reshape(sb * ss, d), sample, "sample", mod, P, ctx, ffw)

    return (y_p.reshape(pb, ps, d), y_s.reshape(sb, ss, d),
            new_da_k.reshape(pb, 1, ps, DA_HEADS, HEAD_W), new_da_v.reshape(pb, 1, ps, DA_HEADS, HEAD_W),
            new_ckv.reshape(pb, 1, ps, MLA_RANK), new_kr.reshape(pb, 1, ps, MLA_ROPE),
            new_gq_k.reshape(pb, 1, ps, GQ_KV_HEADS, GQ_DIM), new_gq_v.reshape(pb, 1, ps, GQ_KV_HEADS, GQ_DIM))
```

```python
import functools
import math
from typing import NamedTuple

import jax
import jax.numpy as jnp
import numpy as np
from jax import lax
from jax.experimental import pallas as pl
from jax.experimental.pallas import tpu as pltpu

F32 = jnp.float32
BF16 = jnp.bfloat16

D_MODEL = 2048
GRID_W = 64
ROPE_BASE = 10000.0
EPS = 1e-6
NEG_INF = -1e30
LOG2E = math.log2(math.e)
DA_HEADS = 8
DA_QK_DIM = 64
DA_W = DA_HEADS * 2 * DA_QK_DIM
MLA_HEADS = 8
MLA_RANK = 512
MLA_NOPE = 128
MLA_ROPE = 64
MLA_V = 128
MLA_QK = MLA_NOPE + MLA_ROPE
MLA_SLAB = 256
GQ_HEADS = 16
GQ_KV_HEADS = 4
GQ_GROUP = GQ_HEADS // GQ_KV_HEADS
GQ_DIM = 128
WINDOW = 128
HEAD_W = 128
AB_MAIN = 3 * DA_W + 2 * MLA_RANK

LANES = 128
MXU_N = 256
VMEM_CAP_BYTES = 60 * 1024 * 1024


class Trunk(NamedTuple):
    groups: int
    seq: int
    batch: int
    mod_row0: int
    rope: bool

    @property
    def rows(self):
        return self.batch * self.seq


def _vmem_limit(block_bytes, scratch_bytes=0, temp_bytes=0):
    est = 2 * block_bytes + scratch_bytes + temp_bytes
    assert est <= 2 * VMEM_CAP_BYTES, est
    return VMEM_CAP_BYTES


def _nbytes(shape, dtype):
    return math.prod(shape) * jnp.dtype(dtype).itemsize


def _params(sem, vmem):
    return pltpu.CompilerParams(dimension_semantics=sem, vmem_limit_bytes=vmem)


def _resident(shape):
    return pl.BlockSpec(shape, lambda *_: (0,) * len(shape), pipeline_mode=pl.Buffered(1))


def _const(shape):
    return pl.BlockSpec(shape, lambda *_: (0,) * len(shape))


def _add_side_casts(kernel, casts, steps, step_of, in_specs, args, out_specs, out_shape):
    n_in, n_out, n_jobs = len(args), len(out_specs), len(casts)
    extra = 0
    for w, layer in casts:
        _, rows, cols = w.shape
        blk = rows // steps
        in_specs.append(pl.BlockSpec((None, blk, cols), lambda *ids, l=layer: (l, step_of(*ids), 0)))
        args.append(w)
        out_specs.append(pl.BlockSpec((blk, cols), lambda *ids: (step_of(*ids), 0)))
        out_shape.append(jax.ShapeDtypeStruct((rows, cols), BF16))
        extra += _nbytes((blk, cols), F32) + _nbytes((blk, cols), BF16)

    def wrapped(*refs):
        srcs = refs[n_in:n_in + n_jobs]
        dsts = refs[n_in + n_jobs + n_out:n_in + 2 * n_jobs + n_out]
        for src, dst in zip(srcs, dsts):
            dst[...] = src[...].astype(BF16)
        kernel(*refs[:n_in], *refs[n_in + n_jobs:n_in + n_jobs + n_out], *refs[n_in + 2 * n_jobs + n_out:])

    return (wrapped if casts else kernel), extra


def _ada_kernel(c_ref, w_ref, b_ref, o_ref):
    c = c_ref[...]
    s = (c / (1.0 + jnp.exp(-c))).astype(BF16)
    o_ref[0] = jnp.dot(s, w_ref[0].astype(BF16), preferred_element_type=F32) + b_ref[0]


def _ada_mod(cond8, ada_w, ada_b):
    depth, d, n = ada_w.shape
    tn = 1024
    blocks = _nbytes((d, tn), F32) + _nbytes((8, d), F32) + _nbytes((8, tn), F32)
    return pl.pallas_call(
        _ada_kernel,
        grid=(depth, n // tn),
        in_specs=[pl.BlockSpec((8, d), lambda l, j: (0, 0)),
                  pl.BlockSpec((1, d, tn), lambda l, j: (l, 0, j)),
                  pl.BlockSpec((1, 1, tn), lambda l, j: (l, 0, j))],
        out_specs=pl.BlockSpec((1, 8, tn), lambda l, j: (l, 0, j)),
        out_shape=jax.ShapeDtypeStruct((depth, 8, n), F32),
        compiler_params=_params(("arbitrary", "arbitrary"),
                                _vmem_limit(blocks, temp_bytes=_nbytes((d, tn), BF16))),
        name="ada_mod",
    )(cond8, ada_w, ada_b.reshape(depth, 1, n))


def _mod_spec(layer, chunk, tm, trunk):
    per = trunk.rows // trunk.groups // tm

    def idx(i, *_):
        return (layer, trunk.mod_row0 + i // per, 0, chunk)

    return pl.BlockSpec((1, 1, 1, D_MODEL), idx)


def _normmod(x, g, sc, sh):
    ms = jnp.mean(x * x, axis=-1, keepdims=True)
    y = x * lax.rsqrt(ms + EPS) * g
    return y * (1.0 + sc) + sh


def _rms(x, g):
    return x * lax.rsqrt(jnp.mean(x * x, axis=-1, keepdims=True) + EPS) * g


def _outproj_kernel(n_in, n_chunks, x_ref, gate_ref, g2_ref, sh2_ref, sc2_ref, *refs):
    o_ref, h_ref = refs[-2], refs[-1]
    rc = x_ref.shape[0] // n_chunks
    for c in range(n_chunks):
        rows = slice(c * rc, (c + 1) * rc)
        acc = None
        for k in range(n_in):
            part = jnp.dot(refs[2 * k][rows], refs[2 * k + 1][...], preferred_element_type=F32)
            acc = part if acc is None else acc + part
        x1 = x_ref[rows] + gate_ref[0, 0] * acc
        o_ref[rows] = x1
        h_ref[rows] = _normmod(x1, g2_ref[...], sc2_ref[0, 0], sh2_ref[0, 0]).astype(BF16)


def _outproj(x, norm2_g, mod, layer, pairs, trunk, tag):
    t, d = x.shape
    tm, n_chunks = 512, 2
    in_specs = [pl.BlockSpec((tm, d), lambda i: (i, 0)),
                _mod_spec(layer, 2, tm, trunk), _const((1, d)),
                _mod_spec(layer, 3, tm, trunk), _mod_spec(layer, 4, tm, trunk)]
    args = [x, mod, norm2_g.reshape(1, d), mod, mod]
    blocks = 2 * _nbytes((tm, d), F32) + _nbytes((tm, d), BF16)
    for o, w, blk in pairs:
        k = o.shape[1]
        in_specs += [pl.BlockSpec((tm, k), lambda i: (i, 0)),
                     pl.BlockSpec((k, d), lambda i, blk=blk: (blk, 0))]
        args += [o, w]
        blocks += _nbytes((tm, k), BF16) + _nbytes((k, d), BF16)
    return pl.pallas_call(
        functools.partial(_outproj_kernel, len(pairs), n_chunks),
        grid=(t // tm,),
        in_specs=in_specs,
        out_specs=[pl.BlockSpec((tm, d), lambda i: (i, 0)), pl.BlockSpec((tm, d), lambda i: (i, 0))],
        out_shape=[jax.ShapeDtypeStruct((t, d), F32), jax.ShapeDtypeStruct((t, d), BF16)],
        compiler_params=_params(("arbitrary",), _vmem_limit(blocks, 0, 2 * _nbytes((tm, d), F32))),
        name=f"outproj_l{layer}_{tag}",
    )(*args)


def _ffn_kernel(x_ref, h_ref, gate_ref, w1_ref, w2_ref, o_ref, acc_ref):
    f = pl.program_id(1)

    @pl.when(f == 0)
    def _():
        acc_ref[...] = jnp.zeros_like(acc_ref)

    a = jnp.dot(h_ref[...], w1_ref[...], preferred_element_type=F32)
    a = jnp.square(jnp.maximum(a, 0.0)).astype(BF16)
    acc_ref[...] += jnp.dot(a, w2_ref[...], preferred_element_type=F32)

    @pl.when(f == pl.num_programs(1) - 1)
    def _():
        o_ref[...] = x_ref[...] + gate_ref[0, 0] * acc_ref[...]


def _ffn(x, h, mod, layer, w1, w2, trunk, tag):
    t, d = x.shape
    ff = w1.shape[1]
    tm, tf = 512, 1024
    blocks = (2 * _nbytes((tm, d), F32) + _nbytes((tm, d), BF16) + _nbytes((d, tf), BF16)
              + _nbytes((tf, d), BF16) + _nbytes((1, d), F32))
    return pl.pallas_call(
        _ffn_kernel,
        grid=(t // tm, ff // tf),
        in_specs=[pl.BlockSpec((tm, d), lambda i, f: (i, 0)),
                  pl.BlockSpec((tm, d), lambda i, f: (i, 0)),
                  _mod_spec(layer, 5, tm, trunk),
                  pl.BlockSpec((d, tf), lambda i, f: (0, f)),
                  pl.BlockSpec((tf, d), lambda i, f: (f, 0))],
        out_specs=pl.BlockSpec((tm, d), lambda i, f: (i, 0)),
        out_shape=jax.ShapeDtypeStruct((t, d), F32),
        scratch_shapes=[pltpu.VMEM((tm, d), F32)],
        compiler_params=_params(("arbitrary", "arbitrary"),
                                _vmem_limit(blocks, _nbytes((tm, d), F32),
                                            _nbytes((tm, tf), F32) * 2 + _nbytes((tm, d), F32))),
        name=f"ffn_l{layer}_{tag}",
    )(x, h, mod, w1, w2)


def _rope_tables(seq, pattern):
    pos_row = (np.arange(seq) // GRID_W).astype(np.float64)
    pos_col = (np.arange(seq) % GRID_W).astype(np.float64)
    cos_cols, sa_cols, sb_cols = [], [], []
    for width, kind in pattern:
        if kind == "none":
            cos_cols.append(np.ones((seq, width)))
            sa_cols.append(np.zeros((seq, width)))
            sb_cols.append(np.zeros((seq, width)))
            continue
        half = width // 2
        inv = ROPE_BASE ** (-np.arange(half, dtype=np.float64) / half)
        pos = pos_row if kind == "row" else pos_col
        ang = pos[:, None] * inv
        cos, sin = np.cos(ang), np.sin(ang)
        zero = np.zeros_like(sin)
        cos_cols += [cos, cos]
        sa_cols += [-sin, zero]
        sb_cols += [zero, sin]
    tabs = [jnp.asarray(np.concatenate(c, axis=1), F32) for c in (cos_cols, sa_cols, sb_cols)]
    assert tabs[0].shape == (seq, LANES)
    return tabs


def _rope(x, tabs, half):
    cos, sin_a, sin_b = tabs
    return (x * cos + pltpu.roll(x, LANES - half, 1) * sin_a + pltpu.roll(x, half, 1) * sin_b)


def _mla_keys(kv_dot, kr, krg, gkn_ref, km_ref, vm_ref, rows):
    ss_kr = jnp.sum(kr * kr, axis=-1, keepdims=True)
    for pair in range(MLA_HEADS // 2):
        kn2 = kv_dot(pair * MXU_N)
        for s in range(2):
            h = 2 * pair + s
            kn = kn2[:, s * LANES:(s + 1) * LANES]
            r = lax.rsqrt((jnp.sum(kn * kn, axis=-1, keepdims=True) + ss_kr) / MLA_QK + EPS)
            km_ref[rows, h * MLA_SLAB:h * MLA_SLAB + LANES] = (kn * r * gkn_ref[...]).astype(BF16)
            km_ref[rows, h * MLA_SLAB + LANES:(h + 1) * MLA_SLAB] = (krg * r).astype(BF16)
    for c in range(MLA_HEADS * MLA_V // MXU_N):
        v = kv_dot(MLA_HEADS * MLA_NOPE + c * MXU_N)
        vm_ref[rows, c * MXU_N:(c + 1) * MXU_N] = v.astype(BF16)


def _norm_ahead(x0_ref, xn_ref, g_ref, sh0_ref, sc0_ref, shn_ref, scn_ref, h_ref, hn_ref):
    @pl.when(pl.program_id(0) == 0)
    def _():
        h_ref[...] = _normmod(x0_ref[...], g_ref[...], sc0_ref[0, 0], sh0_ref[0, 0]).astype(BF16)

    hn_ref[...] = _normmod(xn_ref[...], g_ref[...], scn_ref[0, 0], shn_ref[0, 0]).astype(BF16)


def _norm_ahead_specs(layer, tm, trunk, d):
    n_tiles = trunk.rows // tm
    per = trunk.rows // trunk.groups // tm
    nxt = lambda i: jnp.minimum(i + 1, n_tiles - 1)

    def mod(chunk, tile_of):
        return pl.BlockSpec((1, 1, 1, d), lambda i: (layer, trunk.mod_row0 + tile_of(i) // per, 0, chunk))

    return [pl.BlockSpec((tm, d), lambda i: (0, 0), pipeline_mode=pl.Buffered(1)),
            pl.BlockSpec((tm, d), lambda i: (nxt(i), 0)), _const((1, d)),
            mod(0, lambda i: 0), mod(1, lambda i: 0), mod(0, nxt), mod(1, nxt)]


def _front_ab_kernel(rope, n_chunks, x0_ref, xn_ref, g_ref, sh0_ref, sc0_ref, shn_ref, scn_ref, w_ref,
                     wkr_ref, qn_ref, kn_ref, mqn_ref, wq_ref, mkvn_ref, wkv_ref, gq_ref, gkn_ref, gkr_ref,
                     *refs):
    if rope:
        tab_refs, outs, (h_ref, hn_ref) = refs[:6], refs[6:12], refs[12:]
    else:
        tab_refs, outs, (h_ref, hn_ref) = (), refs[:10], refs[10:]
    _norm_ahead(x0_ref, xn_ref, g_ref, sh0_ref, sc0_ref, shn_ref, scn_ref, h_ref, hn_ref)
    qda_ref, kda_ref, vda_ref, qm_ref, km_ref, vm_ref = outs[:6]
    rc = h_ref.shape[0] // n_chunks
    lo = lax.broadcasted_iota(jnp.int32, (1, HEAD_W), 1) < DA_QK_DIM
    qn_c = qn_ref[...] * (DA_QK_DIM ** -0.5 * LOG2E)
    gq_c = gq_ref[...] * (MLA_QK ** -0.5 * LOG2E)

    def da_norm(x, g):
        sq = x * x
        s_lo = jnp.sum(jnp.where(lo, sq, 0.0), axis=-1, keepdims=True)
        s_hi = jnp.sum(jnp.where(lo, 0.0, sq), axis=-1, keepdims=True)
        r = jnp.where(lo, lax.rsqrt(s_lo / DA_QK_DIM + EPS), lax.rsqrt(s_hi / DA_QK_DIM + EPS))
        return x * r * g

    for c in range(n_chunks):
        rows = slice(c * rc, (c + 1) * rc)

        def hdot(c0, width=MXU_N):
            return jnp.dot(h_ref[rows], w_ref[:, c0:c0 + width], preferred_element_type=F32)

        if rope:
            tabs_da = [t[rows] for t in tab_refs[:3]]
            tabs_mla = [t[rows] for t in tab_refs[3:]]

        def da_pairs(pairs):
            for pair in pairs:
                pq = hdot(pair * MXU_N)
                pk = hdot(DA_W + pair * MXU_N)
                for s in range(2):
                    sl = slice((2 * pair + s) * HEAD_W, (2 * pair + s + 1) * HEAD_W)
                    q = da_norm(pq[:, s * HEAD_W:(s + 1) * HEAD_W], qn_c)
                    k = da_norm(pk[:, s * HEAD_W:(s + 1) * HEAD_W], kn_ref[...])
                    if rope:
                        q = _rope(q, tabs_da, DA_QK_DIM // 4)
                        k = _rope(k, tabs_da, DA_QK_DIM // 4)
                    else:
                        outs[6][rows, sl] = k
                    qda_ref[rows, sl] = q.astype(BF16)
                    kda_ref[rows, sl] = k.astype(BF16)

        def mla_q_heads(mq, heads):
            for h in heads:
                qf = jnp.dot(mq, wq_ref[:, h * MLA_SLAB:(h + 1) * MLA_SLAB], preferred_element_type=F32)
                a, b = qf[:, :LANES], qf[:, LANES:]
                ss = jnp.sum(a * a, axis=-1, keepdims=True) + jnp.sum(b * b, axis=-1, keepdims=True)
                r = lax.rsqrt(ss / MLA_QK + EPS)
                a = a * r * gq_c[:, :LANES]
                b = b * r * gq_c[:, LANES:]
                if rope:
                    b = _rope(b, tabs_mla, MLA_ROPE // 4)
                qm_ref[rows, h * MLA_SLAB:h * MLA_SLAB + LANES] = a.astype(BF16)
                qm_ref[rows, h * MLA_SLAB + LANES:(h + 1) * MLA_SLAB] = b.astype(BF16)

        mq = _rms(hdot(3 * DA_W, MLA_RANK), mqn_ref[...]).astype(BF16)
        ckv = _rms(hdot(3 * DA_W + MLA_RANK, MLA_RANK), mkvn_ref[...])
        kr = jnp.dot(h_ref[rows], wkr_ref[...], preferred_element_type=F32)
        if not rope:
            outs[8][rows] = ckv
            outs[9][rows] = kr[:, :MLA_ROPE]
        ckv_b = ckv.astype(BF16)
        da_pairs(range(0, DA_HEADS // 4))
        mla_q_heads(mq, range(0, MLA_HEADS // 2))
        da_pairs(range(DA_HEADS // 4, DA_HEADS // 2))
        mla_q_heads(mq, range(MLA_HEADS // 2, MLA_HEADS))
        for cc in range(DA_W // MXU_N):
            dv = hdot(2 * DA_W + cc * MXU_N)
            vda_ref[rows, cc * MXU_N:(cc + 1) * MXU_N] = dv.astype(BF16)
            if not rope:
                outs[7][rows, cc * MXU_N:(cc + 1) * MXU_N] = dv
        krg = kr * gkr_ref[...]
        if rope:
            krg = _rope(krg, tabs_mla, MLA_ROPE // 4)

        def kv_dot(c0):
            return jnp.dot(ckv_b, wkv_ref[:, c0:c0 + MXU_N], preferred_element_type=F32)

        _mla_keys(kv_dot, kr, krg, gkn_ref, km_ref, vm_ref, rows)

    h_ref[...] = hn_ref[...]


def _front_ab(x, norm_g, mod, trunk, w, tabs, tag, casts=()):
    tm, n_chunks = 256, 1
    rows, rope, d = trunk.rows, trunk.rope, D_MODEL
    in_specs = _norm_ahead_specs(0, tm, trunk, d) + [
                _resident((d, AB_MAIN)), _resident((d, LANES)),
                _const((1, HEAD_W)), _const((1, HEAD_W)), _const((1, MLA_RANK)),
                _resident((MLA_RANK, MLA_HEADS * MLA_SLAB)), _const((1, MLA_RANK)),
                _resident((MLA_RANK, MLA_HEADS * (MLA_NOPE + MLA_V))),
                _const((1, MLA_SLAB)), _const((1, LANES)), _const((1, LANES))]
    args = [x, x, norm_g.reshape(1, d), mod, mod, mod, mod, w["w_in0"], w["w_kr0"], w["da_qn"], w["da_kn"],
            w["mq_norm"], w["wq"], w["mkv_norm"], w["wkv"], w["gq"], w["gkn"], w["gkr"]]
    if rope:
        per = trunk.seq // tm
        in_specs += [pl.BlockSpec((tm, LANES), lambda i: (i % per, 0))] * 6
        args += list(tabs)
    row = lambda n: pl.BlockSpec((tm, n), lambda i: (i, 0))
    widths = (DA_W, DA_W, DA_W, MLA_HEADS * MLA_SLAB, MLA_HEADS * MLA_SLAB, MLA_HEADS * MLA_V)
    out_specs = [row(n) for n in widths]
    out_shape = [jax.ShapeDtypeStruct((rows, n), BF16) for n in widths]
    out_bytes = sum(_nbytes((tm, n), BF16) for n in widths)
    if not rope:
        cache_w = (DA_W, DA_W, MLA_RANK, MLA_ROPE)
        out_specs += [row(n) for n in cache_w]
        out_shape += [jax.ShapeDtypeStruct((rows, n), F32) for n in cache_w]
        out_bytes += sum(_nbytes((tm, n), F32) for n in cache_w)
    resident = _nbytes((d, AB_MAIN + LANES), BF16) + 2 * _nbytes((MLA_RANK, 2048), BF16)
    blocks = _nbytes((tm, d), F32) + out_bytes + 6 * _nbytes((tm, LANES), F32)
    body, c_bytes = _add_side_casts(functools.partial(_front_ab_kernel, rope, n_chunks), casts,
                                    rows // tm, lambda i: i, in_specs, args, out_specs, out_shape)
    blocks += c_bytes
    return pl.pallas_call(
        body,
        grid=(rows // tm,),
        in_specs=in_specs,
        out_specs=out_specs,
        out_shape=out_shape,
        scratch_shapes=[pltpu.VMEM((tm, d), BF16), pltpu.VMEM((tm, d), BF16)],
        compiler_params=_params(("arbitrary",),
                                _vmem_limit(blocks, resident + 2 * _nbytes((tm, d), BF16),
                                            4 * _nbytes((tm, d), F32))),
        name=f"front_ab_{tag}",
    )(*args)


def _ctx_mla_kernel(ckv_ref, kr_ref, wkv_ref, gkn_ref, gkr_ref, km_ref, vm_ref):
    ckv_b = ckv_ref[...].astype(BF16)
    kr = kr_ref[...]

    def kv_dot(c0):
        return jnp.dot(ckv_b, wkv_ref[:, c0:c0 + MXU_N], preferred_element_type=F32)

    _mla_keys(kv_dot, kr, kr * gkr_ref[...], gkn_ref, km_ref, vm_ref, slice(None))


def _ctx_mla(ckv, kr128, w):
    rows = ckv.shape[0]
    tm = 256
    blocks = (_nbytes((tm, MLA_RANK + LANES), F32) + _nbytes((MLA_RANK, 2048), BF16)
              + _nbytes((tm, 3072), BF16))
    return pl.pallas_call(
        _ctx_mla_kernel,
        grid=(rows // tm,),
        in_specs=[pl.BlockSpec((tm, MLA_RANK), lambda i: (i, 0)),
                  pl.BlockSpec((tm, LANES), lambda i: (i, 0)),
                  _const((MLA_RANK, 2048)), _const((1, LANES)), _const((1, LANES))],
        out_specs=[pl.BlockSpec((tm, 2048), lambda i: (i, 0)),
                   pl.BlockSpec((tm, 1024), lambda i: (i, 0))],
        out_shape=[jax.ShapeDtypeStruct((rows, 2048), BF16),
                   jax.ShapeDtypeStruct((rows, 1024), BF16)],
        compiler_params=_params(("arbitrary",),
                                _vmem_limit(blocks, 0, 2 * _nbytes((tm, 2048), F32))),
        name="ctx_mla",
    )(ckv, kr128, w["wkv"], w["gkn"], w["gkr"])


def _front_c_kernel(rope, n_chunks, x0_ref, xn_ref, g_ref, sh0_ref, sc0_ref, shn_ref, scn_ref, w_ref,
                    qn_ref, kn_ref, *refs):
    if rope:
        tab_refs, outs, (h_ref, hn_ref) = refs[:3], refs[3:6], refs[6:]
    else:
        tab_refs, outs, (h_ref, hn_ref) = (), refs[:5], refs[5:]
    _norm_ahead(x0_ref, xn_ref, g_ref, sh0_ref, sc0_ref, shn_ref, scn_ref, h_ref, hn_ref)
    q_ref, k_ref, v_ref = outs[:3]
    rc = h_ref.shape[0] // n_chunks
    nq = GQ_HEADS * GQ_DIM
    nk = GQ_KV_HEADS * GQ_DIM
    qn_c = qn_ref[...] * (GQ_DIM ** -0.5 * LOG2E)

    def cache_rows(c, head):
        return pl.ds(c * rc * GQ_KV_HEADS + head, rc, stride=GQ_KV_HEADS)

    for c in range(n_chunks):
        rows = slice(c * rc, (c + 1) * rc)

        def hdot(c0):
            return jnp.dot(h_ref[rows], w_ref[:, c0:c0 + MXU_N], preferred_element_type=F32)

        if rope:
            tabs = [t[rows] for t in tab_refs]
        for pair in range(GQ_HEADS // 2):
            pq = hdot(pair * MXU_N)
            for s in range(2):
                sl = slice((2 * pair + s) * GQ_DIM, (2 * pair + s + 1) * GQ_DIM)
                q = _rms(pq[:, s * GQ_DIM:(s + 1) * GQ_DIM], qn_c)
                if rope:
                    q = _rope(q, tabs, GQ_DIM // 4)
                q_ref[rows, sl] = q.astype(BF16)
        for pair in range(GQ_KV_HEADS // 2):
            pk = hdot(nq + pair * MXU_N)
            for s in range(2):
                sl = slice((2 * pair + s) * GQ_DIM, (2 * pair + s + 1) * GQ_DIM)
                k = _rms(pk[:, s * GQ_DIM:(s + 1) * GQ_DIM], kn_ref[...])
                if rope:
                    k = _rope(k, tabs, GQ_DIM // 4)
                else:
                    outs[3][cache_rows(c, 2 * pair + s), :] = k
                k_ref[rows, sl] = k.astype(BF16)
        for cc in range(nk // MXU_N):
            v = hdot(nq + nk + cc * MXU_N)
            v_ref[rows, cc * MXU_N:(cc + 1) * MXU_N] = v.astype(BF16)
            if not rope:
                for s in range(2):
                    outs[4][cache_rows(c, 2 * cc + s), :] = v[:, s * GQ_DIM:(s + 1) * GQ_DIM]

    h_ref[...] = hn_ref[...]


def _front_c(x, norm_g, mod, trunk, w_in, w, tabs, tag, casts=()):
    tm, n_chunks = 256, 1
    rows, rope, d = trunk.rows, trunk.rope, D_MODEL
    nq = GQ_HEADS * GQ_DIM
    nk = GQ_KV_HEADS * GQ_DIM
    n = nq + 2 * nk
    in_specs = _norm_ahead_specs(1, tm, trunk, d) + [
                _resident((d, n)), _const((1, GQ_DIM)), _const((1, GQ_DIM))]
    args = [x, x, norm_g.reshape(1, d), mod, mod, mod, mod, w_in, w["gq_qn"], w["gq_kn"]]
    if rope:
        per = trunk.seq // tm
        in_specs += [pl.BlockSpec((tm, LANES), lambda i: (i % per, 0))] * 3
        args += list(tabs)
    row = lambda wd: pl.BlockSpec((tm, wd), lambda i: (i, 0))
    out_specs = [row(nq), row(nk), row(nk)]
    out_shape = [jax.ShapeDtypeStruct((rows, wd), BF16) for wd in (nq, nk, nk)]
    out_bytes = _nbytes((tm, n), BF16)
    if not rope:
        out_specs += [pl.BlockSpec((tm * GQ_KV_HEADS, GQ_DIM), lambda i: (i, 0))] * 2
        out_shape += [jax.ShapeDtypeStruct((rows * GQ_KV_HEADS, GQ_DIM), F32)] * 2
        out_bytes += 2 * _nbytes((tm, nk), F32)
    blocks = _nbytes((tm, d), F32) + out_bytes + 3 * _nbytes((tm, LANES), F32)
    body, c_bytes = _add_side_casts(functools.partial(_front_c_kernel, rope, n_chunks), casts,
                                    rows // tm, lambda i: i, in_specs, args, out_specs, out_shape)
    blocks += c_bytes
    return pl.pallas_call(
        body,
        grid=(rows // tm,),
        in_specs=in_specs,
        out_specs=out_specs,
        out_shape=out_shape,
        scratch_shapes=[pltpu.VMEM((tm, d), BF16), pltpu.VMEM((tm, d), BF16)],
        compiler_params=_params(("arbitrary",),
                                _vmem_limit(blocks, _nbytes((d, n), BF16) + 2 * _nbytes((tm, d), BF16),
                                            4 * _nbytes((tm, d), F32))),
        name=f"front_c_{tag}",
    )(*args)


def _dot_nt(a, b):
    return lax.dot_general(a, b, (((1,), (1,)), ((), ())), preferred_element_type=F32)


def _attn_rows(q, parts):
    scores = [_dot_nt(q, k) for k, _ in parts]
    m = functools.reduce(jnp.maximum, [jnp.max(s, axis=-1, keepdims=True) for s in scores])
    o1 = None
    for s, (_, v) in zip(scores, parts):
        e = jnp.exp2(s - m).astype(BF16)
        v1 = jnp.concatenate([v, jnp.ones((v.shape[0], LANES), BF16)], axis=1)
        part = jnp.dot(e, v1, preferred_element_type=F32)
        o1 = part if o1 is None else o1 + part
    dv = parts[0][1].shape[1]
    return o1[:, :dv], o1[:, dv:]


ONES_ROWS = 16


def _attn_cols(q, parts, mask=None, sink=None):
    scores = [_dot_nt(k, q) for k, _ in parts]
    if mask is not None:
        scores[-1] = jnp.where(mask, scores[-1], NEG_INF)
    m = functools.reduce(jnp.maximum, [jnp.max(s, axis=0, keepdims=True) for s in scores])
    if sink is not None:
        sink2 = sink * LOG2E
        m = jnp.maximum(m, sink2)
    o1 = None
    for s, (_, v) in zip(scores, parts):
        e = jnp.exp2(s - m).astype(BF16)
        v_t1 = jnp.concatenate([v.astype(F32).T.astype(BF16),
                                jnp.ones((ONES_ROWS, v.shape[0]), BF16)], axis=0)
        part = jnp.dot(v_t1, e, preferred_element_type=F32)
        o1 = part if o1 is None else o1 + part
    dv = parts[0][1].shape[1]
    den = o1[dv:dv + 1]
    if sink is not None:
        den = den + jnp.exp2(sink2 - m)
    return o1[:dv], den


def _kv_parts(refs, n_parts, sl_k, sl_v):
    return [(refs[2 * p][:, sl_k].astype(BF16), refs[2 * p + 1][:, sl_v].astype(BF16))
            for p in range(n_parts)]


def _da_attn_kernel(n_parts, lam_init, lam_ref, gsub_ref, q_ref, *refs):
    o_ref = refs[-1]
    tq = q_ref.shape[0]
    lv = lam_ref[...]
    lam = (jnp.exp(jnp.sum(lv[0:1] * lv[1:2], axis=-1, keepdims=True))
           - jnp.exp(jnp.sum(lv[2:3] * lv[3:4], axis=-1, keepdims=True)) + lam_init)
    lo = lax.broadcasted_iota(jnp.int32, (1, HEAD_W), 1) < DA_QK_DIM
    for h in range(DA_HEADS):
        sl = slice(h * HEAD_W, (h + 1) * HEAD_W)
        q = q_ref[:, sl]
        zero = jnp.zeros_like(q)
        q12 = jnp.concatenate([jnp.where(lo, q, zero), jnp.where(lo, zero, q)], axis=0)
        o12, d12 = _attn_rows(q12, _kv_parts(refs, n_parts, sl, sl))
        o12 = o12 * (1.0 / d12)
        o = o12[:tq] - lam * o12[tq:]
        y = o * lax.rsqrt(jnp.mean(o * o, axis=-1, keepdims=True) + EPS) * gsub_ref[...]
        o_ref[:, sl] = (y * (1.0 - lam_init)).astype(BF16)


def _mla_attn_kernel(n_parts, q_ref, *refs):
    o_ref = refs[-1]
    for h in range(MLA_HEADS):
        sl_k = slice(h * MLA_SLAB, (h + 1) * MLA_SLAB)
        sl_v = slice(h * MLA_V, (h + 1) * MLA_V)
        o, den = _attn_rows(q_ref[:, sl_k], _kv_parts(refs, n_parts, sl_k, sl_v))
        o_ref[:, sl_v] = (o * (1.0 / den)).astype(BF16)


def _gq_attn_kernel(n_parts, seq, kw, sink_ref, q_ref, *refs):
    o_ref = refs[-1]
    tq = q_ref.shape[0]
    qi = pl.program_id(1)
    mask = None
    if n_parts == 2:
        start = pl.multiple_of(jnp.clip(qi * tq - WINDOW, 0, seq - kw), WINDOW)
        keys = start + lax.broadcasted_iota(jnp.int32, (kw, GQ_GROUP * tq), 0)
        qrows = qi * tq + (lax.broadcasted_iota(jnp.int32, (kw, GQ_GROUP * tq), 1) & (tq - 1))
        mask = jnp.abs(qrows - keys) <= WINDOW
    for g in range(GQ_KV_HEADS):
        sl = slice(g * GQ_DIM, (g + 1) * GQ_DIM)
        heads = range(g * GQ_GROUP, (g + 1) * GQ_GROUP)
        q4 = jnp.concatenate([q_ref[:, j * GQ_DIM:(j + 1) * GQ_DIM] for j in heads], axis=0)
        sink = jnp.concatenate([jnp.broadcast_to(sink_ref[j:j + 1, 0:1], (1, tq)) for j in heads], axis=1)
        if n_parts == 2:
            parts = [(refs[0][:, sl].astype(BF16), refs[1][:, sl]),
                     (refs[2][pl.ds(start, kw), sl], refs[3][pl.ds(start, kw), sl])]
        else:
            parts = [(refs[0][:, sl], refs[1][:, sl])]
        o, den = _attn_cols(q4, parts, mask=mask, sink=sink)
        o = o * (1.0 / den)
        for n, j in enumerate(heads):
            o_ref[:, j * GQ_DIM:(j + 1) * GQ_DIM] = o[:, n * tq:(n + 1) * tq].T.astype(BF16)


def _per_sequence(kernel, sps, n_extra, n_parts):
    def wrapped(*refs):
        for s in range(sps):
            def rows(r):
                n = r.shape[0] // sps
                return r.at[pl.ds(s * n, n)]

            kernel(*refs[:n_extra], *[rows(r) for r in refs[n_extra:n_extra + 2 + 2 * n_parts]])

    return wrapped if sps > 1 else kernel


def _attention(kernel, name, q, kv_parts, trunk, tq, out_w, stacked_rows, extra_in=(), casts=(), sps=1):
    nq = trunk.seq // tq
    assert sps == 1 or nq == 1
    in_specs = [_const(a.shape) for a in extra_in]
    args = list(extra_in)
    qw = q.shape[1]
    in_specs.append(pl.BlockSpec((sps * tq, qw), lambda b, i: (b * nq + i, 0)))
    args.append(q)
    blocks = sps * (_nbytes((tq, qw), BF16) + _nbytes((tq, out_w), BF16))
    total_l = 0
    for k, v, l in kv_parts:
        in_specs += [pl.BlockSpec((sps * l, k.shape[1]), lambda b, i: (b, 0)),
                     pl.BlockSpec((sps * l, v.shape[1]), lambda b, i: (b, 0))]
        args += [k, v]
        blocks += sps * (_nbytes((l, k.shape[1]), k.dtype) + _nbytes((l, v.shape[1]), v.dtype))
        total_l += l
    out_specs = [pl.BlockSpec((sps * tq, out_w), lambda b, i: (b * nq + i, 0))]
    out_shape = [jax.ShapeDtypeStruct((trunk.rows, out_w), BF16)]
    kernel = _per_sequence(kernel, sps, len(extra_in), len(kv_parts))
    kernel, c_bytes = _add_side_casts(kernel, casts, trunk.batch // sps * nq, lambda b, i: b * nq + i,
                                      in_specs, args, out_specs, out_shape)
    blocks += c_bytes
    out = pl.pallas_call(
        kernel,
        grid=(trunk.batch // sps, nq),
        in_specs=in_specs,
        out_specs=out_specs,
        out_shape=out_shape,
        compiler_params=_params(("arbitrary",) * 2,
                                _vmem_limit(blocks, 0, 6 * _nbytes((stacked_rows, total_l), F32))),
        name=name,
    )(*args)
    return out if casts else out[0]


CAST_PLAN = {
    "front_ab": {"ff1_0": ("ff1_f32", 0), "w_out0": ("w_out0_f32", 0), "w_in1": ("w_in1_f32", 0)},
    "da_attn": {"ff2_0": ("ff2_f32", 0)},
    "mla_attn": {},
    "front_c": {"ff2_1": ("ff2_f32", 1), "w_out1": ("w_out1_f32", 0)},
    "gq_attn": {"ff1_1": ("ff1_f32", 1)},
}


def _run_trunk(x, trunk, tag, mod, P, ctx, wb):
    tq = min(trunk.seq, 256)
    sps = 4 if trunk.seq == tq and trunk.batch % 4 == 0 else 1
    casting = wb is None
    wb = dict(wb or {})

    def jobs(call):
        return [(P[src], layer) for src, layer in CAST_PLAN[call].values()] if casting else []

    def split(call, outs, n_main):
        if not casting or not CAST_PLAN[call]:
            return outs
        for name, w in zip(CAST_PLAN[call], outs[n_main:]):
            wb[name] = w
        return outs[:n_main] if n_main > 1 else outs[0]

    n_front = 6 if trunk.rope else 10
    front = split("front_ab", _front_ab(x, P["norm1_g"][0], mod, trunk, P,
                                        P["tabs_ab"] if trunk.rope else None, tag,
                                        casts=jobs("front_ab")), n_front)
    qda, kda, vda, qm, km, vm = front[:6]
    da_parts, mla_parts = [(kda, vda, trunk.seq)], [(km, vm, trunk.seq)]
    if ctx is not None:
        da_parts = [(ctx["da_k"], ctx["da_v"], ctx["past"])] + da_parts
        mla_parts = [(ctx["mla_k"], ctx["mla_v"], ctx["past"])] + mla_parts
    o_da = split("da_attn", _attention(functools.partial(_da_attn_kernel, len(da_parts), P["lam_init"]),
                                       f"da_attn_{tag}", qda, da_parts, trunk, tq, DA_HEADS * HEAD_W,
                                       2 * tq, extra_in=(P["lam4"], P["gsub"]), casts=jobs("da_attn"),
                                       sps=sps), 1)
    o_m = split("mla_attn", _attention(functools.partial(_mla_attn_kernel, len(mla_parts)),
                                       f"mla_attn_{tag}", qm, mla_parts, trunk, tq, MLA_HEADS * MLA_V,
                                       tq, casts=jobs("mla_attn"), sps=sps), 1)
    x, h = _outproj(x, P["norm2_g"][0], mod, 0, [(o_da, wb["w_out0"], 0), (o_m, wb["w_out0"], 1)],
                    trunk, tag)
    x = _ffn(x, h, mod, 0, wb["ff1_0"], wb["ff2_0"], trunk, tag)

    n_front_c = 3 if trunk.rope else 5
    front_c = split("front_c", _front_c(x, P["norm1_g"][1], mod, trunk, wb["w_in1"], P,
                                        P["tabs_gq"] if trunk.rope else None, tag,
                                        casts=jobs("front_c")), n_front_c)
    qc, kc, vc = front_c[:3]
    gq_parts = [(kc, vc, trunk.seq)]
    if ctx is not None:
        gq_parts = [(ctx["gq_k"], ctx["gq_v"], ctx["past"])] + gq_parts
    kw = min(trunk.seq, tq + 2 * WINDOW)
    o_c = split("gq_attn", _attention(functools.partial(_gq_attn_kernel, len(gq_parts), trunk.seq, kw),
                                      f"gq_attn_{tag}", qc, gq_parts, trunk, tq, GQ_HEADS * GQ_DIM,
                                      GQ_GROUP * tq, extra_in=(P["sink"],), casts=jobs("gq_attn"),
                                      sps=sps), 1)
    x, h = _outproj(x, P["norm2_g"][1], mod, 1, [(o_c, wb["w_out1"], 0)], trunk, tag)
    x = _ffn(x, h, mod, 1, wb["ff1_1"], wb["ff2_1"], trunk, tag)
    return x, front[6:], front_c[3:], wb


def kernel(x_prompt, x_sample, cache_da_k, cache_da_v, cache_mla_ckv, cache_mla_krope, cache_gq_k, cache_gq_v, c, c_ctx, norm1_g, norm2_g, ada_w, ada_b, ff1_w, ff2_w, ab_w_in, ab_w_out, da_lambda_q1, da_lambda_k1, da_lambda_q2, da_lambda_k2, da_q_norm, da_k_norm, da_subln, mla_q_a_norm, mla_w_q_up, mla_kv_a_norm, mla_w_kv_up, mla_q_norm, mla_k_norm, c_w_in, c_w_out, gq_q_norm, gq_k_norm, gq_sink):
    pb, ps, d = x_prompt.shape
    sb, ss, _ = x_sample.shape
    past = cache_da_k.shape[2]
    assert sb + 1 <= 8 and d == D_MODEL

    cond8 = jnp.concatenate([c_ctx[None], c, jnp.zeros((8 - 1 - sb, d), F32)], axis=0)
    wq = jnp.pad(mla_w_q_up[0].reshape(MLA_RANK, MLA_HEADS, MLA_QK),
                 ((0, 0), (0, 0), (0, MLA_SLAB - MLA_QK))).reshape(MLA_RANK, MLA_HEADS * MLA_SLAB)
    wkv3 = mla_w_kv_up[0].reshape(MLA_RANK, MLA_HEADS, MLA_NOPE + MLA_V)
    wkv = jnp.concatenate([wkv3[..., :MLA_NOPE].reshape(MLA_RANK, -1),
                           wkv3[..., MLA_NOPE:].reshape(MLA_RANK, -1)], axis=1)
    P = {
        "norm1_g": norm1_g, "norm2_g": norm2_g,
        "w_in0": ab_w_in[0, :, :AB_MAIN].astype(BF16),
        "w_kr0": jnp.pad(ab_w_in[0, :, AB_MAIN:], ((0, 0), (0, LANES - MLA_ROPE))).astype(BF16),
        "ff1_f32": ff1_w, "ff2_f32": ff2_w,
        "w_out0_f32": ab_w_out, "w_in1_f32": c_w_in, "w_out1_f32": c_w_out,
        "da_qn": jnp.tile(da_q_norm[0], 2).reshape(1, HEAD_W),
        "da_kn": jnp.tile(da_k_norm[0], 2).reshape(1, HEAD_W),
        "mq_norm": mla_q_a_norm[0].reshape(1, MLA_RANK),
        "mkv_norm": mla_kv_a_norm[0].reshape(1, MLA_RANK),
        "wq": wq.astype(BF16), "wkv": wkv.astype(BF16),
        "gq": jnp.pad(mla_q_norm[0], (0, MLA_SLAB - MLA_QK)).reshape(1, MLA_SLAB),
        "gkn": mla_k_norm[0, :MLA_NOPE].reshape(1, LANES),
        "gkr": jnp.pad(mla_k_norm[0, MLA_NOPE:], (0, LANES - MLA_ROPE)).reshape(1, LANES),
        "lam4": jnp.stack([da_lambda_q1[0], da_lambda_k1[0], da_lambda_q2[0], da_lambda_k2[0]]),
        "gsub": da_subln[0].reshape(1, HEAD_W),
        "lam_init": 0.8 - 0.6 * math.exp(-0.3 * 0),
        "gq_qn": gq_q_norm[0].reshape(1, GQ_DIM), "gq_kn": gq_k_norm[0].reshape(1, GQ_DIM),
        "sink": jnp.broadcast_to(gq_sink[0].reshape(GQ_HEADS, 1), (GQ_HEADS, LANES)),
        "tabs_ab": (_rope_tables(ss, [(32, "row"), (32, "col"), (32, "row"), (32, "col")])
                    + _rope_tables(ss, [(32, "row"), (32, "col"), (32, "none"), (32, "none")])),
        "tabs_gq": _rope_tables(ss, [(64, "row"), (64, "col")]),
    }

    mod = _ada_mod(cond8, ada_w, ada_b).reshape(2, 8, 1, 6 * d)

    kr_ctx = jnp.pad(cache_mla_krope[:, 0].reshape(sb * past, MLA_ROPE), ((0, 0), (0, LANES - MLA_ROPE)))
    mla_k_ctx, mla_v_ctx = _ctx_mla(cache_mla_ckv[:, 0].reshape(sb * past, MLA_RANK), kr_ctx, P)
    ctx = {
        "past": past,
        "da_k": cache_da_k[:, 0].reshape(sb * past, -1), "da_v": cache_da_v[:, 0].reshape(sb * past, -1),
        "mla_k": mla_k_ctx, "mla_v": mla_v_ctx,
        "gq_k": cache_gq_k[:, 0].reshape(sb * past, -1), "gq_v": cache_gq_v[:, 0].reshape(sb * past, -1),
    }

    prompt = Trunk(groups=1, seq=ps, batch=pb, mod_row0=0, rope=False)
    sample = Trunk(groups=sb, seq=ss, batch=sb, mod_row0=1, rope=True)
    y_p, (new_da_k, new_da_v, new_ckv, new_kr), (new_gq_k, new_gq_v), ffw = _run_trunk(
        x_prompt.reshape(pb * ps, d), prompt, "prompt", mod, P, None, None)
    y_s, _, _, _ = _run_trunk(x_sample.reshape(sb * ss, d), sample, "sample", mod, P, ctx, ffw)

    return (y_p.reshape(pb, ps, d), y_s.reshape(sb, ss, d),
            new_da_k.reshape(pb, 1, ps, DA_HEADS, HEAD_W), new_da_v.reshape(pb, 1, ps, DA_HEADS, HEAD_W),
            new_ckv.reshape(pb, 1, ps, MLA_RANK), new_kr.reshape(pb, 1, ps, MLA_ROPE),
            new_gq_k.reshape(pb, 1, ps, GQ_KV_HEADS, GQ_DIM), new_gq_v.reshape(pb, 1, ps, GQ_KV_HEADS, GQ_DIM))
```

```python
import functools
import math
from typing import NamedTuple

import jax
import jax.numpy as jnp
import numpy as np
from jax import lax
from jax.experimental import pallas as pl
from jax.experimental.pallas import tpu as pltpu

F32 = jnp.float32
BF16 = jnp.bfloat16

D_MODEL = 2048
GRID_W = 64
ROPE_BASE = 10000.0
EPS = 1e-6
NEG_INF = -1e30
LOG2E = math.log2(math.e)
DA_HEADS = 8
DA_QK_DIM = 64
DA_W = DA_HEADS * 2 * DA_QK_DIM
MLA_HEADS = 8
MLA_RANK = 512
MLA_NOPE = 128
MLA_ROPE = 64
MLA_V = 128
MLA_QK = MLA_NOPE + MLA_ROPE
MLA_SLAB = 256
GQ_HEADS = 16
GQ_KV_HEADS = 4
GQ_GROUP = GQ_HEADS // GQ_KV_HEADS
GQ_DIM = 128
WINDOW = 128
HEAD_W = 128
AB_MAIN = 3 * DA_W + 2 * MLA_RANK

LANES = 128
MXU_N = 256
VMEM_CAP_BYTES = 60 * 1024 * 1024


class Trunk(NamedTuple):
    groups: int
    seq: int
    batch: int
    mod_row0: int
    rope: bool

    @property
    def rows(self):
        return self.batch * self.seq


def _vmem_limit(block_bytes, scratch_bytes=0, temp_bytes=0):
    est = 2 * block_bytes + scratch_bytes + temp_bytes
    assert est <= 2 * VMEM_CAP_BYTES, est
    return VMEM_CAP_BYTES


def _nbytes(shape, dtype):
    return math.prod(shape) * jnp.dtype(dtype).itemsize


def _params(sem, vmem):
    return pltpu.CompilerParams(dimension_semantics=sem, vmem_limit_bytes=vmem)


def _resident(shape):
    return pl.BlockSpec(shape, lambda *_: (0,) * len(shape), pipeline_mode=pl.Buffered(1))


def _const(shape):
    return pl.BlockSpec(shape, lambda *_: (0,) * len(shape))


def _add_side_casts(kernel, casts, steps, step_of, in_specs, args, out_specs, out_shape):
    n_in, n_out, n_jobs = len(args), len(out_specs), len(casts)
    extra = 0
    for w, layer in casts:
        _, rows, cols = w.shape
        blk = rows // steps
        in_specs.append(pl.BlockSpec((None, blk, cols), lambda *ids, l=layer: (l, step_of(*ids), 0)))
        args.append(w)
        out_specs.append(pl.BlockSpec((blk, cols), lambda *ids: (step_of(*ids), 0)))
        out_shape.append(jax.ShapeDtypeStruct((rows, cols), BF16))
        extra += _nbytes((blk, cols), F32) + _nbytes((blk, cols), BF16)

    def wrapped(*refs):
        srcs = refs[n_in:n_in + n_jobs]
        dsts = refs[n_in + n_jobs + n_out:n_in + 2 * n_jobs + n_out]
        for src, dst in zip(srcs, dsts):
            dst[...] = src[...].astype(BF16)
        kernel(*refs[:n_in], *refs[n_in + n_jobs:n_in + n_jobs + n_out], *refs[n_in + 2 * n_jobs + n_out:])

    return (wrapped if casts else kernel), extra


def _ada_kernel(c_ref, w_ref, b_ref, o_ref):
    c = c_ref[...]
    s = (c / (1.0 + jnp.exp(-c))).astype(BF16)
    o_ref[0] = jnp.dot(s, w_ref[0].astype(BF16), preferred_element_type=F32) + b_ref[0]


def _ada_mod(cond8, ada_w, ada_b):
    depth, d, n = ada_w.shape
    tn = 1024
    blocks = _nbytes((d, tn), F32) + _nbytes((8, d), F32) + _nbytes((8, tn), F32)
    return pl.pallas_call(
        _ada_kernel,
        grid=(depth, n // tn),
        in_specs=[pl.BlockSpec((8, d), lambda l, j: (0, 0)),
                  pl.BlockSpec((1, d, tn), lambda l, j: (l, 0, j)),
                  pl.BlockSpec((1, 1, tn), lambda l, j: (l, 0, j))],
        out_specs=pl.BlockSpec((1, 8, tn), lambda l, j: (l, 0, j)),
        out_shape=jax.ShapeDtypeStruct((depth, 8, n), F32),
        compiler_params=_params(("arbitrary", "arbitrary"),
                                _vmem_limit(blocks, temp_bytes=_nbytes((d, tn), BF16))),
        name="ada_mod",
    )(cond8, ada_w, ada_b.reshape(depth, 1, n))


def _mod_spec(layer, chunk, tm, trunk):
    per = trunk.rows // trunk.groups // tm

    def idx(i, *_):
        return (layer, trunk.mod_row0 + i // per, 0, chunk)

    return pl.BlockSpec((1, 1, 1, D_MODEL), idx)


def _normmod(x, g, sc, sh):
    ms = jnp.mean(x * x, axis=-1, keepdims=True)
    y = x * lax.rsqrt(ms + EPS) * g
    return y * (1.0 + sc) + sh


def _rms(x, g):
    return x * lax.rsqrt(jnp.mean(x * x, axis=-1, keepdims=True) + EPS) * g


def _outproj_kernel(n_in, n_chunks, x_ref, gate_ref, g2_ref, sh2_ref, sc2_ref, *refs):
    o_ref, h_ref = refs[-2], refs[-1]
    rc = x_ref.shape[0] // n_chunks
    for c in range(n_chunks):
        rows = slice(c * rc, (c + 1) * rc)
        acc = None
        for k in range(n_in):
            part = jnp.dot(refs[2 * k][rows], refs[2 * k + 1][...], preferred_element_type=F32)
            acc = part if acc is None else acc + part
        x1 = x_ref[rows] + gate_ref[0, 0] * acc
        o_ref[rows] = x1
        h_ref[rows] = _normmod(x1, g2_ref[...], sc2_ref[0, 0], sh2_ref[0, 0]).astype(BF16)


def _outproj(x, norm2_g, mod, layer, pairs, trunk, tag):
    t, d = x.shape
    tm, n_chunks = 512, 2
    in_specs = [pl.BlockSpec((tm, d), lambda i: (i, 0)),
                _mod_spec(layer, 2, tm, trunk), _const((1, d)),
                _mod_spec(layer, 3, tm, trunk), _mod_spec(layer, 4, tm, trunk)]
    args = [x, mod, norm2_g.reshape(1, d), mod, mod]
    blocks = 2 * _nbytes((tm, d), F32) + _nbytes((tm, d), BF16)
    for o, w, blk in pairs:
        k = o.shape[1]
        in_specs += [pl.BlockSpec((tm, k), lambda i: (i, 0)),
                     pl.BlockSpec((k, d), lambda i, blk=blk: (blk, 0))]
        args += [o, w]
        blocks += _nbytes((tm, k), BF16) + _nbytes((k, d), BF16)
    return pl.pallas_call(
        functools.partial(_outproj_kernel, len(pairs), n_chunks),
        grid=(t // tm,),
        in_specs=in_specs,
        out_specs=[pl.BlockSpec((tm, d), lambda i: (i, 0)), pl.BlockSpec((tm, d), lambda i: (i, 0))],
        out_shape=[jax.ShapeDtypeStruct((t, d), F32), jax.ShapeDtypeStruct((t, d), BF16)],
        compiler_params=_params(("arbitrary",), _vmem_limit(blocks, 0, 2 * _nbytes((tm, d), F32))),
        name=f"outproj_l{layer}_{tag}",
    )(*args)


def _ffn_kernel(x_ref, h_ref, gate_ref, w1_ref, w2_ref, o_ref, acc_ref):
    f = pl.program_id(1)

    @pl.when(f == 0)
    def _():
        acc_ref[...] = jnp.zeros_like(acc_ref)

    a = jnp.dot(h_ref[...], w1_ref[...], preferred_element_type=F32)
    a = jnp.square(jnp.maximum(a, 0.0)).astype(BF16)
    acc_ref[...] += jnp.dot(a, w2_ref[...], preferred_element_type=F32)

    @pl.when(f == pl.num_programs(1) - 1)
    def _():
        o_ref[...] = x_ref[...] + gate_ref[0, 0] * acc_ref[...]


def _ffn(x, h, mod, layer, w1, w2, trunk, tag):
    t, d = x.shape
    ff = w1.shape[1]
    tm, tf = 512, 1024
    blocks = (2 * _nbytes((tm, d), F32) + _nbytes((tm, d), BF16) + _nbytes((d, tf), BF16)
              + _nbytes((tf, d), BF16) + _nbytes((1, d), F32))
    return pl.pallas_call(
        _ffn_kernel,
        grid=(t // tm, ff // tf),
        in_specs=[pl.BlockSpec((tm, d), lambda i, f: (i, 0)),
                  pl.BlockSpec((tm, d), lambda i, f: (i, 0)),
                  _mod_spec(layer, 5, tm, trunk),
                  pl.BlockSpec((d, tf), lambda i, f: (0, f)),
                  pl.BlockSpec((tf, d), lambda i, f: (f, 0))],
        out_specs=pl.BlockSpec((tm, d), lambda i, f: (i, 0)),
        out_shape=jax.ShapeDtypeStruct((t, d), F32),
        scratch_shapes=[pltpu.VMEM((tm, d), F32)],
        compiler_params=_params(("arbitrary", "arbitrary"),
                                _vmem_limit(blocks, _nbytes((tm, d), F32),
                                            _nbytes((tm, tf), F32) * 2 + _nbytes((tm, d), F32))),
        name=f"ffn_l{layer}_{tag}",
    )(x, h, mod, w1, w2)


def _rope_tables(seq, pattern):
    pos_row = (np.arange(seq) // GRID_W).astype(np.float64)
    pos_col = (np.arange(seq) % GRID_W).astype(np.float64)
    cos_cols, sa_cols, sb_cols = [], [], []
    for width, kind in pattern:
        if kind == "none":
            cos_cols.append(np.ones((seq, width)))
            sa_cols.append(np.zeros((seq, width)))
            sb_cols.append(np.zeros((seq, width)))
            continue
        half = width // 2
        inv = ROPE_BASE ** (-np.arange(half, dtype=np.float64) / half)
        pos = pos_row if kind == "row" else pos_col
        ang = pos[:, None] * inv
        cos, sin = np.cos(ang), np.sin(ang)
        zero = np.zeros_like(sin)
        cos_cols += [cos, cos]
        sa_cols += [-sin, zero]
        sb_cols += [zero, sin]
    tabs = [jnp.asarray(np.concatenate(c, axis=1), F32) for c in (cos_cols, sa_cols, sb_cols)]
    assert tabs[0].shape == (seq, LANES)
    return tabs


def _rope(x, tabs, half):
    cos, sin_a, sin_b = tabs
    return (x * cos + pltpu.roll(x, LANES - half, 1) * sin_a + pltpu.roll(x, half, 1) * sin_b)


def _mla_keys(kv_dot, kr, krg, gkn_ref, km_ref, vm_ref, rows):
    ss_kr = jnp.sum(kr * kr, axis=-1, keepdims=True)
    for pair in range(MLA_HEADS // 2):
        kn2 = kv_dot(pair * MXU_N)
        for s in range(2):
            h = 2 * pair + s
            kn = kn2[:, s * LANES:(s + 1) * LANES]
            r = lax.rsqrt((jnp.sum(kn * kn, axis=-1, keepdims=True) + ss_kr) / MLA_QK + EPS)
            km_ref[rows, h * MLA_SLAB:h * MLA_SLAB + LANES] = (kn * r * gkn_ref[...]).astype(BF16)
            km_ref[rows, h * MLA_SLAB + LANES:(h + 1) * MLA_SLAB] = (krg * r).astype(BF16)
    for c in range(MLA_HEADS * MLA_V // MXU_N):
        v = kv_dot(MLA_HEADS * MLA_NOPE + c * MXU_N)
        vm_ref[rows, c * MXU_N:(c + 1) * MXU_N] = v.astype(BF16)


def _norm_ahead(x0_ref, xn_ref, g_ref, sh0_ref, sc0_ref, shn_ref, scn_ref, h_ref, hn_ref):
    @pl.when(pl.program_id(0) == 0)
    def _():
        h_ref[...] = _normmod(x0_ref[...], g_ref[...], sc0_ref[0, 0], sh0_ref[0, 0]).astype(BF16)

    hn_ref[...] = _normmod(xn_ref[...], g_ref[...], scn_ref[0, 0], shn_ref[0, 0]).astype(BF16)


def _norm_ahead_specs(layer, tm, trunk, d):
    n_tiles = trunk.rows // tm
    per = trunk.rows // trunk.groups // tm
    nxt = lambda i: jnp.minimum(i + 1, n_tiles - 1)

    def mod(chunk, tile_of):
        return pl.BlockSpec((1, 1, 1, d), lambda i: (layer, trunk.mod_row0 + tile_of(i) // per, 0, chunk))

    return [pl.BlockSpec((tm, d), lambda i: (0, 0), pipeline_mode=pl.Buffered(1)),
            pl.BlockSpec((tm, d), lambda i: (nxt(i), 0)), _const((1, d)),
            mod(0, lambda i: 0), mod(1, lambda i: 0), mod(0, nxt), mod(1, nxt)]


def _front_ab_kernel(rope, n_chunks, x0_ref, xn_ref, g_ref, sh0_ref, sc0_ref, shn_ref, scn_ref, w_ref,
                     qn_ref, kn_ref, mqn_ref, wq_ref, mkvn_ref, wkv_ref, gq_ref, gkn_ref, gkr_ref, *refs):
    if rope:
        tab_refs, outs, (h_ref, hn_ref) = refs[:6], refs[6:12], refs[12:]
    else:
        tab_refs, outs, (h_ref, hn_ref) = (), refs[:10], refs[10:]
    _norm_ahead(x0_ref, xn_ref, g_ref, sh0_ref, sc0_ref, shn_ref, scn_ref, h_ref, hn_ref)
    qda_ref, kda_ref, vda_ref, qm_ref, km_ref, vm_ref = outs[:6]
    rc = h_ref.shape[0] // n_chunks
    lo = lax.broadcasted_iota(jnp.int32, (1, HEAD_W), 1) < DA_QK_DIM
    qn_c = qn_ref[...] * (DA_QK_DIM ** -0.5 * LOG2E)
    gq_c = gq_ref[...] * (MLA_QK ** -0.5 * LOG2E)

    def da_norm(x, g):
        sq = x * x
        s_lo = jnp.sum(jnp.where(lo, sq, 0.0), axis=-1, keepdims=True)
        s_hi = jnp.sum(jnp.where(lo, 0.0, sq), axis=-1, keepdims=True)
        r = jnp.where(lo, lax.rsqrt(s_lo / DA_QK_DIM + EPS), lax.rsqrt(s_hi / DA_QK_DIM + EPS))
        return x * r * g

    for c in range(n_chunks):
        rows = slice(c * rc, (c + 1) * rc)

        def hdot(c0, width=MXU_N):
            return jnp.dot(h_ref[rows], w_ref[:, c0:c0 + width], preferred_element_type=F32)

        if rope:
            tabs_da = [t[rows] for t in tab_refs[:3]]
            tabs_mla = [t[rows] for t in tab_refs[3:]]

        def da_pairs(pairs):
            for pair in pairs:
                pq = hdot(pair * MXU_N)
                pk = hdot(DA_W + pair * MXU_N)
                for s in range(2):
                    sl = slice((2 * pair + s) * HEAD_W, (2 * pair + s + 1) * HEAD_W)
                    q = da_norm(pq[:, s * HEAD_W:(s + 1) * HEAD_W], qn_c)
                    k = da_norm(pk[:, s * HEAD_W:(s + 1) * HEAD_W], kn_ref[...])
                    if rope:
                        q = _rope(q, tabs_da, DA_QK_DIM // 4)
                        k = _rope(k, tabs_da, DA_QK_DIM // 4)
                    else:
                        outs[6][rows, sl] = k
                    qda_ref[rows, sl] = q.astype(BF16)
                    kda_ref[rows, sl] = k.astype(BF16)

        def mla_q_heads(mq, heads):
            for h in heads:
                qf = jnp.dot(mq, wq_ref[:, h * MLA_SLAB:(h + 1) * MLA_SLAB], preferred_element_type=F32)
                a, b = qf[:, :LANES], qf[:, LANES:]
                ss = jnp.sum(a * a, axis=-1, keepdims=True) + jnp.sum(b * b, axis=-1, keepdims=True)
                r = lax.rsqrt(ss / MLA_QK + EPS)
                a = a * r * gq_c[:, :LANES]
                b = b * r * gq_c[:, LANES:]
                if rope:
                    b = _rope(b, tabs_mla, MLA_ROPE // 4)
                qm_ref[rows, h * MLA_SLAB:h * MLA_SLAB + LANES] = a.astype(BF16)
                qm_ref[rows, h * MLA_SLAB + LANES:(h + 1) * MLA_SLAB] = b.astype(BF16)

        mq = _rms(hdot(3 * DA_W, MLA_RANK), mqn_ref[...]).astype(BF16)
        ckv = _rms(hdot(3 * DA_W + MLA_RANK, MLA_RANK), mkvn_ref[...])
        kr = hdot(AB_MAIN, LANES)
        if not rope:
            outs[8][rows] = ckv
            outs[9][rows] = kr[:, :MLA_ROPE]
        ckv_b = ckv.astype(BF16)
        da_pairs(range(0, DA_HEADS // 4))
        mla_q_heads(mq, range(0, MLA_HEADS // 2))
        da_pairs(range(DA_HEADS // 4, DA_HEADS // 2))
        mla_q_heads(mq, range(MLA_HEADS // 2, MLA_HEADS))
        for cc in range(DA_W // MXU_N):
            dv = hdot(2 * DA_W + cc * MXU_N)
            vda_ref[rows, cc * MXU_N:(cc + 1) * MXU_N] = dv.astype(BF16)
            if not rope:
                outs[7][rows, cc * MXU_N:(cc + 1) * MXU_N] = dv
        krg = kr * gkr_ref[...]
        if rope:
            krg = _rope(krg, tabs_mla, MLA_ROPE // 4)

        def kv_dot(c0):
            return jnp.dot(ckv_b, wkv_ref[:, c0:c0 + MXU_N], preferred_element_type=F32)

        _mla_keys(kv_dot, kr, krg, gkn_ref, km_ref, vm_ref, rows)

    h_ref[...] = hn_ref[...]


def _front_ab(x, norm_g, mod, trunk, w, tabs, tag, casts=()):
    tm, n_chunks = 256, 1
    rows, rope, d = trunk.rows, trunk.rope, D_MODEL
    in_specs = _norm_ahead_specs(0, tm, trunk, d) + [
                _resident((d, AB_MAIN + LANES)),
                _const((1, HEAD_W)), _const((1, HEAD_W)), _const((1, MLA_RANK)),
                _resident((MLA_RANK, MLA_HEADS * MLA_SLAB)), _const((1, MLA_RANK)),
                _resident((MLA_RANK, MLA_HEADS * (MLA_NOPE + MLA_V))),
                _const((1, MLA_SLAB)), _const((1, LANES)), _const((1, LANES))]
    args = [x, x, norm_g.reshape(1, d), mod, mod, mod, mod, w["w_in0"], w["da_qn"], w["da_kn"],
            w["mq_norm"], w["wq"], w["mkv_norm"], w["wkv"], w["gq"], w["gkn"], w["gkr"]]
    if rope:
        per = trunk.seq // tm
        in_specs += [pl.BlockSpec((tm, LANES), lambda i: (i % per, 0))] * 6
        args += list(tabs)
    row = lambda n: pl.BlockSpec((tm, n), lambda i: (i, 0))
    widths = (DA_W, DA_W, DA_W, MLA_HEADS * MLA_SLAB, MLA_HEADS * MLA_SLAB, MLA_HEADS * MLA_V)
    out_specs = [row(n) for n in widths]
    out_shape = [jax.ShapeDtypeStruct((rows, n), BF16) for n in widths]
    out_bytes = sum(_nbytes((tm, n), BF16) for n in widths)
    if not rope:
        cache_w = (DA_W, DA_W, MLA_RANK, MLA_ROPE)
        out_specs += [row(n) for n in cache_w]
        out_shape += [jax.ShapeDtypeStruct((rows, n), F32) for n in cache_w]
        out_bytes += sum(_nbytes((tm, n), F32) for n in cache_w)
    resident = _nbytes((d, AB_MAIN + LANES), BF16) + 2 * _nbytes((MLA_RANK, 2048), BF16)
    blocks = _nbytes((tm, d), F32) + out_bytes + 6 * _nbytes((tm, LANES), F32)
    body, c_bytes = _add_side_casts(functools.partial(_front_ab_kernel, rope, n_chunks), casts,
                                    rows // tm, lambda i: i, in_specs, args, out_specs, out_shape)
    blocks += c_bytes
    return pl.pallas_call(
        body,
        grid=(rows // tm,),
        in_specs=in_specs,
        out_specs=out_specs,
        out_shape=out_shape,
        scratch_shapes=[pltpu.VMEM((tm, d), BF16), pltpu.VMEM((tm, d), BF16)],
        compiler_params=_params(("arbitrary",),
                                _vmem_limit(blocks, resident + 2 * _nbytes((tm, d), BF16),
                                            4 * _nbytes((tm, d), F32))),
        name=f"front_ab_{tag}",
    )(*args)


def _ctx_mla_kernel(ckv_ref, kr_ref, wkv_ref, gkn_ref, gkr_ref, km_ref, vm_ref):
    ckv_b = ckv_ref[...].astype(BF16)
    kr = kr_ref[...]

    def kv_dot(c0):
        return jnp.dot(ckv_b, wkv_ref[:, c0:c0 + MXU_N], preferred_element_type=F32)

    _mla_keys(kv_dot, kr, kr * gkr_ref[...], gkn_ref, km_ref, vm_ref, slice(None))


def _ctx_mla(ckv, kr128, w):
    rows = ckv.shape[0]
    tm = 256
    blocks = (_nbytes((tm, MLA_RANK + LANES), F32) + _nbytes((MLA_RANK, 2048), BF16)
              + _nbytes((tm, 3072), BF16))
    return pl.pallas_call(
        _ctx_mla_kernel,
        grid=(rows // tm,),
        in_specs=[pl.BlockSpec((tm, MLA_RANK), lambda i: (i, 0)),
                  pl.BlockSpec((tm, LANES), lambda i: (i, 0)),
                  _const((MLA_RANK, 2048)), _const((1, LANES)), _const((1, LANES))],
        out_specs=[pl.BlockSpec((tm, 2048), lambda i: (i, 0)),
                   pl.BlockSpec((tm, 1024), lambda i: (i, 0))],
        out_shape=[jax.ShapeDtypeStruct((rows, 2048), BF16),
                   jax.ShapeDtypeStruct((rows, 1024), BF16)],
        compiler_params=_params(("arbitrary",),
                                _vmem_limit(blocks, 0, 2 * _nbytes((tm, 2048), F32))),
        name="ctx_mla",
    )(ckv, kr128, w["wkv"], w["gkn"], w["gkr"])


def _front_c_kernel(rope, n_chunks, x0_ref, xn_ref, g_ref, sh0_ref, sc0_ref, shn_ref, scn_ref, w_ref,
                    qn_ref, kn_ref, *refs):
    if rope:
        tab_refs, outs, (h_ref, hn_ref) = refs[:3], refs[3:6], refs[6:]
    else:
        tab_refs, outs, (h_ref, hn_ref) = (), refs[:5], refs[5:]
    _norm_ahead(x0_ref, xn_ref, g_ref, sh0_ref, sc0_ref, shn_ref, scn_ref, h_ref, hn_ref)
    q_ref, k_ref, v_ref = outs[:3]
    rc = h_ref.shape[0] // n_chunks
    nq = GQ_HEADS * GQ_DIM
    nk = GQ_KV_HEADS * GQ_DIM
    qn_c = qn_ref[...] * (GQ_DIM ** -0.5 * LOG2E)

    def cache_rows(c, head):
        return pl.ds(c * rc * GQ_KV_HEADS + head, rc, stride=GQ_KV_HEADS)

    for c in range(n_chunks):
        rows = slice(c * rc, (c + 1) * rc)

        def hdot(c0):
            return jnp.dot(h_ref[rows], w_ref[:, c0:c0 + MXU_N], preferred_element_type=F32)

        if rope:
            tabs = [t[rows] for t in tab_refs]
        for pair in range(GQ_HEADS // 2):
            pq = hdot(pair * MXU_N)
            for s in range(2):
                sl = slice((2 * pair + s) * GQ_DIM, (2 * pair + s + 1) * GQ_DIM)
                q = _rms(pq[:, s * GQ_DIM:(s + 1) * GQ_DIM], qn_c)
                if rope:
                    q = _rope(q, tabs, GQ_DIM // 4)
                q_ref[rows, sl] = q.astype(BF16)
        for pair in range(GQ_KV_HEADS // 2):
            pk = hdot(nq + pair * MXU_N)
            for s in range(2):
                sl = slice((2 * pair + s) * GQ_DIM, (2 * pair + s + 1) * GQ_DIM)
                k = _rms(pk[:, s * GQ_DIM:(s + 1) * GQ_DIM], kn_ref[...])
                if rope:
                    k = _rope(k, tabs, GQ_DIM // 4)
                else:
                    outs[3][cache_rows(c, 2 * pair + s), :] = k
                k_ref[rows, sl] = k.astype(BF16)
        for cc in range(nk // MXU_N):
            v = hdot(nq + nk + cc * MXU_N)
            v_ref[rows, cc * MXU_N:(cc + 1) * MXU_N] = v.astype(BF16)
            if not rope:
                for s in range(2):
                    outs[4][cache_rows(c, 2 * cc + s), :] = v[:, s * GQ_DIM:(s + 1) * GQ_DIM]

    h_ref[...] = hn_ref[...]


def _front_c(x, norm_g, mod, trunk, w_in, w, tabs, tag, casts=()):
    tm, n_chunks = 256, 1
    rows, rope, d = trunk.rows, trunk.rope, D_MODEL
    nq = GQ_HEADS * GQ_DIM
    nk = GQ_KV_HEADS * GQ_DIM
    n = nq + 2 * nk
    in_specs = _norm_ahead_specs(1, tm, trunk, d) + [
                _resident((d, n)), _const((1, GQ_DIM)), _const((1, GQ_DIM))]
    args = [x, x, norm_g.reshape(1, d), mod, mod, mod, mod, w_in, w["gq_qn"], w["gq_kn"]]
    if rope:
        per = trunk.seq // tm
        in_specs += [pl.BlockSpec((tm, LANES), lambda i: (i % per, 0))] * 3
        args += list(tabs)
    row = lambda wd: pl.BlockSpec((tm, wd), lambda i: (i, 0))
    out_specs = [row(nq), row(nk), row(nk)]
    out_shape = [jax.ShapeDtypeStruct((rows, wd), BF16) for wd in (nq, nk, nk)]
    out_bytes = _nbytes((tm, n), BF16)
    if not rope:
        out_specs += [pl.BlockSpec((tm * GQ_KV_HEADS, GQ_DIM), lambda i: (i, 0))] * 2
        out_shape += [jax.ShapeDtypeStruct((rows * GQ_KV_HEADS, GQ_DIM), F32)] * 2
        out_bytes += 2 * _nbytes((tm, nk), F32)
    blocks = _nbytes((tm, d), F32) + out_bytes + 3 * _nbytes((tm, LANES), F32)
    body, c_bytes = _add_side_casts(functools.partial(_front_c_kernel, rope, n_chunks), casts,
                                    rows // tm, lambda i: i, in_specs, args, out_specs, out_shape)
    blocks += c_bytes
    return pl.pallas_call(
        body,
        grid=(rows // tm,),
        in_specs=in_specs,
        out_specs=out_specs,
        out_shape=out_shape,
        scratch_shapes=[pltpu.VMEM((tm, d), BF16), pltpu.VMEM((tm, d), BF16)],
        compiler_params=_params(("arbitrary",),
                                _vmem_limit(blocks, _nbytes((d, n), BF16) + 2 * _nbytes((tm, d), BF16),
                                            4 * _nbytes((tm, d), F32))),
        name=f"front_c_{tag}",
    )(*args)


def _dot_nt(a, b):
    return lax.dot_general(a, b, (((1,), (1,)), ((), ())), preferred_element_type=F32)


def _attn_rows(q, parts):
    scores = [_dot_nt(q, k) for k, _ in parts]
    m = functools.reduce(jnp.maximum, [jnp.max(s, axis=-1, keepdims=True) for s in scores])
    o1 = None
    for s, (_, v) in zip(scores, parts):
        e = jnp.exp2(s - m).astype(BF16)
        v1 = jnp.concatenate([v, jnp.ones((v.shape[0], LANES), BF16)], axis=1)
        part = jnp.dot(e, v1, preferred_element_type=F32)
        o1 = part if o1 is None else o1 + part
    dv = parts[0][1].shape[1]
    return o1[:, :dv], o1[:, dv:]


ONES_ROWS = 16


def _attn_cols(q, parts, mask=None, sink=None):
    scores = [_dot_nt(k, q) for k, _ in parts]
    if mask is not None:
        scores[-1] = jnp.where(mask, scores[-1], NEG_INF)
    m = functools.reduce(jnp.maximum, [jnp.max(s, axis=0, keepdims=True) for s in scores])
    if sink is not None:
        sink2 = sink * LOG2E
        m = jnp.maximum(m, sink2)
    o1 = None
    for s, (_, v) in zip(scores, parts):
        e = jnp.exp2(s - m).astype(BF16)
        v_t1 = jnp.concatenate([v.astype(F32).T.astype(BF16),
                                jnp.ones((ONES_ROWS, v.shape[0]), BF16)], axis=0)
        part = jnp.dot(v_t1, e, preferred_element_type=F32)
        o1 = part if o1 is None else o1 + part
    dv = parts[0][1].shape[1]
    den = o1[dv:dv + 1]
    if sink is not None:
        den = den + jnp.exp2(sink2 - m)
    return o1[:dv], den


def _head_rows(ref, h, n_heads, sl):
    if ref.shape[1] == sl.stop - sl.start:
        return ref[pl.ds(h, ref.shape[0] // n_heads, stride=n_heads), :]
    return ref[:, sl]


def _kv_parts(refs, n_parts, h, n_heads, sl_k, sl_v):
    return [(_head_rows(refs[2 * p], h, n_heads, sl_k).astype(BF16),
             _head_rows(refs[2 * p + 1], h, n_heads, sl_v).astype(BF16)) for p in range(n_parts)]


def _da_attn_kernel(n_parts, lam_init, lam_ref, gsub_ref, q_ref, *refs):
    o_ref = refs[-1]
    tq = q_ref.shape[0]
    lv = lam_ref[...]
    lam = (jnp.exp(jnp.sum(lv[0:1] * lv[1:2], axis=-1, keepdims=True))
           - jnp.exp(jnp.sum(lv[2:3] * lv[3:4], axis=-1, keepdims=True)) + lam_init)
    lo = lax.broadcasted_iota(jnp.int32, (1, HEAD_W), 1) < DA_QK_DIM
    for h in range(DA_HEADS):
        sl = slice(h * HEAD_W, (h + 1) * HEAD_W)
        q = q_ref[:, sl]
        zero = jnp.zeros_like(q)
        q12 = jnp.concatenate([jnp.where(lo, q, zero), jnp.where(lo, zero, q)], axis=0)
        o12, d12 = _attn_rows(q12, _kv_parts(refs, n_parts, h, DA_HEADS, sl, sl))
        o12 = o12 * (1.0 / d12)
        o = o12[:tq] - lam * o12[tq:]
        y = o * lax.rsqrt(jnp.mean(o * o, axis=-1, keepdims=True) + EPS) * gsub_ref[...]
        o_ref[:, sl] = (y * (1.0 - lam_init)).astype(BF16)


def _mla_attn_kernel(n_parts, q_ref, *refs):
    o_ref = refs[-1]
    for h in range(MLA_HEADS):
        sl_k = slice(h * MLA_SLAB, (h + 1) * MLA_SLAB)
        sl_v = slice(h * MLA_V, (h + 1) * MLA_V)
        o, den = _attn_rows(q_ref[:, sl_k], _kv_parts(refs, n_parts, h, MLA_HEADS, sl_k, sl_v))
        o_ref[:, sl_v] = (o * (1.0 / den)).astype(BF16)


def _gq_attn_kernel(n_parts, seq, kw, sink_ref, q_ref, *refs):
    o_ref = refs[-1]
    tq = q_ref.shape[0]
    qi = pl.program_id(1)
    mask = None
    if n_parts == 2:
        start = pl.multiple_of(jnp.clip(qi * tq - WINDOW, 0, seq - kw), WINDOW)
        keys = start + lax.broadcasted_iota(jnp.int32, (kw, GQ_GROUP * tq), 0)
        qrows = qi * tq + (lax.broadcasted_iota(jnp.int32, (kw, GQ_GROUP * tq), 1) & (tq - 1))
        mask = jnp.abs(qrows - keys) <= WINDOW
    for g in range(GQ_KV_HEADS):
        sl = slice(g * GQ_DIM, (g + 1) * GQ_DIM)
        heads = range(g * GQ_GROUP, (g + 1) * GQ_GROUP)
        q4 = jnp.concatenate([q_ref[:, j * GQ_DIM:(j + 1) * GQ_DIM] for j in heads], axis=0)
        sink = jnp.concatenate([jnp.broadcast_to(sink_ref[j:j + 1, 0:1], (1, tq)) for j in heads], axis=1)
        if n_parts == 2:
            parts = [(_head_rows(refs[0], g, GQ_KV_HEADS, sl).astype(BF16),
                      _head_rows(refs[1], g, GQ_KV_HEADS, sl)),
                     (refs[2][pl.ds(start, kw), sl], refs[3][pl.ds(start, kw), sl])]
        else:
            parts = [(refs[0][:, sl], refs[1][:, sl])]
        o, den = _attn_cols(q4, parts, mask=mask, sink=sink)
        o = o * (1.0 / den)
        for n, j in enumerate(heads):
            o_ref[:, j * GQ_DIM:(j + 1) * GQ_DIM] = o[:, n * tq:(n + 1) * tq].T.astype(BF16)


def _per_sequence(kernel, sps, n_extra, n_parts):
    def wrapped(*refs):
        for s in range(sps):
            def rows(r):
                n = r.shape[0] // sps
                return r.at[pl.ds(s * n, n)]

            kernel(*refs[:n_extra], *[rows(r) for r in refs[n_extra:n_extra + 2 + 2 * n_parts]])

    return wrapped if sps > 1 else kernel


def _attention(kernel, name, q, kv_parts, trunk, tq, out_w, stacked_rows, extra_in=(), casts=(), sps=1):
    nq = trunk.seq // tq
    assert sps == 1 or nq == 1
    in_specs = [_const(a.shape) for a in extra_in]
    args = list(extra_in)
    qw = q.shape[1]
    in_specs.append(pl.BlockSpec((sps * tq, qw), lambda b, i: (b * nq + i, 0)))
    args.append(q)
    blocks = sps * (_nbytes((tq, qw), BF16) + _nbytes((tq, out_w), BF16))
    total_l = 0
    for k, v, l in kv_parts:
        in_specs += [pl.BlockSpec((sps * l, k.shape[1]), lambda b, i: (b, 0)),
                     pl.BlockSpec((sps * l, v.shape[1]), lambda b, i: (b, 0))]
        args += [k, v]
        blocks += sps * (_nbytes((l, k.shape[1]), k.dtype) + _nbytes((l, v.shape[1]), v.dtype))
        total_l += l
    out_specs = [pl.BlockSpec((sps * tq, out_w), lambda b, i: (b * nq + i, 0))]
    out_shape = [jax.ShapeDtypeStruct((trunk.rows, out_w), BF16)]
    kernel = _per_sequence(kernel, sps, len(extra_in), len(kv_parts))
    kernel, c_bytes = _add_side_casts(kernel, casts, trunk.batch // sps * nq, lambda b, i: b * nq + i,
                                      in_specs, args, out_specs, out_shape)
    blocks += c_bytes
    out = pl.pallas_call(
        kernel,
        grid=(trunk.batch // sps, nq),
        in_specs=in_specs,
        out_specs=out_specs,
        out_shape=out_shape,
        compiler_params=_params(("arbitrary",) * 2,
                                _vmem_limit(blocks, 0, 6 * _nbytes((stacked_rows, total_l), F32))),
        name=name,
    )(*args)
    return out if casts else out[0]


CAST_PLAN = {
    "front_ab": {"ff1_0": ("ff1_f32", 0), "w_out0": ("w_out0_f32", 0), "w_in1": ("w_in1_f32", 0)},
    "da_attn": {"ff2_0": ("ff2_f32", 0)},
    "mla_attn": {},
    "front_c": {"ff2_1": ("ff2_f32", 1), "w_out1": ("w_out1_f32", 0)},
    "gq_attn": {"ff1_1": ("ff1_f32", 1)},
}


def _run_trunk(x, trunk, tag, mod, P, ctx, wb):
    tq = min(trunk.seq, 256)
    sps = 4 if trunk.seq == tq and trunk.batch % 4 == 0 else 1
    casting = wb is None
    wb = dict(wb or {})

    def jobs(call):
        return [(P[src], layer) for src, layer in CAST_PLAN[call].values()] if casting else []

    def split(call, outs, n_main):
        if not casting or not CAST_PLAN[call]:
            return outs
        for name, w in zip(CAST_PLAN[call], outs[n_main:]):
            wb[name] = w
        return outs[:n_main] if n_main > 1 else outs[0]

    n_front = 6 if trunk.rope else 10
    front = split("front_ab", _front_ab(x, P["norm1_g"][0], mod, trunk, P,
                                        P["tabs_ab"] if trunk.rope else None, tag,
                                        casts=jobs("front_ab")), n_front)
    qda, kda, vda, qm, km, vm = front[:6]
    da_parts, mla_parts = [(kda, vda, trunk.seq)], [(km, vm, trunk.seq)]
    if ctx is not None:
        da_parts = [(ctx["da_k"], ctx["da_v"], ctx["past"] * DA_HEADS)] + da_parts
        mla_parts = [(ctx["mla_k"], ctx["mla_v"], ctx["past"])] + mla_parts
    o_da = split("da_attn", _attention(functools.partial(_da_attn_kernel, len(da_parts), P["lam_init"]),
                                       f"da_attn_{tag}", qda, da_parts, trunk, tq, DA_HEADS * HEAD_W,
                                       2 * tq, extra_in=(P["lam4"], P["gsub"]), casts=jobs("da_attn"),
                                       sps=sps), 1)
    o_m = split("mla_attn", _attention(functools.partial(_mla_attn_kernel, len(mla_parts)),
                                       f"mla_attn_{tag}", qm, mla_parts, trunk, tq, MLA_HEADS * MLA_V,
                                       tq, casts=jobs("mla_attn"), sps=sps), 1)
    x, h = _outproj(x, P["norm2_g"][0], mod, 0, [(o_da, wb["w_out0"], 0), (o_m, wb["w_out0"], 1)],
                    trunk, tag)
    x = _ffn(x, h, mod, 0, wb["ff1_0"], wb["ff2_0"], trunk, tag)

    n_front_c = 3 if trunk.rope else 5
    front_c = split("front_c", _front_c(x, P["norm1_g"][1], mod, trunk, wb["w_in1"], P,
                                        P["tabs_gq"] if trunk.rope else None, tag,
                                        casts=jobs("front_c")), n_front_c)
    qc, kc, vc = front_c[:3]
    gq_parts = [(kc, vc, trunk.seq)]
    if ctx is not None:
        gq_parts = [(ctx["gq_k"], ctx["gq_v"], ctx["past"] * GQ_KV_HEADS)] + gq_parts
    kw = min(trunk.seq, tq + 2 * WINDOW)
    o_c = split("gq_attn", _attention(functools.partial(_gq_attn_kernel, len(gq_parts), trunk.seq, kw),
                                      f"gq_attn_{tag}", qc, gq_parts, trunk, tq, GQ_HEADS * GQ_DIM,
                                      GQ_GROUP * tq, extra_in=(P["sink"],), casts=jobs("gq_attn"),
                                      sps=sps), 1)
    x, h = _outproj(x, P["norm2_g"][1], mod, 1, [(o_c, wb["w_out1"], 0)], trunk, tag)
    x = _ffn(x, h, mod, 1, wb["ff1_1"], wb["ff2_1"], trunk, tag)
    return x, front[6:], front_c[3:], wb


def kernel(x_prompt, x_sample, cache_da_k, cache_da_v, cache_mla_ckv, cache_mla_krope, cache_gq_k, cache_gq_v, c, c_ctx, norm1_g, norm2_g, ada_w, ada_b, ff1_w, ff2_w, ab_w_in, ab_w_out, da_lambda_q1, da_lambda_k1, da_lambda_q2, da_lambda_k2, da_q_norm, da_k_norm, da_subln, mla_q_a_norm, mla_w_q_up, mla_kv_a_norm, mla_w_kv_up, mla_q_norm, mla_k_norm, c_w_in, c_w_out, gq_q_norm, gq_k_norm, gq_sink):
    pb, ps, d = x_prompt.shape
    sb, ss, _ = x_sample.shape
    past = cache_da_k.shape[2]
    assert sb + 1 <= 8 and d == D_MODEL

    cond8 = jnp.concatenate([c_ctx[None], c, jnp.zeros((8 - 1 - sb, d), F32)], axis=0)
    wq = jnp.pad(mla_w_q_up[0].reshape(MLA_RANK, MLA_HEADS, MLA_QK),
                 ((0, 0), (0, 0), (0, MLA_SLAB - MLA_QK))).reshape(MLA_RANK, MLA_HEADS * MLA_SLAB)
    wkv3 = mla_w_kv_up[0].reshape(MLA_RANK, MLA_HEADS, MLA_NOPE + MLA_V)
    wkv = jnp.concatenate([wkv3[..., :MLA_NOPE].reshape(MLA_RANK, -1),
                           wkv3[..., MLA_NOPE:].reshape(MLA_RANK, -1)], axis=1)
    P = {
        "norm1_g": norm1_g, "norm2_g": norm2_g,
        "w_in0": jnp.pad(ab_w_in[0], ((0, 0), (0, LANES - MLA_ROPE))).astype(BF16),
        "ff1_f32": ff1_w, "ff2_f32": ff2_w,
        "w_out0_f32": ab_w_out, "w_in1_f32": c_w_in, "w_out1_f32": c_w_out,
        "da_qn": jnp.tile(da_q_norm[0], 2).reshape(1, HEAD_W),
        "da_kn": jnp.tile(da_k_norm[0], 2).reshape(1, HEAD_W),
        "mq_norm": mla_q_a_norm[0].reshape(1, MLA_RANK),
        "mkv_norm": mla_kv_a_norm[0].reshape(1, MLA_RANK),
        "wq": wq.astype(BF16), "wkv": wkv.astype(BF16),
        "gq": jnp.pad(mla_q_norm[0], (0, MLA_SLAB - MLA_QK)).reshape(1, MLA_SLAB),
        "gkn": mla_k_norm[0, :MLA_NOPE].reshape(1, LANES),
        "gkr": jnp.pad(mla_k_norm[0, MLA_NOPE:], (0, LANES - MLA_ROPE)).reshape(1, LANES),
        "lam4": jnp.stack([da_lambda_q1[0], da_lambda_k1[0], da_lambda_q2[0], da_lambda_k2[0]]),
        "gsub": da_subln[0].reshape(1, HEAD_W),
        "lam_init": 0.8 - 0.6 * math.exp(-0.3 * 0),
        "gq_qn": gq_q_norm[0].reshape(1, GQ_DIM), "gq_kn": gq_k_norm[0].reshape(1, GQ_DIM),
        "sink": jnp.broadcast_to(gq_sink[0].reshape(GQ_HEADS, 1), (GQ_HEADS, LANES)),
        "tabs_ab": (_rope_tables(ss, [(32, "row"), (32, "col"), (32, "row"), (32, "col")])
                    + _rope_tables(ss, [(32, "row"), (32, "col"), (32, "none"), (32, "none")])),
        "tabs_gq": _rope_tables(ss, [(64, "row"), (64, "col")]),
    }

    mod = _ada_mod(cond8, ada_w, ada_b).reshape(2, 8, 1, 6 * d)

    kr_ctx = jnp.pad(cache_mla_krope[:, 0].reshape(sb * past, MLA_ROPE), ((0, 0), (0, LANES - MLA_ROPE)))
    mla_k_ctx, mla_v_ctx = _ctx_mla(cache_mla_ckv[:, 0].reshape(sb * past, MLA_RANK), kr_ctx, P)
    ctx = {
        "past": past,
        "da_k": cache_da_k[:, 0].reshape(-1, HEAD_W), "da_v": cache_da_v[:, 0].reshape(-1, HEAD_W),
        "mla_k": mla_k_ctx, "mla_v": mla_v_ctx,
        "gq_k": cache_gq_k[:, 0].reshape(-1, GQ_DIM), "gq_v": cache_gq_v[:, 0].reshape(-1, GQ_DIM),
    }

    prompt = Trunk(groups=1, seq=ps, batch=pb, mod_row0=0, rope=False)
    sample = Trunk(groups=sb, seq=ss, batch=sb, mod_row0=1, rope=True)
    y_p, (new_da_k, new_da_v, new_ckv, new_kr), (new_gq_k, new_gq_v), ffw = _run_trunk(
        x_prompt.reshape(pb * ps, d), prompt, "prompt", mod, P, None, None)
    y_s, _, _, _ = _run_trunk(x_sample.reshape(sb * ss, d), sample, "sample", mod, P, ctx, ffw)

    return (y_p.reshape(pb, ps, d), y_s.reshape(sb, ss, d),
            new_da_k.reshape(pb, 1, ps, DA_HEADS, HEAD_W), new_da_v.reshape(pb, 1, ps, DA_HEADS, HEAD_W),
            new_ckv.reshape(pb, 1, ps, MLA_RANK), new_kr.reshape(pb, 1, ps, MLA_ROPE),
            new_gq_k.reshape(pb, 1, ps, GQ_KV_HEADS, GQ_DIM), new_gq_v.reshape(pb, 1, ps, GQ_KV_HEADS, GQ_DIM))
```

```python
import functools
import math
from typing import NamedTuple

import jax
import jax.numpy as jnp
import numpy as np
from jax import lax
from jax.experimental import pallas as pl
from jax.experimental.pallas import tpu as pltpu

F32 = jnp.float32
BF16 = jnp.bfloat16

D_MODEL = 2048
GRID_W = 64
ROPE_BASE = 10000.0
EPS = 1e-6
NEG_INF = -1e30
LOG2E = math.log2(math.e)
DA_HEADS = 8
DA_QK_DIM = 64
DA_W = DA_HEADS * 2 * DA_QK_DIM
MLA_HEADS = 8
MLA_RANK = 512
MLA_NOPE = 128
MLA_ROPE = 64
MLA_V = 128
MLA_QK = MLA_NOPE + MLA_ROPE
MLA_SLAB = 256
GQ_HEADS = 16
GQ_KV_HEADS = 4
GQ_GROUP = GQ_HEADS // GQ_KV_HEADS
GQ_DIM = 128
WINDOW = 128
HEAD_W = 128
AB_MAIN = 3 * DA_W + 2 * MLA_RANK

LANES = 128
MXU_N = 256
VMEM_CAP_BYTES = 60 * 1024 * 1024


class Trunk(NamedTuple):
    groups: int
    seq: int
    batch: int
    mod_row0: int
    rope: bool

    @property
    def rows(self):
        return self.batch * self.seq


def _vmem_limit(block_bytes, scratch_bytes=0, temp_bytes=0):
    est = 2 * block_bytes + scratch_bytes + temp_bytes
    assert est <= 2 * VMEM_CAP_BYTES, est
    return VMEM_CAP_BYTES


def _nbytes(shape, dtype):
    return math.prod(shape) * jnp.dtype(dtype).itemsize


def _params(sem, vmem):
    return pltpu.CompilerParams(dimension_semantics=sem, vmem_limit_bytes=vmem)


def _resident(shape):
    return pl.BlockSpec(shape, lambda *_: (0,) * len(shape), pipeline_mode=pl.Buffered(1))


def _const(shape):
    return pl.BlockSpec(shape, lambda *_: (0,) * len(shape))


def _add_side_casts(kernel, casts, steps, step_of, in_specs, args, out_specs, out_shape):
    n_in, n_out, n_jobs = len(args), len(out_specs), len(casts)
    extra = 0
    for w, layer in casts:
        _, rows, cols = w.shape
        blk = rows // steps
        in_specs.append(pl.BlockSpec((None, blk, cols), lambda *ids, l=layer: (l, step_of(*ids), 0)))
        args.append(w)
        out_specs.append(pl.BlockSpec((blk, cols), lambda *ids: (step_of(*ids), 0)))
        out_shape.append(jax.ShapeDtypeStruct((rows, cols), BF16))
        extra += _nbytes((blk, cols), F32) + _nbytes((blk, cols), BF16)

    def wrapped(*refs):
        srcs = refs[n_in:n_in + n_jobs]
        dsts = refs[n_in + n_jobs + n_out:n_in + 2 * n_jobs + n_out]
        for src, dst in zip(srcs, dsts):
            dst[...] = src[...].astype(BF16)
        kernel(*refs[:n_in], *refs[n_in + n_jobs:n_in + n_jobs + n_out], *refs[n_in + 2 * n_jobs + n_out:])

    return (wrapped if casts else kernel), extra


def _ada_kernel(c_ref, w_ref, b_ref, o_ref):
    c = c_ref[...]
    s = (c / (1.0 + jnp.exp(-c))).astype(BF16)
    o_ref[0] = jnp.dot(s, w_ref[0].astype(BF16), preferred_element_type=F32) + b_ref[0]


def _ada_mod(cond8, ada_w, ada_b):
    depth, d, n = ada_w.shape
    tn = 1024
    blocks = _nbytes((d, tn), F32) + _nbytes((8, d), F32) + _nbytes((8, tn), F32)
    return pl.pallas_call(
        _ada_kernel,
        grid=(depth, n // tn),
        in_specs=[pl.BlockSpec((8, d), lambda l, j: (0, 0)),
                  pl.BlockSpec((1, d, tn), lambda l, j: (l, 0, j)),
                  pl.BlockSpec((1, 1, tn), lambda l, j: (l, 0, j))],
        out_specs=pl.BlockSpec((1, 8, tn), lambda l, j: (l, 0, j)),
        out_shape=jax.ShapeDtypeStruct((depth, 8, n), F32),
        compiler_params=_params(("arbitrary", "arbitrary"),
                                _vmem_limit(blocks, temp_bytes=_nbytes((d, tn), BF16))),
        name="ada_mod",
    )(cond8, ada_w, ada_b.reshape(depth, 1, n))


def _mod_spec(layer, chunk, tm, trunk):
    per = trunk.rows // trunk.groups // tm

    def idx(i, *_):
        return (layer, trunk.mod_row0 + i // per, 0, chunk)

    return pl.BlockSpec((1, 1, 1, D_MODEL), idx)


def _normmod(x, g, sc, sh):
    ms = jnp.mean(x * x, axis=-1, keepdims=True)
    y = x * lax.rsqrt(ms + EPS) * g
    return y * (1.0 + sc) + sh


def _rms(x, g):
    return x * lax.rsqrt(jnp.mean(x * x, axis=-1, keepdims=True) + EPS) * g


def _outproj_kernel(n_in, n_chunks, x_ref, gate_ref, g2_ref, sh2_ref, sc2_ref, *refs):
    o_ref, h_ref = refs[-2], refs[-1]
    rc = x_ref.shape[0] // n_chunks
    for c in range(n_chunks):
        rows = slice(c * rc, (c + 1) * rc)
        acc = None
        for k in range(n_in):
            part = jnp.dot(refs[2 * k][rows], refs[2 * k + 1][...], preferred_element_type=F32)
            acc = part if acc is None else acc + part
        x1 = x_ref[rows] + gate_ref[0, 0] * acc
        o_ref[rows] = x1
        h_ref[rows] = _normmod(x1, g2_ref[...], sc2_ref[0, 0], sh2_ref[0, 0]).astype(BF16)


def _outproj(x, norm2_g, mod, layer, pairs, trunk, tag):
    t, d = x.shape
    tm, n_chunks = 512, 2
    in_specs = [pl.BlockSpec((tm, d), lambda i: (i, 0)),
                _mod_spec(layer, 2, tm, trunk), _const((1, d)),
                _mod_spec(layer, 3, tm, trunk), _mod_spec(layer, 4, tm, trunk)]
    args = [x, mod, norm2_g.reshape(1, d), mod, mod]
    blocks = 2 * _nbytes((tm, d), F32) + _nbytes((tm, d), BF16)
    for o, w, blk in pairs:
        k = o.shape[1]
        in_specs += [pl.BlockSpec((tm, k), lambda i: (i, 0)),
                     pl.BlockSpec((k, d), lambda i, blk=blk: (blk, 0))]
        args += [o, w]
        blocks += _nbytes((tm, k), BF16) + _nbytes((k, d), BF16)
    return pl.pallas_call(
        functools.partial(_outproj_kernel, len(pairs), n_chunks),
        grid=(t // tm,),
        in_specs=in_specs,
        out_specs=[pl.BlockSpec((tm, d), lambda i: (i, 0)), pl.BlockSpec((tm, d), lambda i: (i, 0))],
        out_shape=[jax.ShapeDtypeStruct((t, d), F32), jax.ShapeDtypeStruct((t, d), BF16)],
        compiler_params=_params(("arbitrary",), _vmem_limit(blocks, 0, 2 * _nbytes((tm, d), F32))),
        name=f"outproj_l{layer}_{tag}",
    )(*args)


def _ffn_kernel(x_ref, h_ref, gate_ref, w1_ref, w2_ref, o_ref, acc_ref):
    f = pl.program_id(1)

    def hidden_chunk():
        a = jnp.dot(h_ref[...], w1_ref[...], preferred_element_type=F32)
        a = jnp.square(jnp.maximum(a, 0.0)).astype(BF16)
        return jnp.dot(a, w2_ref[...], preferred_element_type=F32)

    @pl.when(f == 0)
    def _():
        acc_ref[...] = hidden_chunk()

    @pl.when(f > 0)
    def _():
        acc_ref[...] += hidden_chunk()

    @pl.when(f == pl.num_programs(1) - 1)
    def _():
        o_ref[...] = x_ref[...] + gate_ref[0, 0] * acc_ref[...]


def _ffn(x, h, mod, layer, w1, w2, trunk, tag):
    t, d = x.shape
    ff = w1.shape[1]
    tm, tf = 512, 1024
    blocks = (2 * _nbytes((tm, d), F32) + _nbytes((tm, d), BF16) + _nbytes((d, tf), BF16)
              + _nbytes((tf, d), BF16) + _nbytes((1, d), F32))
    return pl.pallas_call(
        _ffn_kernel,
        grid=(t // tm, ff // tf),
        in_specs=[pl.BlockSpec((tm, d), lambda i, f: (i, 0)),
                  pl.BlockSpec((tm, d), lambda i, f: (i, 0)),
                  _mod_spec(layer, 5, tm, trunk),
                  pl.BlockSpec((d, tf), lambda i, f: (0, f)),
                  pl.BlockSpec((tf, d), lambda i, f: (f, 0))],
        out_specs=pl.BlockSpec((tm, d), lambda i, f: (i, 0)),
        out_shape=jax.ShapeDtypeStruct((t, d), F32),
        scratch_shapes=[pltpu.VMEM((tm, d), F32)],
        compiler_params=_params(("arbitrary", "arbitrary"),
                                _vmem_limit(blocks, _nbytes((tm, d), F32),
                                            _nbytes((tm, tf), F32) * 2 + _nbytes((tm, d), F32))),
        name=f"ffn_l{layer}_{tag}",
    )(x, h, mod, w1, w2)


def _rope_tables(seq, pattern):
    pos_row = (np.arange(seq) // GRID_W).astype(np.float64)
    pos_col = (np.arange(seq) % GRID_W).astype(np.float64)
    cos_cols, sa_cols, sb_cols = [], [], []
    for width, kind in pattern:
        if kind == "none":
            cos_cols.append(np.ones((seq, width)))
            sa_cols.append(np.zeros((seq, width)))
            sb_cols.append(np.zeros((seq, width)))
            continue
        half = width // 2
        inv = ROPE_BASE ** (-np.arange(half, dtype=np.float64) / half)
        pos = pos_row if kind == "row" else pos_col
        ang = pos[:, None] * inv
        cos, sin = np.cos(ang), np.sin(ang)
        zero = np.zeros_like(sin)
        cos_cols += [cos, cos]
        sa_cols += [-sin, zero]
        sb_cols += [zero, sin]
    tabs = [jnp.asarray(np.concatenate(c, axis=1), F32) for c in (cos_cols, sa_cols, sb_cols)]
    assert tabs[0].shape == (seq, LANES)
    return tabs


def _rope(x, tabs, half):
    cos, sin_a, sin_b = tabs
    return (x * cos + pltpu.roll(x, LANES - half, 1) * sin_a + pltpu.roll(x, half, 1) * sin_b)


def _mla_keys(kv_dot, kr, krg, gkn_ref, km_ref, vm_ref, rows):
    ss_kr = jnp.sum(kr * kr, axis=-1, keepdims=True)
    for pair in range(MLA_HEADS // 2):
        kn2 = kv_dot(pair * MXU_N)
        for s in range(2):
            h = 2 * pair + s
            kn = kn2[:, s * LANES:(s + 1) * LANES]
            r = lax.rsqrt((jnp.sum(kn * kn, axis=-1, keepdims=True) + ss_kr) / MLA_QK + EPS)
            km_ref[rows, h * MLA_SLAB:h * MLA_SLAB + LANES] = (kn * r * gkn_ref[...]).astype(BF16)
            km_ref[rows, h * MLA_SLAB + LANES:(h + 1) * MLA_SLAB] = (krg * r).astype(BF16)
    for c in range(MLA_HEADS * MLA_V // MXU_N):
        v = kv_dot(MLA_HEADS * MLA_NOPE + c * MXU_N)
        vm_ref[rows, c * MXU_N:(c + 1) * MXU_N] = v.astype(BF16)


def _norm_ahead(x0_ref, xn_ref, g_ref, sh0_ref, sc0_ref, shn_ref, scn_ref, h_ref, hn_ref):
    @pl.when(pl.program_id(0) == 0)
    def _():
        h_ref[...] = _normmod(x0_ref[...], g_ref[...], sc0_ref[0, 0], sh0_ref[0, 0]).astype(BF16)

    hn_ref[...] = _normmod(xn_ref[...], g_ref[...], scn_ref[0, 0], shn_ref[0, 0]).astype(BF16)


def _norm_ahead_specs(layer, tm, trunk, d):
    n_tiles = trunk.rows // tm
    per = trunk.rows // trunk.groups // tm
    nxt = lambda i: jnp.minimum(i + 1, n_tiles - 1)

    def mod(chunk, tile_of):
        return pl.BlockSpec((1, 1, 1, d), lambda i: (layer, trunk.mod_row0 + tile_of(i) // per, 0, chunk))

    return [pl.BlockSpec((tm, d), lambda i: (0, 0), pipeline_mode=pl.Buffered(1)),
            pl.BlockSpec((tm, d), lambda i: (nxt(i), 0)), _const((1, d)),
            mod(0, lambda i: 0), mod(1, lambda i: 0), mod(0, nxt), mod(1, nxt)]


def _front_ab_kernel(rope, n_chunks, x0_ref, xn_ref, g_ref, sh0_ref, sc0_ref, shn_ref, scn_ref, w_ref,
                     qn_ref, kn_ref, mqn_ref, wq_ref, mkvn_ref, wkv_ref, gq_ref, gkn_ref, gkr_ref, *refs):
    if rope:
        tab_refs, outs, (h_ref, hn_ref) = refs[:6], refs[6:12], refs[12:]
    else:
        tab_refs, outs, (h_ref, hn_ref) = (), refs[:10], refs[10:]
    _norm_ahead(x0_ref, xn_ref, g_ref, sh0_ref, sc0_ref, shn_ref, scn_ref, h_ref, hn_ref)
    qda_ref, kda_ref, vda_ref, qm_ref, km_ref, vm_ref = outs[:6]
    rc = h_ref.shape[0] // n_chunks
    lo = lax.broadcasted_iota(jnp.int32, (1, HEAD_W), 1) < DA_QK_DIM
    qn_c = qn_ref[...] * (DA_QK_DIM ** -0.5 * LOG2E)
    gq_c = gq_ref[...] * (MLA_QK ** -0.5 * LOG2E)

    def da_norm(x, g):
        sq = x * x
        s_lo = jnp.sum(jnp.where(lo, sq, 0.0), axis=-1, keepdims=True)
        s_hi = jnp.sum(jnp.where(lo, 0.0, sq), axis=-1, keepdims=True)
        r = jnp.where(lo, lax.rsqrt(s_lo / DA_QK_DIM + EPS), lax.rsqrt(s_hi / DA_QK_DIM + EPS))
        return x * r * g

    for c in range(n_chunks):
        rows = slice(c * rc, (c + 1) * rc)

        def hdot(c0, width=MXU_N):
            return jnp.dot(h_ref[rows], w_ref[:, c0:c0 + width], preferred_element_type=F32)

        if rope:
            tabs_da = [t[rows] for t in tab_refs[:3]]
            tabs_mla = [t[rows] for t in tab_refs[3:]]

        def da_pairs(pairs):
            for pair in pairs:
                pq = hdot(pair * MXU_N)
                pk = hdot(DA_W + pair * MXU_N)
                for s in range(2):
                    sl = slice((2 * pair + s) * HEAD_W, (2 * pair + s + 1) * HEAD_W)
                    q = da_norm(pq[:, s * HEAD_W:(s + 1) * HEAD_W], qn_c)
                    k = da_norm(pk[:, s * HEAD_W:(s + 1) * HEAD_W], kn_ref[...])
                    if rope:
                        q = _rope(q, tabs_da, DA_QK_DIM // 4)
                        k = _rope(k, tabs_da, DA_QK_DIM // 4)
                    else:
                        outs[6][rows, sl] = k
                    qda_ref[rows, sl] = q.astype(BF16)
                    kda_ref[rows, sl] = k.astype(BF16)

        def mla_q_heads(mq, heads):
            for h in heads:
                qf = jnp.dot(mq, wq_ref[:, h * MLA_SLAB:(h + 1) * MLA_SLAB], preferred_element_type=F32)
                a, b = qf[:, :LANES], qf[:, LANES:]
                ss = jnp.sum(a * a, axis=-1, keepdims=True) + jnp.sum(b * b, axis=-1, keepdims=True)
                r = lax.rsqrt(ss / MLA_QK + EPS)
                a = a * r * gq_c[:, :LANES]
                b = b * r * gq_c[:, LANES:]
                if rope:
                    b = _rope(b, tabs_mla, MLA_ROPE // 4)
                qm_ref[rows, h * MLA_SLAB:h * MLA_SLAB + LANES] = a.astype(BF16)
                qm_ref[rows, h * MLA_SLAB + LANES:(h + 1) * MLA_SLAB] = b.astype(BF16)

        mq = _rms(hdot(3 * DA_W, MLA_RANK), mqn_ref[...]).astype(BF16)
        ckv = _rms(hdot(3 * DA_W + MLA_RANK, MLA_RANK), mkvn_ref[...])
        kr = hdot(AB_MAIN, LANES)
        if not rope:
            outs[8][rows] = ckv
            outs[9][rows] = kr[:, :MLA_ROPE]
        ckv_b = ckv.astype(BF16)
        da_pairs(range(0, DA_HEADS // 4))
        mla_q_heads(mq, range(0, MLA_HEADS // 2))
        da_pairs(range(DA_HEADS // 4, DA_HEADS // 2))
        mla_q_heads(mq, range(MLA_HEADS // 2, MLA_HEADS))
        for cc in range(DA_W // MXU_N):
            dv = hdot(2 * DA_W + cc * MXU_N)
            vda_ref[rows, cc * MXU_N:(cc + 1) * MXU_N] = dv.astype(BF16)
            if not rope:
                outs[7][rows, cc * MXU_N:(cc + 1) * MXU_N] = dv
        krg = kr * gkr_ref[...]
        if rope:
            krg = _rope(krg, tabs_mla, MLA_ROPE // 4)

        def kv_dot(c0):
            return jnp.dot(ckv_b, wkv_ref[:, c0:c0 + MXU_N], preferred_element_type=F32)

        _mla_keys(kv_dot, kr, krg, gkn_ref, km_ref, vm_ref, rows)

    h_ref[...] = hn_ref[...]


def _front_ab(x, norm_g, mod, trunk, w, tabs, tag, casts=()):
    tm, n_chunks = 256, 1
    rows, rope, d = trunk.rows, trunk.rope, D_MODEL
    in_specs = _norm_ahead_specs(0, tm, trunk, d) + [
                _resident((d, AB_MAIN + LANES)),
                _const((1, HEAD_W)), _const((1, HEAD_W)), _const((1, MLA_RANK)),
                _resident((MLA_RANK, MLA_HEADS * MLA_SLAB)), _const((1, MLA_RANK)),
                _resident((MLA_RANK, MLA_HEADS * (MLA_NOPE + MLA_V))),
                _const((1, MLA_SLAB)), _const((1, LANES)), _const((1, LANES))]
    args = [x, x, norm_g.reshape(1, d), mod, mod, mod, mod, w["w_in0"], w["da_qn"], w["da_kn"],
            w["mq_norm"], w["wq"], w["mkv_norm"], w["wkv"], w["gq"], w["gkn"], w["gkr"]]
    if rope:
        per = trunk.seq // tm
        in_specs += [pl.BlockSpec((tm, LANES), lambda i: (i % per, 0))] * 6
        args += list(tabs)
    row = lambda n: pl.BlockSpec((tm, n), lambda i: (i, 0))
    widths = (DA_W, DA_W, DA_W, MLA_HEADS * MLA_SLAB, MLA_HEADS * MLA_SLAB, MLA_HEADS * MLA_V)
    out_specs = [row(n) for n in widths]
    out_shape = [jax.ShapeDtypeStruct((rows, n), BF16) for n in widths]
    out_bytes = sum(_nbytes((tm, n), BF16) for n in widths)
    if not rope:
        cache_w = (DA_W, DA_W, MLA_RANK, MLA_ROPE)
        out_specs += [row(n) for n in cache_w]
        out_shape += [jax.ShapeDtypeStruct((rows, n), F32) for n in cache_w]
        out_bytes += sum(_nbytes((tm, n), F32) for n in cache_w)
    resident = _nbytes((d, AB_MAIN + LANES), BF16) + 2 * _nbytes((MLA_RANK, MLA_HEADS * MLA_SLAB), BF16)
    blocks = _nbytes((tm, d), F32) + out_bytes + 6 * _nbytes((tm, LANES), F32)
    body, c_bytes = _add_side_casts(functools.partial(_front_ab_kernel, rope, n_chunks), casts,
                                    rows // tm, lambda i: i, in_specs, args, out_specs, out_shape)
    blocks += c_bytes
    return pl.pallas_call(
        body,
        grid=(rows // tm,),
        in_specs=in_specs,
        out_specs=out_specs,
        out_shape=out_shape,
        scratch_shapes=[pltpu.VMEM((tm, d), BF16), pltpu.VMEM((tm, d), BF16)],
        compiler_params=_params(("arbitrary",),
                                _vmem_limit(blocks, resident + 2 * _nbytes((tm, d), BF16),
                                            4 * _nbytes((tm, d), F32))),
        name=f"front_ab_{tag}",
    )(*args)


def _ctx_mla_kernel(ckv_ref, kr_ref, wkv_ref, gkn_ref, gkr_ref, km_ref, vm_ref):
    ckv_b = ckv_ref[...].astype(BF16)
    kr = kr_ref[...]

    def kv_dot(c0):
        return jnp.dot(ckv_b, wkv_ref[:, c0:c0 + MXU_N], preferred_element_type=F32)

    _mla_keys(kv_dot, kr, kr * gkr_ref[...], gkn_ref, km_ref, vm_ref, slice(None))


def _ctx_mla(ckv, kr128, w):
    rows = ckv.shape[0]
    tm = 256
    kw, vw = MLA_HEADS * MLA_SLAB, MLA_HEADS * MLA_V
    blocks = (_nbytes((tm, MLA_RANK + LANES), F32) + _nbytes((MLA_RANK, MLA_HEADS * (MLA_NOPE + MLA_V)), BF16)
              + _nbytes((tm, kw + vw), BF16))
    return pl.pallas_call(
        _ctx_mla_kernel,
        grid=(rows // tm,),
        in_specs=[pl.BlockSpec((tm, MLA_RANK), lambda i: (i, 0)),
                  pl.BlockSpec((tm, LANES), lambda i: (i, 0)),
                  _const((MLA_RANK, MLA_HEADS * (MLA_NOPE + MLA_V))), _const((1, LANES)), _const((1, LANES))],
        out_specs=[pl.BlockSpec((tm, kw), lambda i: (i, 0)),
                   pl.BlockSpec((tm, vw), lambda i: (i, 0))],
        out_shape=[jax.ShapeDtypeStruct((rows, kw), BF16),
                   jax.ShapeDtypeStruct((rows, vw), BF16)],
        compiler_params=_params(("arbitrary",),
                                _vmem_limit(blocks, 0, 2 * _nbytes((tm, kw), F32))),
        name="ctx_mla",
    )(ckv, kr128, w["wkv"], w["gkn"], w["gkr"])


def _front_c_kernel(rope, n_chunks, x0_ref, xn_ref, g_ref, sh0_ref, sc0_ref, shn_ref, scn_ref, w_ref,
                    qn_ref, kn_ref, *refs):
    if rope:
        tab_refs, outs, (h_ref, hn_ref) = refs[:3], refs[3:6], refs[6:]
    else:
        tab_refs, outs, (h_ref, hn_ref) = (), refs[:5], refs[5:]
    _norm_ahead(x0_ref, xn_ref, g_ref, sh0_ref, sc0_ref, shn_ref, scn_ref, h_ref, hn_ref)
    q_ref, k_ref, v_ref = outs[:3]
    rc = h_ref.shape[0] // n_chunks
    nq = GQ_HEADS * GQ_DIM
    nk = GQ_KV_HEADS * GQ_DIM
    qn_c = qn_ref[...] * (GQ_DIM ** -0.5 * LOG2E)

    def cache_rows(c, head):
        return pl.ds(c * rc * GQ_KV_HEADS + head, rc, stride=GQ_KV_HEADS)

    for c in range(n_chunks):
        rows = slice(c * rc, (c + 1) * rc)

        def hdot(c0):
            return jnp.dot(h_ref[rows], w_ref[:, c0:c0 + MXU_N], preferred_element_type=F32)

        if rope:
            tabs = [t[rows] for t in tab_refs]
        for pair in range(GQ_HEADS // 2):
            pq = hdot(pair * MXU_N)
            for s in range(2):
                sl = slice((2 * pair + s) * GQ_DIM, (2 * pair + s + 1) * GQ_DIM)
                q = _rms(pq[:, s * GQ_DIM:(s + 1) * GQ_DIM], qn_c)
                if rope:
                    q = _rope(q, tabs, GQ_DIM // 4)
                q_ref[rows, sl] = q.astype(BF16)
        for pair in range(GQ_KV_HEADS // 2):
            pk = hdot(nq + pair * MXU_N)
            for s in range(2):
                sl = slice((2 * pair + s) * GQ_DIM, (2 * pair + s + 1) * GQ_DIM)
                k = _rms(pk[:, s * GQ_DIM:(s + 1) * GQ_DIM], kn_ref[...])
                if rope:
                    k = _rope(k, tabs, GQ_DIM // 4)
                else:
                    outs[3][cache_rows(c, 2 * pair + s), :] = k
                k_ref[rows, sl] = k.astype(BF16)
        for cc in range(nk // MXU_N):
            v = hdot(nq + nk + cc * MXU_N)
            v_ref[rows, cc * MXU_N:(cc + 1) * MXU_N] = v.astype(BF16)
            if not rope:
                for s in range(2):
                    outs[4][cache_rows(c, 2 * cc + s), :] = v[:, s * GQ_DIM:(s + 1) * GQ_DIM]

    h_ref[...] = hn_ref[...]


def _front_c(x, norm_g, mod, trunk, w_in, w, tabs, tag, casts=()):
    tm, n_chunks = 256, 1
    rows, rope, d = trunk.rows, trunk.rope, D_MODEL
    nq = GQ_HEADS * GQ_DIM
    nk = GQ_KV_HEADS * GQ_DIM
    n = nq + 2 * nk
    in_specs = _norm_ahead_specs(1, tm, trunk, d) + [
                _resident((d, n)), _const((1, GQ_DIM)), _const((1, GQ_DIM))]
    args = [x, x, norm_g.reshape(1, d), mod, mod, mod, mod, w_in, w["gq_qn"], w["gq_kn"]]
    if rope:
        per = trunk.seq // tm
        in_specs += [pl.BlockSpec((tm, LANES), lambda i: (i % per, 0))] * 3
        args += list(tabs)
    row = lambda wd: pl.BlockSpec((tm, wd), lambda i: (i, 0))
    out_specs = [row(nq), row(nk), row(nk)]
    out_shape = [jax.ShapeDtypeStruct((rows, wd), BF16) for wd in (nq, nk, nk)]
    out_bytes = _nbytes((tm, n), BF16)
    if not rope:
        out_specs += [pl.BlockSpec((tm * GQ_KV_HEADS, GQ_DIM), lambda i: (i, 0))] * 2
        out_shape += [jax.ShapeDtypeStruct((rows * GQ_KV_HEADS, GQ_DIM), F32)] * 2
        out_bytes += 2 * _nbytes((tm, nk), F32)
    blocks = _nbytes((tm, d), F32) + out_bytes + 3 * _nbytes((tm, LANES), F32)
    body, c_bytes = _add_side_casts(functools.partial(_front_c_kernel, rope, n_chunks), casts,
                                    rows // tm, lambda i: i, in_specs, args, out_specs, out_shape)
    blocks += c_bytes
    return pl.pallas_call(
        body,
        grid=(rows // tm,),
        in_specs=in_specs,
        out_specs=out_specs,
        out_shape=out_shape,
        scratch_shapes=[pltpu.VMEM((tm, d), BF16), pltpu.VMEM((tm, d), BF16)],
        compiler_params=_params(("arbitrary",),
                                _vmem_limit(blocks, _nbytes((d, n), BF16) + 2 * _nbytes((tm, d), BF16),
                                            4 * _nbytes((tm, d), F32))),
        name=f"front_c_{tag}",
    )(*args)


def _dot_nt(a, b):
    return lax.dot_general(a, b, (((1,), (1,)), ((), ())), preferred_element_type=F32)


def _attn_rows(q, parts):
    scores = [_dot_nt(q, k) for k, _ in parts]
    m = functools.reduce(jnp.maximum, [jnp.max(s, axis=-1, keepdims=True) for s in scores])
    o1 = None
    for s, (_, v) in zip(scores, parts):
        e = jnp.exp2(s - m).astype(BF16)
        v1 = jnp.concatenate([v, jnp.ones((v.shape[0], LANES), BF16)], axis=1)
        part = jnp.dot(e, v1, preferred_element_type=F32)
        o1 = part if o1 is None else o1 + part
    dv = parts[0][1].shape[1]
    return o1[:, :dv], o1[:, dv:]


ONES_ROWS = 16


def _attn_cols(q, parts, mask=None, sink=None):
    scores = [_dot_nt(k, q) for k, _ in parts]
    if mask is not None:
        scores[-1] = jnp.where(mask, scores[-1], NEG_INF)
    m = functools.reduce(jnp.maximum, [jnp.max(s, axis=0, keepdims=True) for s in scores])
    if sink is not None:
        sink2 = sink * LOG2E
        m = jnp.maximum(m, sink2)
    o1 = None
    for s, (_, v) in zip(scores, parts):
        e = jnp.exp2(s - m).astype(BF16)
        v_t1 = jnp.concatenate([v.astype(F32).T.astype(BF16),
                                jnp.ones((ONES_ROWS, v.shape[0]), BF16)], axis=0)
        part = jnp.dot(v_t1, e, preferred_element_type=F32)
        o1 = part if o1 is None else o1 + part
    dv = parts[0][1].shape[1]
    den = o1[dv:dv + 1]
    if sink is not None:
        den = den + jnp.exp2(sink2 - m)
    return o1[:dv], den


def _head_rows(ref, h, n_heads, sl):
    if ref.shape[1] == sl.stop - sl.start:
        return ref[pl.ds(h, ref.shape[0] // n_heads, stride=n_heads), :]
    return ref[:, sl]


def _kv_parts(refs, n_parts, h, n_heads, sl_k, sl_v):
    return [(_head_rows(refs[2 * p], h, n_heads, sl_k).astype(BF16),
             _head_rows(refs[2 * p + 1], h, n_heads, sl_v).astype(BF16)) for p in range(n_parts)]


def _da_attn_kernel(n_parts, lam_init, lam_ref, gsub_ref, q_ref, *refs):
    o_ref = refs[-1]
    tq = q_ref.shape[0]
    lv = lam_ref[...]
    lam = (jnp.exp(jnp.sum(lv[0:1] * lv[1:2], axis=-1, keepdims=True))
           - jnp.exp(jnp.sum(lv[2:3] * lv[3:4], axis=-1, keepdims=True)) + lam_init)
    lo = lax.broadcasted_iota(jnp.int32, (1, HEAD_W), 1) < DA_QK_DIM
    for h in range(DA_HEADS):
        sl = slice(h * HEAD_W, (h + 1) * HEAD_W)
        q = q_ref[:, sl]
        zero = jnp.zeros_like(q)
        q12 = jnp.concatenate([jnp.where(lo, q, zero), jnp.where(lo, zero, q)], axis=0)
        o12, d12 = _attn_rows(q12, _kv_parts(refs, n_parts, h, DA_HEADS, sl, sl))
        o12 = o12 * (1.0 / d12)
        o = o12[:tq] - lam * o12[tq:]
        y = o * lax.rsqrt(jnp.mean(o * o, axis=-1, keepdims=True) + EPS) * gsub_ref[...]
        o_ref[:, sl] = (y * (1.0 - lam_init)).astype(BF16)


def _mla_attn_kernel(n_parts, q_ref, *refs):
    o_ref = refs[-1]
    for h in range(MLA_HEADS):
        sl_k = slice(h * MLA_SLAB, (h + 1) * MLA_SLAB)
        sl_v = slice(h * MLA_V, (h + 1) * MLA_V)
        o, den = _attn_rows(q_ref[:, sl_k], _kv_parts(refs, n_parts, h, MLA_HEADS, sl_k, sl_v))
        o_ref[:, sl_v] = (o * (1.0 / den)).astype(BF16)


def _gq_attn_kernel(n_parts, seq, kw, sink_ref, q_ref, *refs):
    o_ref = refs[-1]
    tq = q_ref.shape[0]
    qi = pl.program_id(1)
    mask = None
    if n_parts == 2:
        start = pl.multiple_of(jnp.clip(qi * tq - WINDOW, 0, seq - kw), WINDOW)
        keys = start + lax.broadcasted_iota(jnp.int32, (kw, GQ_GROUP * tq), 0)
        qrows = qi * tq + (lax.broadcasted_iota(jnp.int32, (kw, GQ_GROUP * tq), 1) & (tq - 1))
        mask = jnp.abs(qrows - keys) <= WINDOW
    for g in range(GQ_KV_HEADS):
        sl = slice(g * GQ_DIM, (g + 1) * GQ_DIM)
        heads = range(g * GQ_GROUP, (g + 1) * GQ_GROUP)
        q4 = jnp.concatenate([q_ref[:, j * GQ_DIM:(j + 1) * GQ_DIM] for j in heads], axis=0)
        sink = jnp.concatenate([jnp.broadcast_to(sink_ref[j:j + 1, 0:1], (1, tq)) for j in heads], axis=1)
        if n_parts == 2:
            parts = [(_head_rows(refs[0], g, GQ_KV_HEADS, sl).astype(BF16),
                      _head_rows(refs[1], g, GQ_KV_HEADS, sl)),
                     (refs[2][pl.ds(start, kw), sl], refs[3][pl.ds(start, kw), sl])]
        else:
            parts = [(refs[0][:, sl], refs[1][:, sl])]
        o, den = _attn_cols(q4, parts, mask=mask, sink=sink)
        o = o * (1.0 / den)
        for n, j in enumerate(heads):
            o_ref[:, j * GQ_DIM:(j + 1) * GQ_DIM] = o[:, n * tq:(n + 1) * tq].T.astype(BF16)


def _per_sequence(kernel, sps, n_extra, n_parts):
    def wrapped(*refs):
        for s in range(sps):
            def rows(r):
                n = r.shape[0] // sps
                return r.at[pl.ds(s * n, n)]

            kernel(*refs[:n_extra], *[rows(r) for r in refs[n_extra:n_extra + 2 + 2 * n_parts]])

    return wrapped if sps > 1 else kernel


def _attention(kernel, name, q, kv_parts, trunk, tq, out_w, stacked_rows, extra_in=(), casts=(), sps=1):
    nq = trunk.seq // tq
    assert sps == 1 or nq == 1
    in_specs = [_const(a.shape) for a in extra_in]
    args = list(extra_in)
    qw = q.shape[1]
    in_specs.append(pl.BlockSpec((sps * tq, qw), lambda b, i: (b * nq + i, 0)))
    args.append(q)
    blocks = sps * (_nbytes((tq, qw), BF16) + _nbytes((tq, out_w), BF16))
    total_l = 0
    for k, v, l in kv_parts:
        in_specs += [pl.BlockSpec((sps * l, k.shape[1]), lambda b, i: (b, 0)),
                     pl.BlockSpec((sps * l, v.shape[1]), lambda b, i: (b, 0))]
        args += [k, v]
        blocks += sps * (_nbytes((l, k.shape[1]), k.dtype) + _nbytes((l, v.shape[1]), v.dtype))
        total_l += l
    out_specs = [pl.BlockSpec((sps * tq, out_w), lambda b, i: (b * nq + i, 0))]
    out_shape = [jax.ShapeDtypeStruct((trunk.rows, out_w), BF16)]
    kernel = _per_sequence(kernel, sps, len(extra_in), len(kv_parts))
    kernel, c_bytes = _add_side_casts(kernel, casts, trunk.batch // sps * nq, lambda b, i: b * nq + i,
                                      in_specs, args, out_specs, out_shape)
    blocks += c_bytes
    out = pl.pallas_call(
        kernel,
        grid=(trunk.batch // sps, nq),
        in_specs=in_specs,
        out_specs=out_specs,
        out_shape=out_shape,
        compiler_params=_params(("arbitrary",) * 2,
                                _vmem_limit(blocks, 0, 6 * _nbytes((stacked_rows, total_l), F32))),
        name=name,
    )(*args)
    return out if casts else out[0]


CAST_PLAN = {
    "front_ab": {"ff1_0": ("ff1_f32", 0), "w_out0": ("w_out0_f32", 0), "w_in1": ("w_in1_f32", 0)},
    "da_attn": {"ff2_0": ("ff2_f32", 0)},
    "mla_attn": {},
    "front_c": {"ff2_1": ("ff2_f32", 1), "w_out1": ("w_out1_f32", 0)},
    "gq_attn": {"ff1_1": ("ff1_f32", 1)},
}


def _run_trunk(x, trunk, tag, mod, P, ctx, wb):
    tq = min(trunk.seq, 256)
    sps = 4 if trunk.seq == tq and trunk.batch % 4 == 0 else 1
    casting = wb is None
    wb = dict(wb or {})

    def jobs(call):
        return [(P[src], layer) for src, layer in CAST_PLAN[call].values()] if casting else []

    def split(call, outs, n_main):
        if not casting or not CAST_PLAN[call]:
            return outs
        for name, w in zip(CAST_PLAN[call], outs[n_main:]):
            wb[name] = w
        return outs[:n_main] if n_main > 1 else outs[0]

    n_front = 6 if trunk.rope else 10
    front = split("front_ab", _front_ab(x, P["norm1_g"][0], mod, trunk, P,
                                        P["tabs_ab"] if trunk.rope else None, tag,
                                        casts=jobs("front_ab")), n_front)
    qda, kda, vda, qm, km, vm = front[:6]
    da_parts, mla_parts = [(kda, vda, trunk.seq)], [(km, vm, trunk.seq)]
    if ctx is not None:
        da_parts = [(ctx["da_k"], ctx["da_v"], ctx["past"] * DA_HEADS)] + da_parts
        mla_parts = [(ctx["mla_k"], ctx["mla_v"], ctx["past"])] + mla_parts
    o_da = split("da_attn", _attention(functools.partial(_da_attn_kernel, len(da_parts), P["lam_init"]),
                                       f"da_attn_{tag}", qda, da_parts, trunk, tq, DA_HEADS * HEAD_W,
                                       2 * tq, extra_in=(P["lam4"], P["gsub"]), casts=jobs("da_attn"),
                                       sps=sps), 1)
    o_m = split("mla_attn", _attention(functools.partial(_mla_attn_kernel, len(mla_parts)),
                                       f"mla_attn_{tag}", qm, mla_parts, trunk, tq, MLA_HEADS * MLA_V,
                                       tq, casts=jobs("mla_attn"), sps=sps), 1)
    x, h = _outproj(x, P["norm2_g"][0], mod, 0, [(o_da, wb["w_out0"], 0), (o_m, wb["w_out0"], 1)],
                    trunk, tag)
    x = _ffn(x, h, mod, 0, wb["ff1_0"], wb["ff2_0"], trunk, tag)

    n_front_c = 3 if trunk.rope else 5
    front_c = split("front_c", _front_c(x, P["norm1_g"][1], mod, trunk, wb["w_in1"], P,
                                        P["tabs_gq"] if trunk.rope else None, tag,
                                        casts=jobs("front_c")), n_front_c)
    qc, kc, vc = front_c[:3]
    gq_parts = [(kc, vc, trunk.seq)]
    if ctx is not None:
        gq_parts = [(ctx["gq_k"], ctx["gq_v"], ctx["past"] * GQ_KV_HEADS)] + gq_parts
    kw = min(trunk.seq, tq + 2 * WINDOW)
    o_c = split("gq_attn", _attention(functools.partial(_gq_attn_kernel, len(gq_parts), trunk.seq, kw),
                                      f"gq_attn_{tag}", qc, gq_parts, trunk, tq, GQ_HEADS * GQ_DIM,
                                      GQ_GROUP * tq, extra_in=(P["sink"],), casts=jobs("gq_attn"),
                                      sps=sps), 1)
    x, h = _outproj(x, P["norm2_g"][1], mod, 1, [(o_c, wb["w_out1"], 0)], trunk, tag)
    x = _ffn(x, h, mod, 1, wb["ff1_1"], wb["ff2_1"], trunk, tag)
    return x, front[6:], front_c[3:], wb


def kernel(x_prompt, x_sample, cache_da_k, cache_da_v, cache_mla_ckv, cache_mla_krope, cache_gq_k, cache_gq_v, c, c_ctx, norm1_g, norm2_g, ada_w, ada_b, ff1_w, ff2_w, ab_w_in, ab_w_out, da_lambda_q1, da_lambda_k1, da_lambda_q2, da_lambda_k2, da_q_norm, da_k_norm, da_subln, mla_q_a_norm, mla_w_q_up, mla_kv_a_norm, mla_w_kv_up, mla_q_norm, mla_k_norm, c_w_in, c_w_out, gq_q_norm, gq_k_norm, gq_sink):
    pb, ps, d = x_prompt.shape
    sb, ss, _ = x_sample.shape
    past = cache_da_k.shape[2]
    assert sb + 1 <= 8 and d == D_MODEL

    cond8 = jnp.concatenate([c_ctx[None], c, jnp.zeros((8 - 1 - sb, d), F32)], axis=0)
    wq = jnp.pad(mla_w_q_up[0].reshape(MLA_RANK, MLA_HEADS, MLA_QK),
                 ((0, 0), (0, 0), (0, MLA_SLAB - MLA_QK))).reshape(MLA_RANK, MLA_HEADS * MLA_SLAB)
    wkv3 = mla_w_kv_up[0].reshape(MLA_RANK, MLA_HEADS, MLA_NOPE + MLA_V)
    wkv = jnp.concatenate([wkv3[..., :MLA_NOPE].reshape(MLA_RANK, -1),
                           wkv3[..., MLA_NOPE:].reshape(MLA_RANK, -1)], axis=1)
    P = {
        "norm1_g": norm1_g, "norm2_g": norm2_g,
        "w_in0": jnp.pad(ab_w_in[0], ((0, 0), (0, LANES - MLA_ROPE))).astype(BF16),
        "ff1_f32": ff1_w, "ff2_f32": ff2_w,
        "w_out0_f32": ab_w_out, "w_in1_f32": c_w_in, "w_out1_f32": c_w_out,
        "da_qn": jnp.tile(da_q_norm[0], 2).reshape(1, HEAD_W),
        "da_kn": jnp.tile(da_k_norm[0], 2).reshape(1, HEAD_W),
        "mq_norm": mla_q_a_norm[0].reshape(1, MLA_RANK),
        "mkv_norm": mla_kv_a_norm[0].reshape(1, MLA_RANK),
        "wq": wq.astype(BF16), "wkv": wkv.astype(BF16),
        "gq": jnp.pad(mla_q_norm[0], (0, MLA_SLAB - MLA_QK)).reshape(1, MLA_SLAB),
        "gkn": mla_k_norm[0, :MLA_NOPE].reshape(1, LANES),
        "gkr": jnp.pad(mla_k_norm[0, MLA_NOPE:], (0, LANES - MLA_ROPE)).reshape(1, LANES),
        "lam4": jnp.stack([da_lambda_q1[0], da_lambda_k1[0], da_lambda_q2[0], da_lambda_k2[0]]),
        "gsub": da_subln[0].reshape(1, HEAD_W),
        "lam_init": 0.8 - 0.6 * math.exp(-0.3 * 0),
        "gq_qn": gq_q_norm[0].reshape(1, GQ_DIM), "gq_kn": gq_k_norm[0].reshape(1, GQ_DIM),
        "sink": jnp.broadcast_to(gq_sink[0].reshape(GQ_HEADS, 1), (GQ_HEADS, LANES)),
        "tabs_ab": (_rope_tables(ss, [(32, "row"), (32, "col"), (32, "row"), (32, "col")])
                    + _rope_tables(ss, [(32, "row"), (32, "col"), (32, "none"), (32, "none")])),
        "tabs_gq": _rope_tables(ss, [(64, "row"), (64, "col")]),
    }

    mod = _ada_mod(cond8, ada_w, ada_b).reshape(2, 8, 1, 6 * d)

    kr_ctx = jnp.pad(cache_mla_krope[:, 0].reshape(sb * past, MLA_ROPE), ((0, 0), (0, LANES - MLA_ROPE)))
    mla_k_ctx, mla_v_ctx = _ctx_mla(cache_mla_ckv[:, 0].reshape(sb * past, MLA_RANK), kr_ctx, P)
    ctx = {
        "past": past,
        "da_k": cache_da_k[:, 0].reshape(-1, HEAD_W), "da_v": cache_da_v[:, 0].reshape(-1, HEAD_W),
        "mla_k": mla_k_ctx, "mla_v": mla_v_ctx,
        "gq_k": cache_gq_k[:, 0].reshape(-1, GQ_DIM), "gq_v": cache_gq_v[:, 0].reshape(-1, GQ_DIM),
    }

    prompt = Trunk(groups=1, seq=ps, batch=pb, mod_row0=0, rope=False)
    sample = Trunk(groups=sb, seq=ss, batch=sb, mod_row0=1, rope=True)
    y_p, (new_da_k, new_da_v, new_ckv, new_kr), (new_gq_k, new_gq_v), ffw = _run_trunk(
        x_prompt.reshape(pb * ps, d), prompt, "prompt", mod, P, None, None)
    y_s, _, _, _ = _run_trunk(x_sample.reshape(sb * ss, d), sample, "sample", mod, P, ctx, ffw)

    return (y_p.reshape(pb, ps, d), y_s.reshape(sb, ss, d),
            new_da_k.reshape(pb, 1, ps, DA_HEADS, HEAD_W), new_da_v.reshape(pb, 1, ps, DA_HEADS, HEAD_W),
            new_ckv.reshape(pb, 1, ps, MLA_RANK), new_kr.reshape(pb, 1, ps, MLA_ROPE),
            new_gq_k.reshape(pb, 1, ps, GQ_KV_HEADS, GQ_DIM), new_gq_v.reshape(pb, 1, ps, GQ_KV_HEADS, GQ_DIM))
```

```python
import functools
import math
from typing import NamedTuple

import jax
import jax.numpy as jnp
import numpy as np
from jax import lax
from jax.experimental import pallas as pl
from jax.experimental.pallas import tpu as pltpu

F32 = jnp.float32
BF16 = jnp.bfloat16

D_MODEL = 2048
GRID_W = 64
ROPE_BASE = 10000.0
EPS = 1e-6
NEG_INF = -1e30
LOG2E = math.log2(math.e)
DA_HEADS = 8
DA_QK_DIM = 64
DA_W = DA_HEADS * 2 * DA_QK_DIM
MLA_HEADS = 8
MLA_RANK = 512
MLA_NOPE = 128
MLA_ROPE = 64
MLA_V = 128
MLA_QK = MLA_NOPE + MLA_ROPE
MLA_SLAB = 256
GQ_HEADS = 16
GQ_KV_HEADS = 4
GQ_GROUP = GQ_HEADS // GQ_KV_HEADS
GQ_DIM = 128
WINDOW = 128
HEAD_W = 128
AB_MAIN = 3 * DA_W + 2 * MLA_RANK

LANES = 128
MXU_N = 256
VMEM_CAP_BYTES = 60 * 1024 * 1024


class Trunk(NamedTuple):
    groups: int
    seq: int
    batch: int
    mod_row0: int
    rope: bool

    @property
    def rows(self):
        return self.batch * self.seq


def _vmem_limit(block_bytes, scratch_bytes=0, temp_bytes=0):
    est = 2 * block_bytes + scratch_bytes + temp_bytes
    assert est <= 4 * VMEM_CAP_BYTES, est
    return VMEM_CAP_BYTES


def _nbytes(shape, dtype):
    return math.prod(shape) * jnp.dtype(dtype).itemsize


def _params(sem, vmem):
    return pltpu.CompilerParams(dimension_semantics=sem, vmem_limit_bytes=vmem)


def _resident(shape):
    return pl.BlockSpec(shape, lambda *_: (0,) * len(shape), pipeline_mode=pl.Buffered(1))


def _const(shape):
    return pl.BlockSpec(shape, lambda *_: (0,) * len(shape))


def _add_side_casts(kernel, casts, steps, step_of, in_specs, args, out_specs, out_shape):
    n_in, n_out, n_jobs = len(args), len(out_specs), len(casts)
    extra = 0
    for w, layer in casts:
        _, rows, cols = w.shape
        blk = rows // steps
        in_specs.append(pl.BlockSpec((None, blk, cols), lambda *ids, l=layer: (l, step_of(*ids), 0)))
        args.append(w)
        out_specs.append(pl.BlockSpec((blk, cols), lambda *ids: (step_of(*ids), 0)))
        out_shape.append(jax.ShapeDtypeStruct((rows, cols), BF16))
        extra += _nbytes((blk, cols), F32) + _nbytes((blk, cols), BF16)

    def wrapped(*refs):
        srcs = refs[n_in:n_in + n_jobs]
        dsts = refs[n_in + n_jobs + n_out:n_in + 2 * n_jobs + n_out]
        for src, dst in zip(srcs, dsts):
            dst[...] = src[...].astype(BF16)
        kernel(*refs[:n_in], *refs[n_in + n_jobs:n_in + n_jobs + n_out], *refs[n_in + 2 * n_jobs + n_out:])

    return (wrapped if casts else kernel), extra


def _ada_kernel(c_ref, w_ref, b_ref, o_ref):
    c = c_ref[...]
    s = (c / (1.0 + jnp.exp(-c))).astype(BF16)
    o_ref[0] = jnp.dot(s, w_ref[0].astype(BF16), preferred_element_type=F32) + b_ref[0]


def _ada_mod(cond8, ada_w, ada_b):
    depth, d, n = ada_w.shape
    tn = 1024
    blocks = _nbytes((d, tn), F32) + _nbytes((8, d), F32) + _nbytes((8, tn), F32)
    return pl.pallas_call(
        _ada_kernel,
        grid=(depth, n // tn),
        in_specs=[pl.BlockSpec((8, d), lambda l, j: (0, 0)),
                  pl.BlockSpec((1, d, tn), lambda l, j: (l, 0, j)),
                  pl.BlockSpec((1, 1, tn), lambda l, j: (l, 0, j))],
        out_specs=pl.BlockSpec((1, 8, tn), lambda l, j: (l, 0, j)),
        out_shape=jax.ShapeDtypeStruct((depth, 8, n), F32),
        compiler_params=_params(("arbitrary", "arbitrary"),
                                _vmem_limit(blocks, temp_bytes=_nbytes((d, tn), BF16))),
        name="ada_mod",
    )(cond8, ada_w, ada_b.reshape(depth, 1, n))


def _mod_spec(layer, chunk, tm, trunk):
    per = trunk.rows // trunk.groups // tm

    def idx(i, *_):
        return (layer, trunk.mod_row0 + i // per, 0, chunk)

    return pl.BlockSpec((1, 1, 1, D_MODEL), idx)


def _normmod(x, g, sc, sh):
    ms = jnp.mean(x * x, axis=-1, keepdims=True)
    y = x * lax.rsqrt(ms + EPS) * g
    return y * (1.0 + sc) + sh


def _rms(x, g):
    return x * lax.rsqrt(jnp.mean(x * x, axis=-1, keepdims=True) + EPS) * g


def _outproj_kernel(n_in, n_chunks, x_ref, gate_ref, g2_ref, sh2_ref, sc2_ref, *refs):
    o_ref, h_ref = refs[-2], refs[-1]
    rc = x_ref.shape[0] // n_chunks
    for c in range(n_chunks):
        rows = slice(c * rc, (c + 1) * rc)
        acc = None
        for k in range(n_in):
            part = jnp.dot(refs[2 * k][rows], refs[2 * k + 1][...], preferred_element_type=F32)
            acc = part if acc is None else acc + part
        x1 = x_ref[rows] + gate_ref[0, 0] * acc
        o_ref[rows] = x1
        h_ref[rows] = _normmod(x1, g2_ref[...], sc2_ref[0, 0], sh2_ref[0, 0]).astype(BF16)


def _outproj(x, norm2_g, mod, layer, pairs, trunk, tag):
    t, d = x.shape
    tm, n_chunks = 512, 2
    in_specs = [pl.BlockSpec((tm, d), lambda i: (i, 0)),
                _mod_spec(layer, 2, tm, trunk), _const((1, d)),
                _mod_spec(layer, 3, tm, trunk), _mod_spec(layer, 4, tm, trunk)]
    args = [x, mod, norm2_g.reshape(1, d), mod, mod]
    blocks = 2 * _nbytes((tm, d), F32) + _nbytes((tm, d), BF16)
    for o, w, blk in pairs:
        k = o.shape[1]
        in_specs += [pl.BlockSpec((tm, k), lambda i: (i, 0)),
                     pl.BlockSpec((k, d), lambda i, blk=blk: (blk, 0))]
        args += [o, w]
        blocks += _nbytes((tm, k), BF16) + _nbytes((k, d), BF16)
    return pl.pallas_call(
        functools.partial(_outproj_kernel, len(pairs), n_chunks),
        grid=(t // tm,),
        in_specs=in_specs,
        out_specs=[pl.BlockSpec((tm, d), lambda i: (i, 0)), pl.BlockSpec((tm, d), lambda i: (i, 0))],
        out_shape=[jax.ShapeDtypeStruct((t, d), F32), jax.ShapeDtypeStruct((t, d), BF16)],
        compiler_params=_params(("arbitrary",), _vmem_limit(blocks, 0, 2 * _nbytes((tm, d), F32))),
        name=f"outproj_l{layer}_{tag}",
    )(*args)


def _ffn_kernel(x_ref, h_ref, gate_ref, w1_ref, w2_ref, o_ref, acc_ref):
    f = pl.program_id(1)

    def hidden_chunk():
        a = jnp.dot(h_ref[...], w1_ref[...], preferred_element_type=F32)
        a = jnp.square(jnp.maximum(a, 0.0)).astype(BF16)
        return jnp.dot(a, w2_ref[...], preferred_element_type=F32)

    @pl.when(f == 0)
    def _():
        acc_ref[...] = hidden_chunk()

    @pl.when(f > 0)
    def _():
        acc_ref[...] += hidden_chunk()

    @pl.when(f == pl.num_programs(1) - 1)
    def _():
        o_ref[...] = x_ref[...] + gate_ref[0, 0] * acc_ref[...]


def _ffn(x, h, mod, layer, w1, w2, trunk, tag):
    t, d = x.shape
    ff = w1.shape[1]
    tm, tf = 512, 1024
    blocks = (2 * _nbytes((tm, d), F32) + _nbytes((tm, d), BF16) + _nbytes((d, tf), BF16)
              + _nbytes((tf, d), BF16) + _nbytes((1, d), F32))
    return pl.pallas_call(
        _ffn_kernel,
        grid=(t // tm, ff // tf),
        in_specs=[pl.BlockSpec((tm, d), lambda i, f: (i, 0)),
                  pl.BlockSpec((tm, d), lambda i, f: (i, 0)),
                  _mod_spec(layer, 5, tm, trunk),
                  pl.BlockSpec((d, tf), lambda i, f: (0, f)),
                  pl.BlockSpec((tf, d), lambda i, f: (f, 0))],
        out_specs=pl.BlockSpec((tm, d), lambda i, f: (i, 0)),
        out_shape=jax.ShapeDtypeStruct((t, d), F32),
        scratch_shapes=[pltpu.VMEM((tm, d), F32)],
        compiler_params=_params(("arbitrary", "arbitrary"),
                                _vmem_limit(blocks, _nbytes((tm, d), F32),
                                            _nbytes((tm, tf), F32) * 2 + _nbytes((tm, d), F32))),
        name=f"ffn_l{layer}_{tag}",
    )(x, h, mod, w1, w2)


def _rope_tables(seq, pattern):
    pos_row = (np.arange(seq) // GRID_W).astype(np.float64)
    pos_col = (np.arange(seq) % GRID_W).astype(np.float64)
    cos_cols, sa_cols, sb_cols = [], [], []
    for width, kind in pattern:
        if kind == "none":
            cos_cols.append(np.ones((seq, width)))
            sa_cols.append(np.zeros((seq, width)))
            sb_cols.append(np.zeros((seq, width)))
            continue
        half = width // 2
        inv = ROPE_BASE ** (-np.arange(half, dtype=np.float64) / half)
        pos = pos_row if kind == "row" else pos_col
        ang = pos[:, None] * inv
        cos, sin = np.cos(ang), np.sin(ang)
        zero = np.zeros_like(sin)
        cos_cols += [cos, cos]
        sa_cols += [-sin, zero]
        sb_cols += [zero, sin]
    tabs = [jnp.asarray(np.concatenate(c, axis=1), F32) for c in (cos_cols, sa_cols, sb_cols)]
    assert tabs[0].shape == (seq, LANES)
    return tabs


def _rope(x, tabs, half):
    cos, sin_a, sin_b = tabs
    return (x * cos + pltpu.roll(x, LANES - half, 1) * sin_a + pltpu.roll(x, half, 1) * sin_b)


def _mla_keys(kv_dot, kr, krg, gkn_ref, km_ref, vm_ref, rows):
    ss_kr = jnp.sum(kr * kr, axis=-1, keepdims=True)
    for pair in range(MLA_HEADS // 2):
        kn2 = kv_dot(pair * MXU_N)
        for s in range(2):
            h = 2 * pair + s
            kn = kn2[:, s * LANES:(s + 1) * LANES]
            r = lax.rsqrt((jnp.sum(kn * kn, axis=-1, keepdims=True) + ss_kr) / MLA_QK + EPS)
            km_ref[rows, h * MLA_SLAB:h * MLA_SLAB + LANES] = (kn * r * gkn_ref[...]).astype(BF16)
            km_ref[rows, h * MLA_SLAB + LANES:(h + 1) * MLA_SLAB] = (krg * r).astype(BF16)
    for c in range(MLA_HEADS * MLA_V // MXU_N):
        v = kv_dot(MLA_HEADS * MLA_NOPE + c * MXU_N)
        vm_ref[rows, c * MXU_N:(c + 1) * MXU_N] = v.astype(BF16)


def _norm_ahead(x0_ref, xn_ref, g_ref, sh0_ref, sc0_ref, shn_ref, scn_ref, h_ref, hn_ref):
    @pl.when(pl.program_id(0) == 0)
    def _():
        h_ref[...] = _normmod(x0_ref[...], g_ref[...], sc0_ref[0, 0], sh0_ref[0, 0]).astype(BF16)

    hn_ref[...] = _normmod(xn_ref[...], g_ref[...], scn_ref[0, 0], shn_ref[0, 0]).astype(BF16)


def _norm_ahead_specs(layer, tm, trunk, d):
    n_tiles = trunk.rows // tm
    per = trunk.rows // trunk.groups // tm
    nxt = lambda i: jnp.minimum(i + 1, n_tiles - 1)

    def mod(chunk, tile_of):
        return pl.BlockSpec((1, 1, 1, d), lambda i: (layer, trunk.mod_row0 + tile_of(i) // per, 0, chunk))

    return [pl.BlockSpec((tm, d), lambda i: (0, 0), pipeline_mode=pl.Buffered(1)),
            pl.BlockSpec((tm, d), lambda i: (nxt(i), 0)), _const((1, d)),
            mod(0, lambda i: 0), mod(1, lambda i: 0), mod(0, nxt), mod(1, nxt)]


def _front_ab_kernel(rope, n_chunks, x0_ref, xn_ref, g_ref, sh0_ref, sc0_ref, shn_ref, scn_ref, w_ref,
                     qn_ref, kn_ref, mqn_ref, wq_ref, mkvn_ref, wkv_ref, gq_ref, gkn_ref, gkr_ref, *refs):
    if rope:
        tab_refs, outs, (h_ref, hn_ref) = refs[:6], refs[6:12], refs[12:]
    else:
        tab_refs, outs, (h_ref, hn_ref) = (), refs[:10], refs[10:]
    _norm_ahead(x0_ref, xn_ref, g_ref, sh0_ref, sc0_ref, shn_ref, scn_ref, h_ref, hn_ref)
    qda_ref, kda_ref, vda_ref, qm_ref, km_ref, vm_ref = outs[:6]
    rc = h_ref.shape[0] // n_chunks
    lo = lax.broadcasted_iota(jnp.int32, (1, HEAD_W), 1) < DA_QK_DIM
    qn_c = qn_ref[...] * (DA_QK_DIM ** -0.5 * LOG2E)
    gq_c = gq_ref[...] * (MLA_QK ** -0.5 * LOG2E)

    def da_norm(x, g):
        sq = x * x
        s_lo = jnp.sum(jnp.where(lo, sq, 0.0), axis=-1, keepdims=True)
        s_hi = jnp.sum(jnp.where(lo, 0.0, sq), axis=-1, keepdims=True)
        r = jnp.where(lo, lax.rsqrt(s_lo / DA_QK_DIM + EPS), lax.rsqrt(s_hi / DA_QK_DIM + EPS))
        return x * r * g

    for c in range(n_chunks):
        rows = slice(c * rc, (c + 1) * rc)

        def hdot(c0, width=MXU_N):
            return jnp.dot(h_ref[rows], w_ref[:, c0:c0 + width], preferred_element_type=F32)

        if rope:
            tabs_da = [t[rows] for t in tab_refs[:3]]
            tabs_mla = [t[rows] for t in tab_refs[3:]]

        def da_pairs(pairs):
            for pair in pairs:
                pq = hdot(pair * MXU_N)
                pk = hdot(DA_W + pair * MXU_N)
                for s in range(2):
                    sl = slice((2 * pair + s) * HEAD_W, (2 * pair + s + 1) * HEAD_W)
                    q = da_norm(pq[:, s * HEAD_W:(s + 1) * HEAD_W], qn_c)
                    k = da_norm(pk[:, s * HEAD_W:(s + 1) * HEAD_W], kn_ref[...])
                    if rope:
                        q = _rope(q, tabs_da, DA_QK_DIM // 4)
                        k = _rope(k, tabs_da, DA_QK_DIM // 4)
                    else:
                        outs[6][rows, sl] = k
                    qda_ref[rows, sl] = q.astype(BF16)
                    kda_ref[rows, sl] = k.astype(BF16)

        def mla_q_heads(mq, heads):
            for h in heads:
                qf = jnp.dot(mq, wq_ref[:, h * MLA_SLAB:(h + 1) * MLA_SLAB], preferred_element_type=F32)
                a, b = qf[:, :LANES], qf[:, LANES:]
                ss = jnp.sum(a * a, axis=-1, keepdims=True) + jnp.sum(b * b, axis=-1, keepdims=True)
                r = lax.rsqrt(ss / MLA_QK + EPS)
                a = a * r * gq_c[:, :LANES]
                b = b * r * gq_c[:, LANES:]
                if rope:
                    b = _rope(b, tabs_mla, MLA_ROPE // 4)
                qm_ref[rows, h * MLA_SLAB:h * MLA_SLAB + LANES] = a.astype(BF16)
                qm_ref[rows, h * MLA_SLAB + LANES:(h + 1) * MLA_SLAB] = b.astype(BF16)

        mq = _rms(hdot(3 * DA_W, MLA_RANK), mqn_ref[...]).astype(BF16)
        ckv = _rms(hdot(3 * DA_W + MLA_RANK, MLA_RANK), mkvn_ref[...])
        kr = hdot(AB_MAIN, LANES)
        if not rope:
            outs[8][rows] = ckv
            outs[9][rows] = kr[:, :MLA_ROPE]
        ckv_b = ckv.astype(BF16)
        da_pairs(range(0, DA_HEADS // 4))
        mla_q_heads(mq, range(0, MLA_HEADS // 2))
        da_pairs(range(DA_HEADS // 4, DA_HEADS // 2))
        mla_q_heads(mq, range(MLA_HEADS // 2, MLA_HEADS))
        for cc in range(DA_W // MXU_N):
            dv = hdot(2 * DA_W + cc * MXU_N)
            vda_ref[rows, cc * MXU_N:(cc + 1) * MXU_N] = dv.astype(BF16)
            if not rope:
                outs[7][rows, cc * MXU_N:(cc + 1) * MXU_N] = dv
        krg = kr * gkr_ref[...]
        if rope:
            krg = _rope(krg, tabs_mla, MLA_ROPE // 4)

        def kv_dot(c0):
            return jnp.dot(ckv_b, wkv_ref[:, c0:c0 + MXU_N], preferred_element_type=F32)

        _mla_keys(kv_dot, kr, krg, gkn_ref, km_ref, vm_ref, rows)

    h_ref[...] = hn_ref[...]


def _front_ab(x, norm_g, mod, trunk, w, tabs, tag, casts=()):
    tm, n_chunks = 256, 1
    rows, rope, d = trunk.rows, trunk.rope, D_MODEL
    in_specs = _norm_ahead_specs(0, tm, trunk, d) + [
                _resident((d, AB_MAIN + LANES)),
                _const((1, HEAD_W)), _const((1, HEAD_W)), _const((1, MLA_RANK)),
                _resident((MLA_RANK, MLA_HEADS * MLA_SLAB)), _const((1, MLA_RANK)),
                _resident((MLA_RANK, MLA_HEADS * (MLA_NOPE + MLA_V))),
                _const((1, MLA_SLAB)), _const((1, LANES)), _const((1, LANES))]
    args = [x, x, norm_g.reshape(1, d), mod, mod, mod, mod, w["w_in0"], w["da_qn"], w["da_kn"],
            w["mq_norm"], w["wq"], w["mkv_norm"], w["wkv"], w["gq"], w["gkn"], w["gkr"]]
    if rope:
        per = trunk.seq // tm
        in_specs += [pl.BlockSpec((tm, LANES), lambda i: (i % per, 0))] * 6
        args += list(tabs)
    row = lambda n: pl.BlockSpec((tm, n), lambda i: (i, 0))
    widths = (DA_W, DA_W, DA_W, MLA_HEADS * MLA_SLAB, MLA_HEADS * MLA_SLAB, MLA_HEADS * MLA_V)
    out_specs = [row(n) for n in widths]
    out_shape = [jax.ShapeDtypeStruct((rows, n), BF16) for n in widths]
    out_bytes = sum(_nbytes((tm, n), BF16) for n in widths)
    if not rope:
        cache_w = (DA_W, DA_W, MLA_RANK, MLA_ROPE)
        out_specs += [row(n) for n in cache_w]
        out_shape += [jax.ShapeDtypeStruct((rows, n), F32) for n in cache_w]
        out_bytes += sum(_nbytes((tm, n), F32) for n in cache_w)
    resident = _nbytes((d, AB_MAIN + LANES), BF16) + 2 * _nbytes((MLA_RANK, MLA_HEADS * MLA_SLAB), BF16)
    blocks = _nbytes((tm, d), F32) + out_bytes + 6 * _nbytes((tm, LANES), F32)
    body, c_bytes = _add_side_casts(functools.partial(_front_ab_kernel, rope, n_chunks), casts,
                                    rows // tm, lambda i: i, in_specs, args, out_specs, out_shape)
    blocks += c_bytes
    return pl.pallas_call(
        body,
        grid=(rows // tm,),
        in_specs=in_specs,
        out_specs=out_specs,
        out_shape=out_shape,
        scratch_shapes=[pltpu.VMEM((tm, d), BF16), pltpu.VMEM((tm, d), BF16)],
        compiler_params=_params(("arbitrary",),
                                _vmem_limit(blocks, resident + 2 * _nbytes((tm, d), BF16),
                                            4 * _nbytes((tm, d), F32))),
        name=f"front_ab_{tag}",
    )(*args)


def _ctx_mla_kernel(ckv_ref, kr_ref, wkv_ref, gkn_ref, gkr_ref, km_ref, vm_ref):
    ckv_b = ckv_ref[...].astype(BF16)
    kr = kr_ref[...]

    def kv_dot(c0):
        return jnp.dot(ckv_b, wkv_ref[:, c0:c0 + MXU_N], preferred_element_type=F32)

    _mla_keys(kv_dot, kr, kr * gkr_ref[...], gkn_ref, km_ref, vm_ref, slice(None))


def _ctx_mla(ckv, kr128, w):
    rows = ckv.shape[0]
    tm = 512
    kw, vw = MLA_HEADS * MLA_SLAB, MLA_HEADS * MLA_V
    blocks = (_nbytes((tm, MLA_RANK + LANES), F32) + _nbytes((MLA_RANK, MLA_HEADS * (MLA_NOPE + MLA_V)), BF16)
              + _nbytes((tm, kw + vw), BF16))
    return pl.pallas_call(
        _ctx_mla_kernel,
        grid=(rows // tm,),
        in_specs=[pl.BlockSpec((tm, MLA_RANK), lambda i: (i, 0)),
                  pl.BlockSpec((tm, LANES), lambda i: (i, 0)),
                  _const((MLA_RANK, MLA_HEADS * (MLA_NOPE + MLA_V))), _const((1, LANES)), _const((1, LANES))],
        out_specs=[pl.BlockSpec((tm, kw), lambda i: (i, 0)),
                   pl.BlockSpec((tm, vw), lambda i: (i, 0))],
        out_shape=[jax.ShapeDtypeStruct((rows, kw), BF16),
                   jax.ShapeDtypeStruct((rows, vw), BF16)],
        compiler_params=_params(("arbitrary",),
                                _vmem_limit(blocks, 0, 2 * _nbytes((tm, kw), F32))),
        name="ctx_mla",
    )(ckv, kr128, w["wkv"], w["gkn"], w["gkr"])


def _front_c_kernel(rope, n_chunks, x0_ref, xn_ref, g_ref, sh0_ref, sc0_ref, shn_ref, scn_ref, w_ref,
                    qn_ref, kn_ref, *refs):
    if rope:
        tab_refs, outs, (h_ref, hn_ref) = refs[:3], refs[3:6], refs[6:]
    else:
        tab_refs, outs, (h_ref, hn_ref) = (), refs[:5], refs[5:]
    _norm_ahead(x0_ref, xn_ref, g_ref, sh0_ref, sc0_ref, shn_ref, scn_ref, h_ref, hn_ref)
    q_ref, k_ref, v_ref = outs[:3]
    rc = h_ref.shape[0] // n_chunks
    nq = GQ_HEADS * GQ_DIM
    nk = GQ_KV_HEADS * GQ_DIM
    qn_c = qn_ref[...] * (GQ_DIM ** -0.5 * LOG2E)

    def cache_rows(c, head):
        return pl.ds(c * rc * GQ_KV_HEADS + head, rc, stride=GQ_KV_HEADS)

    for c in range(n_chunks):
        rows = slice(c * rc, (c + 1) * rc)

        def hdot(c0):
            return jnp.dot(h_ref[rows], w_ref[:, c0:c0 + MXU_N], preferred_element_type=F32)

        if rope:
            tabs = [t[rows] for t in tab_refs]
        for pair in range(GQ_HEADS // 2):
            pq = hdot(pair * MXU_N)
            for s in range(2):
                sl = slice((2 * pair + s) * GQ_DIM, (2 * pair + s + 1) * GQ_DIM)
                q = _rms(pq[:, s * GQ_DIM:(s + 1) * GQ_DIM], qn_c)
                if rope:
                    q = _rope(q, tabs, GQ_DIM // 4)
                q_ref[rows, sl] = q.astype(BF16)
        for pair in range(GQ_KV_HEADS // 2):
            pk = hdot(nq + pair * MXU_N)
            for s in range(2):
                sl = slice((2 * pair + s) * GQ_DIM, (2 * pair + s + 1) * GQ_DIM)
                k = _rms(pk[:, s * GQ_DIM:(s + 1) * GQ_DIM], kn_ref[...])
                if rope:
                    k = _rope(k, tabs, GQ_DIM // 4)
                else:
                    outs[3][cache_rows(c, 2 * pair + s), :] = k
                k_ref[rows, sl] = k.astype(BF16)
        for cc in range(nk // MXU_N):
            v = hdot(nq + nk + cc * MXU_N)
            v_ref[rows, cc * MXU_N:(cc + 1) * MXU_N] = v.astype(BF16)
            if not rope:
                for s in range(2):
                    outs[4][cache_rows(c, 2 * cc + s), :] = v[:, s * GQ_DIM:(s + 1) * GQ_DIM]

    h_ref[...] = hn_ref[...]


def _front_c(x, norm_g, mod, trunk, w_in, w, tabs, tag, casts=()):
    tm, n_chunks = 256, 1
    rows, rope, d = trunk.rows, trunk.rope, D_MODEL
    nq = GQ_HEADS * GQ_DIM
    nk = GQ_KV_HEADS * GQ_DIM
    n = nq + 2 * nk
    in_specs = _norm_ahead_specs(1, tm, trunk, d) + [
                _resident((d, n)), _const((1, GQ_DIM)), _const((1, GQ_DIM))]
    args = [x, x, norm_g.reshape(1, d), mod, mod, mod, mod, w_in, w["gq_qn"], w["gq_kn"]]
    if rope:
        per = trunk.seq // tm
        in_specs += [pl.BlockSpec((tm, LANES), lambda i: (i % per, 0))] * 3
        args += list(tabs)
    row = lambda wd: pl.BlockSpec((tm, wd), lambda i: (i, 0))
    out_specs = [row(nq), row(nk), row(nk)]
    out_shape = [jax.ShapeDtypeStruct((rows, wd), BF16) for wd in (nq, nk, nk)]
    out_bytes = _nbytes((tm, n), BF16)
    if not rope:
        out_specs += [pl.BlockSpec((tm * GQ_KV_HEADS, GQ_DIM), lambda i: (i, 0))] * 2
        out_shape += [jax.ShapeDtypeStruct((rows * GQ_KV_HEADS, GQ_DIM), F32)] * 2
        out_bytes += 2 * _nbytes((tm, nk), F32)
    blocks = _nbytes((tm, d), F32) + out_bytes + 3 * _nbytes((tm, LANES), F32)
    body, c_bytes = _add_side_casts(functools.partial(_front_c_kernel, rope, n_chunks), casts,
                                    rows // tm, lambda i: i, in_specs, args, out_specs, out_shape)
    blocks += c_bytes
    return pl.pallas_call(
        body,
        grid=(rows // tm,),
        in_specs=in_specs,
        out_specs=out_specs,
        out_shape=out_shape,
        scratch_shapes=[pltpu.VMEM((tm, d), BF16), pltpu.VMEM((tm, d), BF16)],
        compiler_params=_params(("arbitrary",),
                                _vmem_limit(blocks, _nbytes((d, n), BF16) + 2 * _nbytes((tm, d), BF16),
                                            4 * _nbytes((tm, d), F32))),
        name=f"front_c_{tag}",
    )(*args)


def _dot_nt(a, b):
    return lax.dot_general(a, b, (((1,), (1,)), ((), ())), preferred_element_type=F32)


def _attn_rows(q, parts):
    scores = [_dot_nt(q, k) for k, _ in parts]
    m = functools.reduce(jnp.maximum, [jnp.max(s, axis=-1, keepdims=True) for s in scores])
    o1 = None
    for s, (_, v) in zip(scores, parts):
        e = jnp.exp2(s - m).astype(BF16)
        v1 = jnp.concatenate([v, jnp.ones((v.shape[0], LANES), BF16)], axis=1)
        part = jnp.dot(e, v1, preferred_element_type=F32)
        o1 = part if o1 is None else o1 + part
    dv = parts[0][1].shape[1]
    return o1[:, :dv], o1[:, dv:]


ONES_ROWS = 16


def _attn_cols(q, parts, mask=None, sink=None):
    scores = [_dot_nt(k, q) for k, _ in parts]
    if mask is not None:
        scores[-1] = jnp.where(mask, scores[-1], NEG_INF)
    m = functools.reduce(jnp.maximum, [jnp.max(s, axis=0, keepdims=True) for s in scores])
    if sink is not None:
        sink2 = sink * LOG2E
        m = jnp.maximum(m, sink2)
    o1 = None
    for s, (_, v) in zip(scores, parts):
        e = jnp.exp2(s - m).astype(BF16)
        v_t1 = jnp.concatenate([v.astype(F32).T.astype(BF16),
                                jnp.ones((ONES_ROWS, v.shape[0]), BF16)], axis=0)
        part = jnp.dot(v_t1, e, preferred_element_type=F32)
        o1 = part if o1 is None else o1 + part
    dv = parts[0][1].shape[1]
    den = o1[dv:dv + 1]
    if sink is not None:
        den = den + jnp.exp2(sink2 - m)
    return o1[:dv], den


def _head_rows(ref, h, n_heads, sl):
    if ref.shape[1] == sl.stop - sl.start:
        return ref[pl.ds(h, ref.shape[0] // n_heads, stride=n_heads), :]
    return ref[:, sl]


def _kv_parts(refs, n_parts, h, n_heads, sl_k, sl_v):
    return [(_head_rows(refs[2 * p], h, n_heads, sl_k).astype(BF16),
             _head_rows(refs[2 * p + 1], h, n_heads, sl_v).astype(BF16)) for p in range(n_parts)]


def _da_attn_kernel(n_parts, lam_init, lam_ref, gsub_ref, q_ref, *refs):
    o_ref = refs[-1]
    tq = q_ref.shape[0]
    lv = lam_ref[...]
    lam = (jnp.exp(jnp.sum(lv[0:1] * lv[1:2], axis=-1, keepdims=True))
           - jnp.exp(jnp.sum(lv[2:3] * lv[3:4], axis=-1, keepdims=True)) + lam_init)
    lo = lax.broadcasted_iota(jnp.int32, (1, HEAD_W), 1) < DA_QK_DIM
    for h in range(DA_HEADS):
        sl = slice(h * HEAD_W, (h + 1) * HEAD_W)
        q = q_ref[:, sl]
        zero = jnp.zeros_like(q)
        q12 = jnp.concatenate([jnp.where(lo, q, zero), jnp.where(lo, zero, q)], axis=0)
        o12, d12 = _attn_rows(q12, _kv_parts(refs, n_parts, h, DA_HEADS, sl, sl))
        o12 = o12 * (1.0 / d12)
        o = o12[:tq] - lam * o12[tq:]
        y = o * lax.rsqrt(jnp.mean(o * o, axis=-1, keepdims=True) + EPS) * gsub_ref[...]
        o_ref[:, sl] = (y * (1.0 - lam_init)).astype(BF16)


def _mla_attn_kernel(n_parts, q_ref, *refs):
    o_ref = refs[-1]
    for h in range(MLA_HEADS):
        sl_k = slice(h * MLA_SLAB, (h + 1) * MLA_SLAB)
        sl_v = slice(h * MLA_V, (h + 1) * MLA_V)
        o, den = _attn_rows(q_ref[:, sl_k], _kv_parts(refs, n_parts, h, MLA_HEADS, sl_k, sl_v))
        o_ref[:, sl_v] = (o * (1.0 / den)).astype(BF16)


def _gq_attn_kernel(n_parts, seq, kw, sink_ref, q_ref, *refs):
    o_ref = refs[-1]
    tq = q_ref.shape[0]
    qi = pl.program_id(1)
    mask = None
    if n_parts == 2:
        start = pl.multiple_of(jnp.clip(qi * tq - WINDOW, 0, seq - kw), WINDOW)
        keys = start + lax.broadcasted_iota(jnp.int32, (kw, GQ_GROUP * tq), 0)
        qrows = qi * tq + (lax.broadcasted_iota(jnp.int32, (kw, GQ_GROUP * tq), 1) & (tq - 1))
        mask = jnp.abs(qrows - keys) <= WINDOW
    for g in range(GQ_KV_HEADS):
        sl = slice(g * GQ_DIM, (g + 1) * GQ_DIM)
        heads = range(g * GQ_GROUP, (g + 1) * GQ_GROUP)
        q4 = jnp.concatenate([q_ref[:, j * GQ_DIM:(j + 1) * GQ_DIM] for j in heads], axis=0)
        sink = jnp.concatenate([jnp.broadcast_to(sink_ref[j:j + 1, 0:1], (1, tq)) for j in heads], axis=1)
        if n_parts == 2:
            parts = [(_head_rows(refs[0], g, GQ_KV_HEADS, sl).astype(BF16),
                      _head_rows(refs[1], g, GQ_KV_HEADS, sl)),
                     (refs[2][pl.ds(start, kw), sl], refs[3][pl.ds(start, kw), sl])]
        else:
            parts = [(refs[0][:, sl], refs[1][:, sl])]
        o, den = _attn_cols(q4, parts, mask=mask, sink=sink)
        o = o * (1.0 / den)
        for n, j in enumerate(heads):
            o_ref[:, j * GQ_DIM:(j + 1) * GQ_DIM] = o[:, n * tq:(n + 1) * tq].T.astype(BF16)


def _per_sequence(kernel, sps, n_extra, n_parts):
    def wrapped(*refs):
        for s in range(sps):
            def rows(r):
                n = r.shape[0] // sps
                return r.at[pl.ds(s * n, n)]

            kernel(*refs[:n_extra], *[rows(r) for r in refs[n_extra:n_extra + 2 + 2 * n_parts]])

    return wrapped if sps > 1 else kernel


def _attention(kernel, name, q, kv_parts, trunk, tq, out_w, stacked_rows, extra_in=(), casts=(), sps=1):
    nq = trunk.seq // tq
    assert sps == 1 or nq == 1
    in_specs = [_const(a.shape) for a in extra_in]
    args = list(extra_in)
    qw = q.shape[1]
    in_specs.append(pl.BlockSpec((sps * tq, qw), lambda b, i: (b * nq + i, 0)))
    args.append(q)
    blocks = sps * (_nbytes((tq, qw), BF16) + _nbytes((tq, out_w), BF16))
    total_l = 0
    for k, v, l in kv_parts:
        in_specs += [pl.BlockSpec((sps * l, k.shape[1]), lambda b, i: (b, 0)),
                     pl.BlockSpec((sps * l, v.shape[1]), lambda b, i: (b, 0))]
        args += [k, v]
        blocks += sps * (_nbytes((l, k.shape[1]), k.dtype) + _nbytes((l, v.shape[1]), v.dtype))
        total_l += l
    out_specs = [pl.BlockSpec((sps * tq, out_w), lambda b, i: (b * nq + i, 0))]
    out_shape = [jax.ShapeDtypeStruct((trunk.rows, out_w), BF16)]
    kernel = _per_sequence(kernel, sps, len(extra_in), len(kv_parts))
    kernel, c_bytes = _add_side_casts(kernel, casts, trunk.batch // sps * nq, lambda b, i: b * nq + i,
                                      in_specs, args, out_specs, out_shape)
    blocks += c_bytes
    out = pl.pallas_call(
        kernel,
        grid=(trunk.batch // sps, nq),
        in_specs=in_specs,
        out_specs=out_specs,
        out_shape=out_shape,
        compiler_params=_params(("arbitrary",) * 2,
                                _vmem_limit(blocks, 0, 6 * _nbytes((stacked_rows, total_l), F32))),
        name=name,
    )(*args)
    return out if casts else out[0]


CAST_PLAN = {
    "front_ab": {"ff1_0": ("ff1_f32", 0), "w_out0": ("w_out0_f32", 0), "w_in1": ("w_in1_f32", 0)},
    "da_attn": {"ff2_0": ("ff2_f32", 0)},
    "mla_attn": {},
    "front_c": {"ff2_1": ("ff2_f32", 1), "w_out1": ("w_out1_f32", 0)},
    "gq_attn": {"ff1_1": ("ff1_f32", 1)},
}


def _run_trunk(x, trunk, tag, mod, P, ctx, wb):
    tq = min(trunk.seq, 256)
    tq0 = min(trunk.seq, 512)
    sps = 4 if trunk.seq == tq and trunk.batch % 4 == 0 else 1
    casting = wb is None
    wb = dict(wb or {})

    def jobs(call):
        return [(P[src], layer) for src, layer in CAST_PLAN[call].values()] if casting else []

    def split(call, outs, n_main):
        if not casting or not CAST_PLAN[call]:
            return outs
        for name, w in zip(CAST_PLAN[call], outs[n_main:]):
            wb[name] = w
        return outs[:n_main] if n_main > 1 else outs[0]

    n_front = 6 if trunk.rope else 10
    front = split("front_ab", _front_ab(x, P["norm1_g"][0], mod, trunk, P,
                                        P["tabs_ab"] if trunk.rope else None, tag,
                                        casts=jobs("front_ab")), n_front)
    qda, kda, vda, qm, km, vm = front[:6]
    da_parts, mla_parts = [(kda, vda, trunk.seq)], [(km, vm, trunk.seq)]
    if ctx is not None:
        da_parts = [(ctx["da_k"], ctx["da_v"], ctx["past"] * DA_HEADS)] + da_parts
        mla_parts = [(ctx["mla_k"], ctx["mla_v"], ctx["past"])] + mla_parts
    o_da = split("da_attn", _attention(functools.partial(_da_attn_kernel, len(da_parts), P["lam_init"]),
                                       f"da_attn_{tag}", qda, da_parts, trunk, tq0, DA_HEADS * HEAD_W,
                                       2 * tq0, extra_in=(P["lam4"], P["gsub"]), casts=jobs("da_attn"),
                                       sps=sps), 1)
    o_m = split("mla_attn", _attention(functools.partial(_mla_attn_kernel, len(mla_parts)),
                                       f"mla_attn_{tag}", qm, mla_parts, trunk, tq0, MLA_HEADS * MLA_V,
                                       tq0, casts=jobs("mla_attn"), sps=sps), 1)
    x, h = _outproj(x, P["norm2_g"][0], mod, 0, [(o_da, wb["w_out0"], 0), (o_m, wb["w_out0"], 1)],
                    trunk, tag)
    x = _ffn(x, h, mod, 0, wb["ff1_0"], wb["ff2_0"], trunk, tag)

    n_front_c = 3 if trunk.rope else 5
    front_c = split("front_c", _front_c(x, P["norm1_g"][1], mod, trunk, wb["w_in1"], P,
                                        P["tabs_gq"] if trunk.rope else None, tag,
                                        casts=jobs("front_c")), n_front_c)
    qc, kc, vc = front_c[:3]
    gq_parts = [(kc, vc, trunk.seq)]
    if ctx is not None:
        gq_parts = [(ctx["gq_k"], ctx["gq_v"], ctx["past"] * GQ_KV_HEADS)] + gq_parts
    kw = min(trunk.seq, tq + 2 * WINDOW)
    o_c = split("gq_attn", _attention(functools.partial(_gq_attn_kernel, len(gq_parts), trunk.seq, kw),
                                      f"gq_attn_{tag}", qc, gq_parts, trunk, tq, GQ_HEADS * GQ_DIM,
                                      GQ_GROUP * tq, extra_in=(P["sink"],), casts=jobs("gq_attn"),
                                      sps=sps), 1)
    x, h = _outproj(x, P["norm2_g"][1], mod, 1, [(o_c, wb["w_out1"], 0)], trunk, tag)
    x = _ffn(x, h, mod, 1, wb["ff1_1"], wb["ff2_1"], trunk, tag)
    return x, front[6:], front_c[3:], wb


def kernel(x_prompt, x_sample, cache_da_k, cache_da_v, cache_mla_ckv, cache_mla_krope, cache_gq_k, cache_gq_v, c, c_ctx, norm1_g, norm2_g, ada_w, ada_b, ff1_w, ff2_w, ab_w_in, ab_w_out, da_lambda_q1, da_lambda_k1, da_lambda_q2, da_lambda_k2, da_q_norm, da_k_norm, da_subln, mla_q_a_norm, mla_w_q_up, mla_kv_a_norm, mla_w_kv_up, mla_q_norm, mla_k_norm, c_w_in, c_w_out, gq_q_norm, gq_k_norm, gq_sink):
    pb, ps, d = x_prompt.shape
    sb, ss, _ = x_sample.shape
    past = cache_da_k.shape[2]
    assert sb + 1 <= 8 and d == D_MODEL

    cond8 = jnp.concatenate([c_ctx[None], c, jnp.zeros((8 - 1 - sb, d), F32)], axis=0)
    wq = jnp.pad(mla_w_q_up[0].reshape(MLA_RANK, MLA_HEADS, MLA_QK),
                 ((0, 0), (0, 0), (0, MLA_SLAB - MLA_QK))).reshape(MLA_RANK, MLA_HEADS * MLA_SLAB)
    wkv3 = mla_w_kv_up[0].reshape(MLA_RANK, MLA_HEADS, MLA_NOPE + MLA_V)
    wkv = jnp.concatenate([wkv3[..., :MLA_NOPE].reshape(MLA_RANK, -1),
                           wkv3[..., MLA_NOPE:].reshape(MLA_RANK, -1)], axis=1)
    P = {
        "norm1_g": norm1_g, "norm2_g": norm2_g,
        "w_in0": jnp.pad(ab_w_in[0], ((0, 0), (0, LANES - MLA_ROPE))).astype(BF16),
        "ff1_f32": ff1_w, "ff2_f32": ff2_w,
        "w_out0_f32": ab_w_out, "w_in1_f32": c_w_in, "w_out1_f32": c_w_out,
        "da_qn": jnp.tile(da_q_norm[0], 2).reshape(1, HEAD_W),
        "da_kn": jnp.tile(da_k_norm[0], 2).reshape(1, HEAD_W),
        "mq_norm": mla_q_a_norm[0].reshape(1, MLA_RANK),
        "mkv_norm": mla_kv_a_norm[0].reshape(1, MLA_RANK),
        "wq": wq.astype(BF16), "wkv": wkv.astype(BF16),
        "gq": jnp.pad(mla_q_norm[0], (0, MLA_SLAB - MLA_QK)).reshape(1, MLA_SLAB),
        "gkn": mla_k_norm[0, :MLA_NOPE].reshape(1, LANES),
        "gkr": jnp.pad(mla_k_norm[0, MLA_NOPE:], (0, LANES - MLA_ROPE)).reshape(1, LANES),
        "lam4": jnp.stack([da_lambda_q1[0], da_lambda_k1[0], da_lambda_q2[0], da_lambda_k2[0]]),
        "gsub": da_subln[0].reshape(1, HEAD_W),
        "lam_init": 0.8 - 0.6 * math.exp(-0.3 * 0),
        "gq_qn": gq_q_norm[0].reshape(1, GQ_DIM), "gq_kn": gq_k_norm[0].reshape(1, GQ_DIM),
        "sink": jnp.broadcast_to(gq_sink[0].reshape(GQ_HEADS, 1), (GQ_HEADS, LANES)),
        "tabs_ab": (_rope_tables(ss, [(32, "row"), (32, "col"), (32, "row"), (32, "col")])
                    + _rope_tables(ss, [(32, "row"), (32, "col"), (32, "none"), (32, "none")])),
        "tabs_gq": _rope_tables(ss, [(64, "row"), (64, "col")]),
    }

    mod = _ada_mod(cond8, ada_w, ada_b).reshape(2, 8, 1, 6 * d)

    kr_ctx = jnp.pad(cache_mla_krope[:, 0].reshape(sb * past, MLA_ROPE), ((0, 0), (0, LANES - MLA_ROPE)))
    mla_k_ctx, mla_v_ctx = _ctx_mla(cache_mla_ckv[:, 0].reshape(sb * past, MLA_RANK), kr_ctx, P)
    ctx = {
        "past": past,
        "da_k": cache_da_k[:, 0].reshape(-1, HEAD_W), "da_v": cache_da_v[:, 0].reshape(-1, HEAD_W),
        "mla_k": mla_k_ctx, "mla_v": mla_v_ctx,
        "gq_k": cache_gq_k[:, 0].reshape(-1, GQ_DIM), "gq_v": cache_gq_v[:, 0].reshape(-1, GQ_DIM),
    }

    prompt = Trunk(groups=1, seq=ps, batch=pb, mod_row0=0, rope=False)
    sample = Trunk(groups=sb, seq=ss, batch=sb, mod_row0=1, rope=True)
    y_p, (new_da_k, new_da_v, new_ckv, new_kr), (new_gq_k, new_gq_v), ffw = _run_trunk(
        x_prompt.reshape(pb * ps, d), prompt, "prompt", mod, P, None, None)
    y_s, _, _, _ = _run_trunk(x_sample.reshape(sb * ss, d), sample, "sample", mod, P, ctx, ffw)

    return (y_p.reshape(pb, ps, d), y_s.reshape(sb, ss, d),
            new_da_k.reshape(pb, 1, ps, DA_HEADS, HEAD_W), new_da_v.reshape(pb, 1, ps, DA_HEADS, HEAD_W),
            new_ckv.reshape(pb, 1, ps, MLA_RANK), new_kr.reshape(pb, 1, ps, MLA_ROPE),
            new_gq_k.reshape(pb, 1, ps, GQ_KV_HEADS, GQ_DIM), new_gq_v.reshape(pb, 1, ps, GQ_KV_HEADS, GQ_DIM))
```
